```python
import math
import jax, jax.numpy as jnp
from jax import lax
import numpy as np

D_MODEL = 1024
BATCH = 2
SEQ = 8192
DEPTH = 2

N_MIXERS = 2
RMS_EPS = 1e-6
NEG_INF = -1e30
A_HEADS = 16
A_KV_HEADS = 4
A_HEAD_DIM = D_MODEL // A_HEADS
A_WINDOW = 128
A_BLOCK = 128
A_IN = (A_HEADS + 2 * A_KV_HEADS) * A_HEAD_DIM
B_HEADS = 8
B_HEAD_DIM = D_MODEL // B_HEADS
B_CONV = 4
B_CHUNK = 64
B_IN = 4 * D_MODEL + 4 * B_HEADS
N_GROUPS = 4
EXPERTS_PER_GROUP = 8
N_EXPERTS = N_GROUPS * EXPERTS_PER_GROUP
TOP_K = 2
D_EXPERT = D_MODEL // 4

kernel_name = "hybrid_swa_gdn_hmoe_encoder"


def rmsnorm(x, g):
    xf = x.astype(jnp.float32)
    y = xf * lax.rsqrt(jnp.mean(xf * xf, axis=-1, keepdims=True) + RMS_EPS) * g.astype(jnp.float32)
    return y.astype(x.dtype)


def l2norm(x):
    return x * lax.rsqrt(jnp.sum(x * x, axis=-1, keepdims=True) + RMS_EPS)


def alibi_slopes(n_heads):
    return np.array([2.0 ** (-8.0 * (h + 1) / n_heads) for h in range(n_heads)], np.float32)


def windowed_attention(h, w_in, q_gain, k_gain, sink, w_out):
    B, S, _ = h.shape
    H, G, dh, BLK = A_HEADS, A_KV_HEADS, A_HEAD_DIM, A_BLOCK
    R = H // G
    nb = S // BLK
    proj = h @ w_in
    q, k, v = jnp.split(proj, [H * dh, (H + G) * dh], axis=-1)
    q = rmsnorm(q.reshape(B, S, H, dh), q_gain)
    k = rmsnorm(k.reshape(B, S, G, dh), k_gain)
    v = v.reshape(B, S, G, dh)
    qb = q.reshape(B, nb, BLK, G, R, dh)
    pad = ((0, 0), (BLK, BLK), (0, 0), (0, 0))
    kp = jnp.pad(k, pad).reshape(B, nb + 2, BLK, G, dh)
    vp = jnp.pad(v, pad).reshape(B, nb + 2, BLK, G, dh)
    kw = jnp.concatenate([kp[:, :-2], kp[:, 1:-1], kp[:, 2:]], axis=2)
    vw = jnp.concatenate([vp[:, :-2], vp[:, 1:-1], vp[:, 2:]], axis=2)
    scores = jnp.einsum('bnqgrd,bnkgd->bngrqk', qb, kw,
                        preferred_element_type=jnp.float32) * (dh ** -0.5)
    qi = np.arange(BLK)
    kj = np.arange(3 * BLK)
    dist = np.abs(BLK + qi[:, None] - kj[None, :]).astype(np.float32)
    s_abs = (np.arange(nb)[:, None] - 1) * BLK + kj[None, :]
    mask = (dist <= A_WINDOW)[None] & ((s_abs >= 0) & (s_abs < S))[:, None, :]
    slopes = alibi_slopes(H).reshape(G, R)
    alibi = -slopes[:, :, None, None] * dist[None, None]
    scores = jnp.where(mask[None, :, None, None], scores + alibi[None, None], NEG_INF)
    sink_b = sink.astype(jnp.float32).reshape(1, 1, G, R, 1, 1)
    m = jnp.maximum(jnp.max(scores, axis=-1, keepdims=True), sink_b)
    p = jnp.exp(scores - m)
    attn = p / (jnp.sum(p, axis=-1, keepdims=True) + jnp.exp(sink_b - m))
    out = jnp.einsum('bngrqk,bnkgd->bnqgrd', attn.astype(vw.dtype), vw)
    return out.reshape(B, S, H * dh) @ w_out


def short_conv(u, w):
    K, C = w.shape
    return lax.conv_general_dilated(u, w[:, None, :].astype(u.dtype), window_strides=(1,),
                                    padding=[(K // 2, K - 1 - K // 2)],
                                    dimension_numbers=('NWC', 'WIO', 'NWC'),
                                    feature_group_count=C)


def gated_delta_chunked(q, k, v, g, beta):
    B, H, S, dk = q.shape
    dv = v.shape[-1]
    C = B_CHUNK
    nc = S // C
    q = q * (dk ** -0.5)
    q, k, v = (t.reshape(B, H, nc, C, t.shape[-1]) for t in (q, k, v))
    g = g.reshape(B, H, nc, C)
    beta = beta.reshape(B, H, nc, C)
    gc = jnp.cumsum(g, axis=-1)
    tril = jnp.tril(jnp.ones((C, C), bool))
    strict = jnp.tril(jnp.ones((C, C), bool), -1)
    diff = gc[..., :, None] - gc[..., None, :]
    decay = jnp.where(tril, jnp.exp(jnp.where(tril, diff, 0.0)), 0.0)
    kb = k * beta[..., None]
    L = jnp.where(strict, jnp.einsum('bhnid,bhnjd->bhnij', kb, k) * decay, 0.0)
    a_mat = L + jnp.eye(C, dtype=L.dtype)
    rhs = jnp.concatenate([v * beta[..., None], kb * jnp.exp(gc)[..., None]], axis=-1)
    sol = lax.linalg.triangular_solve(a_mat, rhs, left_side=True, lower=True, unit_diagonal=True)
    u, w = sol[..., :dv], sol[..., dv:]
    intra = jnp.where(tril, jnp.einsum('bhnid,bhnjd->bhnij', q, k) * decay, 0.0)
    q_dec = q * jnp.exp(gc)[..., None]
    k_dec = k * jnp.exp(gc[..., -1:] - gc)[..., None]
    chunk_decay = jnp.exp(gc[..., -1])

    def step(state, xs):
        q_i, k_i, u_i, w_i, a_i, d_i = xs
        v_new = u_i - jnp.einsum('bhck,bhkv->bhcv', w_i, state)
        o_i = jnp.einsum('bhck,bhkv->bhcv', q_i, state) + jnp.einsum('bhcs,bhsv->bhcv', a_i, v_new)
        state = state * d_i[..., None, None] + jnp.einsum('bhck,bhcv->bhkv', k_i, v_new)
        return state, o_i

    xs = tuple(jnp.moveaxis(t, 2, 0) for t in (q_dec, k_dec, u, w, intra, chunk_decay))
    state0 = jnp.zeros((B, H, dk, dv), jnp.float32)
    _, o = lax.scan(step, state0, xs)
    return jnp.moveaxis(o, 0, 2).reshape(B, H, S, dv)


def gated_deltanet(h, w_in, conv_w, a_log, dt_bias, o_gain, w_out):
    B, S, D = h.shape
    H, dk = B_HEADS, B_HEAD_DIM
    proj = h @ w_in
    qkv, z, a, b = jnp.split(proj, [3 * D, 4 * D, 4 * D + 2 * H], axis=-1)
    qkv = jax.nn.silu(short_conv(qkv, conv_w))
    q, k, v = jnp.split(qkv, 3, axis=-1)

    def heads(t):
        return t.reshape(B, S, H, dk).transpose(0, 2, 1, 3).astype(jnp.float32)

    q, k, v = l2norm(heads(q)), l2norm(heads(k)), heads(v)
    a = a.astype(jnp.float32).reshape(B, S, 2, H).transpose(2, 0, 3, 1)
    b = b.astype(jnp.float32).reshape(B, S, 2, H).transpose(2, 0, 3, 1)
    A = jnp.exp(a_log.astype(jnp.float32))[:, None, :, None]
    g = -A * jax.nn.softplus(a + dt_bias.astype(jnp.float32)[:, None, :, None])
    beta = jax.nn.sigmoid(b)
    o_fwd = gated_delta_chunked(q, k, v, g[0], beta[0])

    def flip(t):
        return jnp.flip(t, axis=2)

    o_bwd = flip(gated_delta_chunked(flip(q), flip(k), flip(v), flip(g[1]), flip(beta[1])))
    o = (o_fwd + o_bwd).transpose(0, 2, 1, 3)
    o = rmsnorm(o, o_gain) * jax.nn.silu(z.reshape(B, S, H, dk).astype(jnp.float32))
    return o.reshape(B, S, D).astype(h.dtype) @ w_out


def hierarchical_moe(h, w_group, b_group, w_expert, b_expert, w_gate, w_up, w_down):
    B, S, D = h.shape
    t = h.reshape(B * S, D)
    g_logits = (t @ w_group).astype(jnp.float32) + b_group.astype(jnp.float32)
    g_prob, g_idx = lax.top_k(jax.nn.softmax(g_logits, axis=-1), 1)
    e_logits = ((t @ w_expert).astype(jnp.float32) + b_expert.astype(jnp.float32)
                ).reshape(-1, N_GROUPS, EXPERTS_PER_GROUP)
    e_sel = jnp.take_along_axis(e_logits, g_idx[:, :, None], axis=1)[:, 0]
    e_prob, e_idx = lax.top_k(jax.nn.softmax(e_sel, axis=-1), TOP_K)
    gate = g_prob * e_prob / jnp.sum(e_prob, axis=-1, keepdims=True)
    expert_id = g_idx * EXPERTS_PER_GROUP + e_idx
    combine = jnp.sum(jax.nn.one_hot(expert_id, N_EXPERTS, dtype=jnp.float32) * gate[..., None], axis=1)
    hid = jax.nn.silu(jnp.einsum('td,edf->tef', t, w_gate)) * jnp.einsum('td,edf->tef', t, w_up)
    hid = hid * combine[..., None].astype(hid.dtype)
    return jnp.einsum('tef,efd->td', hid, w_down).reshape(B, S, D)


def setup_inputs(seed: int = 0) -> dict:
    key = jax.random.key(seed)
    ks = jax.random.split(key, 24)
    n_a = (DEPTH + 1) // 2
    n_b = DEPTH // 2
    f32 = jnp.float32

    def nrm(k, shape, fan_in):
        return jax.random.normal(k, shape, f32) * fan_in ** -0.5

    def gain(k, shape):
        return 1.0 + 0.02 * jax.random.normal(k, shape, f32)

    dt = jnp.exp(jax.random.uniform(ks[12], (n_b, 2, B_HEADS), f32, math.log(1e-3), math.log(1e-1)))
    return {
        "x": jax.random.normal(ks[0], (BATCH, SEQ, D_MODEL), f32),
        "norm_mix": gain(ks[1], (DEPTH, D_MODEL)),
        "norm_ffn": gain(ks[2], (DEPTH, D_MODEL)),
        "attn_w_in": nrm(ks[3], (n_a, D_MODEL, A_IN), D_MODEL),
        "attn_q_gain": gain(ks[4], (n_a, A_HEAD_DIM)),
        "attn_k_gain": gain(ks[5], (n_a, A_HEAD_DIM)),
        "attn_sink": 0.5 * jax.random.normal(ks[6], (n_a, A_HEADS), f32),
        "attn_w_out": nrm(ks[7], (n_a, A_HEADS * A_HEAD_DIM, D_MODEL), A_HEADS * A_HEAD_DIM),
        "gdn_w_in": nrm(ks[8], (n_b, D_MODEL, B_IN), D_MODEL),
        "gdn_conv": nrm(ks[9], (n_b, B_CONV, 3 * D_MODEL), B_CONV),
        "gdn_a_log": jnp.log(jax.random.uniform(ks[10], (n_b, 2, B_HEADS), f32, 1.0, 16.0)),
        "gdn_dt_bias": dt + jnp.log(-jnp.expm1(-dt)),
        "gdn_o_gain": gain(ks[11], (n_b, B_HEAD_DIM)),
        "gdn_w_out": nrm(ks[13], (n_b, D_MODEL, D_MODEL), D_MODEL),
        "moe_w_group": nrm(ks[14], (DEPTH, D_MODEL, N_GROUPS), D_MODEL),
        "moe_b_group": 0.01 * jax.random.normal(ks[15], (DEPTH, N_GROUPS), f32),
        "moe_w_expert": nrm(ks[16], (DEPTH, D_MODEL, N_EXPERTS), D_MODEL),
        "moe_b_expert": 0.01 * jax.random.normal(ks[17], (DEPTH, N_EXPERTS), f32),
        "moe_w_gate": nrm(ks[18], (DEPTH, N_EXPERTS, D_MODEL, D_EXPERT), D_MODEL),
        "moe_w_up": nrm(ks[19], (DEPTH, N_EXPERTS, D_MODEL, D_EXPERT), D_MODEL),
        "moe_w_down": nrm(ks[20], (DEPTH, N_EXPERTS, D_EXPERT, D_MODEL), D_EXPERT),
    }


def reference(x, norm_mix, norm_ffn, attn_w_in, attn_q_gain, attn_k_gain, attn_sink, attn_w_out,
              gdn_w_in, gdn_conv, gdn_a_log, gdn_dt_bias, gdn_o_gain, gdn_w_out,
              moe_w_group, moe_b_group, moe_w_expert, moe_b_expert, moe_w_gate, moe_w_up, moe_w_down):
    for i in range(DEPTH):
        h = rmsnorm(x, norm_mix[i])
        j = i // N_MIXERS
        if i % N_MIXERS == 0:
            x = x + windowed_attention(h, attn_w_in[j], attn_q_gain[j], attn_k_gain[j],
                                       attn_sink[j], attn_w_out[j])
        else:
            x = x + gated_deltanet(h, gdn_w_in[j], gdn_conv[j], gdn_a_log[j], gdn_dt_bias[j],
                                   gdn_o_gain[j], gdn_w_out[j])
        h = rmsnorm(x, norm_ffn[i])
        x = x + hierarchical_moe(h, moe_w_group[i], moe_b_group[i], moe_w_expert[i], moe_b_expert[i],
                                 moe_w_gate[i], moe_w_up[i], moe_w_down[i])
    return x
```

```python
import functools
import math

import jax
import jax.numpy as jnp
import numpy as np
from jax import lax
from jax.experimental import pallas as pl
from jax.experimental.pallas import tpu as pltpu

RMS_EPS = 1e-6
NEG_INF = -1e30
F32 = jnp.float32
BF16 = jnp.bfloat16

A_HEADS = 16
A_KV_HEADS = 4
A_HEAD_DIM = 64
A_REP = A_HEADS // A_KV_HEADS
A_BLOCK = 128
B_HEADS = 8
B_HEAD_DIM = 128
B_CONV = 4
GDN_CHUNK = 64
N_GROUPS = 4
EXPERTS_PER_GROUP = 8
N_EXPERTS = 32
D_EXPERT = 256
ROUTER_LANES = 128

V7X_VMEM_LIMIT_BYTES = 56 * 1024 * 1024


def _cparams(*sem):
    return pltpu.CompilerParams(dimension_semantics=sem, vmem_limit_bytes=V7X_VMEM_LIMIT_BYTES)


def _bdot(a, b):
    return jnp.dot(a.astype(BF16), b.astype(BF16), preferred_element_type=F32)


def _bdot_nt(a, b):
    return lax.dot_general(a.astype(BF16), b.astype(BF16), (((1,), (1,)), ((), ())),
                           preferred_element_type=F32)


def _bdot_tn(a, b):
    return lax.dot_general(a.astype(BF16), b.astype(BF16), (((0,), (0,)), ((), ())),
                           preferred_element_type=F32)


def _split_bf16(a):
    hi = a.astype(BF16)
    lo = (a - hi.astype(F32)).astype(BF16)
    return hi, lo


def _rms(x, gain):
    return x * lax.rsqrt(jnp.mean(x * x, axis=-1, keepdims=True) + RMS_EPS) * gain


def _norm_matmul_kernel(x_ref, g_ref, w_ref, o_ref, xn_ref):
    @pl.when(pl.program_id(1) == 0)
    def _():
        xn_ref[...] = _rms(x_ref[...], g_ref[...]).astype(BF16)

    o_ref[...] = jnp.dot(xn_ref[...], w_ref[...], preferred_element_type=F32)


def _norm_matmul(x, gain, w, tm, tn):
    T, D = x.shape
    N = w.shape[1]
    return pl.pallas_call(
        _norm_matmul_kernel,
        grid=(T // tm, N // tn),
        in_specs=[pl.BlockSpec((tm, D), lambda i, j: (i, 0)),
                  pl.BlockSpec((1, D), lambda i, j: (0, 0)),
                  pl.BlockSpec((D, tn), lambda i, j: (0, j))],
        out_specs=pl.BlockSpec((tm, tn), lambda i, j: (i, j)),
        out_shape=jax.ShapeDtypeStruct((T, N), F32),
        scratch_shapes=[pltpu.VMEM((tm, D), BF16)],
        compiler_params=_cparams("parallel", "arbitrary"),
        name="norm_matmul",
    )(x, gain.reshape(1, D), w)


def _norm_matmul2_kernel(x_ref, g_ref, w_ref, w2_ref, o_ref, o2_ref, xn_ref):
    @pl.when(pl.program_id(1) == 0)
    def _():
        xn = _rms(x_ref[...], g_ref[...])
        xn_ref[...] = xn.astype(BF16)
        x_hi, x_lo = _split_bf16(xn)
        w_hi, w_lo = _split_bf16(w2_ref[...])
        o2_ref[...] = (jnp.dot(x_hi, w_hi, preferred_element_type=F32)
                       + jnp.dot(x_lo, w_hi, preferred_element_type=F32)
                       + jnp.dot(x_hi, w_lo, preferred_element_type=F32))

    o_ref[...] = jnp.dot(xn_ref[...], w_ref[...], preferred_element_type=F32)


def _norm_matmul2(x, gain, w, w2, tm, tn):
    T, D = x.shape
    N = w.shape[1]
    N2 = w2.shape[1]
    return pl.pallas_call(
        _norm_matmul2_kernel,
        grid=(T // tm, N // tn),
        in_specs=[pl.BlockSpec((tm, D), lambda i, j: (i, 0)),
                  pl.BlockSpec((1, D), lambda i, j: (0, 0)),
                  pl.BlockSpec((D, tn), lambda i, j: (0, j)),
                  pl.BlockSpec((D, N2), lambda i, j: (0, 0))],
        out_specs=[pl.BlockSpec((tm, tn), lambda i, j: (i, j)),
                   pl.BlockSpec((tm, N2), lambda i, j: (i, 0))],
        out_shape=[jax.ShapeDtypeStruct((T, N), F32), jax.ShapeDtypeStruct((T, N2), F32)],
        scratch_shapes=[pltpu.VMEM((tm, D), BF16)],
        compiler_params=_cparams("parallel", "arbitrary"),
        name="norm_matmul2",
    )(x, gain.reshape(1, D), w, w2)


def _matmul_residual_kernel(a_ref, w_ref, x_ref, o_ref):
    o_ref[...] = x_ref[...] + jnp.dot(a_ref[...], w_ref[...], preferred_element_type=F32)


def _matmul_residual(a, w, x, tm, tn):
    T, K = a.shape
    N = w.shape[1]
    return pl.pallas_call(
        _matmul_residual_kernel,
        grid=(T // tm, N // tn),
        in_specs=[pl.BlockSpec((tm, K), lambda i, j: (i, 0)),
                  pl.BlockSpec((K, tn), lambda i, j: (0, j)),
                  pl.BlockSpec((tm, tn), lambda i, j: (i, j))],
        out_specs=pl.BlockSpec((tm, tn), lambda i, j: (i, j)),
        out_shape=jax.ShapeDtypeStruct((T, N), F32),
        compiler_params=_cparams("parallel", "arbitrary"),
        name="matmul_residual",
    )(a, w, x)


def _attn_kernel(main_ref, prev_ref, next_ref, qg_ref, kg_ref, sink_ref, bias_ref, o_ref):
    n = pl.program_id(1)
    nb = pl.num_programs(1)
    dh, blk = A_HEAD_DIM, A_BLOCK
    kv_cols = A_KV_HEADS * dh
    main = main_ref[...]
    kv = jnp.concatenate([prev_ref[...], main[:, A_HEADS * dh:], next_ref[...]], axis=0)
    col = lax.broadcasted_iota(jnp.int32, (1, 3 * blk), 1)
    outside = ((col < blk) & (n == 0)) | ((col >= 2 * blk) & (n == nb - 1))
    edge = jnp.where(outside, NEG_INF, 0.0)
    qg = qg_ref[...] * (dh ** -0.5)
    for g in range(A_KV_HEADS):
        kn = _rms(kv[:, g * dh:(g + 1) * dh], kg_ref[...]).astype(BF16)
        vg = kv[:, kv_cols + g * dh: kv_cols + (g + 1) * dh].astype(BF16)
        qs = [_rms(main[:, (A_REP * g + r) * dh:(A_REP * g + r + 1) * dh], qg).astype(BF16)
              for r in range(A_REP)]
        q4 = jnp.concatenate(qs, axis=0)
        s = _bdot_nt(q4, kn) + bias_ref[g] + edge
        sk = sink_ref[g]
        m = jnp.maximum(jnp.max(s, axis=-1, keepdims=True), sk)
        p = jnp.exp(s - m)
        denom = jnp.sum(p, axis=-1, keepdims=True) + jnp.exp(sk - m)
        o = jnp.dot(p.astype(BF16), vg, preferred_element_type=F32) / denom
        for r in range(A_REP):
            h = A_REP * g + r
            o_ref[:, h * dh:(h + 1) * dh] = o[r * blk:(r + 1) * blk].astype(o_ref.dtype)


def _attn_tables(sink):
    blk = A_BLOCK
    slopes = np.array([2.0 ** (-8.0 * (h + 1) / A_HEADS) for h in range(A_HEADS)], np.float32)
    qi = np.arange(blk)
    kj = np.arange(3 * blk)
    dist = np.abs(blk + qi[:, None] - kj[None, :]).astype(np.float32)
    bias = np.where(dist[None] <= blk, -slopes[:, None, None] * dist[None], np.float32(NEG_INF))
    bias = bias.astype(np.float32).reshape(A_KV_HEADS, A_REP * blk, 3 * blk)
    sink_rows = jnp.repeat(sink.astype(F32).reshape(A_KV_HEADS, A_REP), blk, axis=1)[..., None]
    return jnp.asarray(bias), sink_rows


def _attention(qkv, q_gain, k_gain, sink, batch, seq):
    T, W = qkv.shape
    blk, dh = A_BLOCK, A_HEAD_DIM
    nb = seq // blk
    kv_w = 2 * A_KV_HEADS * dh
    kv_blk = (A_HEADS * dh) // kv_w
    bias, sink_rows = _attn_tables(sink)
    return pl.pallas_call(
        _attn_kernel,
        grid=(batch, nb),
        in_specs=[
            pl.BlockSpec((blk, W), lambda b, n: (b * nb + n, 0)),
            pl.BlockSpec((blk, kv_w), lambda b, n: (b * nb + jnp.maximum(n - 1, 0), kv_blk)),
            pl.BlockSpec((blk, kv_w), lambda b, n: (b * nb + jnp.minimum(n + 1, nb - 1), kv_blk)),
            pl.BlockSpec((1, dh), lambda b, n: (0, 0)),
            pl.BlockSpec((1, dh), lambda b, n: (0, 0)),
            pl.BlockSpec((A_KV_HEADS, A_REP * blk, 1), lambda b, n: (0, 0, 0)),
            pl.BlockSpec((A_KV_HEADS, A_REP * blk, 3 * blk), lambda b, n: (0, 0, 0)),
        ],
        out_specs=pl.BlockSpec((blk, A_HEADS * dh), lambda b, n: (b * nb + n, 0)),
        out_shape=jax.ShapeDtypeStruct((T, A_HEADS * dh), BF16),
        compiler_params=_cparams("parallel", "parallel"),
        name="window_attention",
    )(qkv, qkv, qkv, q_gain.reshape(1, dh), k_gain.reshape(1, dh), sink_rows, bias)


def _route(logits):
    lane = lax.broadcasted_iota(jnp.int32, logits.shape, 1).astype(F32)
    big = jnp.float32(1e9)
    is_g = (lane >= N_EXPERTS) & (lane < N_EXPERTS + N_GROUPS)
    lg = jnp.where(is_g, logits, NEG_INF)
    gmax = jnp.max(lg, axis=-1, keepdims=True)
    gidx = jnp.min(jnp.where(is_g & (lg == gmax), lane, big), axis=-1, keepdims=True) - N_EXPERTS
    g_prob = 1.0 / jnp.sum(jnp.where(is_g, jnp.exp(lg - gmax), 0.0), axis=-1, keepdims=True)
    lo = gidx * EXPERTS_PER_GROUP
    in_grp = (lane >= lo) & (lane < lo + EXPERTS_PER_GROUP)
    le = jnp.where(in_grp, logits, NEG_INF)
    emax = jnp.max(le, axis=-1, keepdims=True)
    ex = jnp.where(in_grp, jnp.exp(le - emax), 0.0)
    prob = ex / jnp.sum(ex, axis=-1, keepdims=True)
    cand = jnp.where(in_grp, prob, -1.0)
    p1 = jnp.max(cand, axis=-1, keepdims=True)
    i1 = jnp.min(jnp.where(cand == p1, lane, big), axis=-1, keepdims=True)
    cand2 = jnp.where(lane == i1, -1.0, cand)
    p2 = jnp.max(cand2, axis=-1, keepdims=True)
    i2 = jnp.min(jnp.where(cand2 == p2, lane, big), axis=-1, keepdims=True)
    scale = g_prob / (p1 + p2)
    return jnp.where(lane == i1, p1 * scale, 0.0) + jnp.where(lane == i2, p2 * scale, 0.0)


def _moe_kernel(x_ref, g_ref, wr_ref, br_ref, wgu_ref, wd_ref, o_ref, h_ref, comb_ref, acc_ref):
    e = pl.program_id(1)

    @pl.when(e == 0)
    def _():
        hn = _rms(x_ref[...], g_ref[...])
        h_hi, h_lo = _split_bf16(hn)
        w_hi, w_lo = _split_bf16(wr_ref[...])
        h_ref[...] = h_hi
        logits = (jnp.dot(h_hi, w_hi, preferred_element_type=F32)
                  + jnp.dot(h_lo, w_hi, preferred_element_type=F32)
                  + jnp.dot(h_hi, w_lo, preferred_element_type=F32)) + br_ref[...]
        comb_ref[...] = _route(logits)
        acc_ref[...] = jnp.zeros_like(acc_ref)

    gu = jnp.dot(h_ref[...], wgu_ref[0], preferred_element_type=F32)
    gate, up = gu[:, :D_EXPERT], gu[:, D_EXPERT:]
    lane = lax.broadcasted_iota(jnp.int32, comb_ref.shape, 1)
    c = jnp.sum(jnp.where(lane == e, comb_ref[...], 0.0), axis=-1, keepdims=True)
    hid = (gate * jax.nn.sigmoid(gate)) * up * c
    acc_ref[...] += jnp.dot(hid.astype(BF16), wd_ref[0], preferred_element_type=F32)

    @pl.when(e == pl.num_programs(1) - 1)
    def _():
        o_ref[...] = x_ref[...] + acc_ref[...]


def _moe(x, gain, w_group, b_group, w_expert, b_expert, w_gate, w_up, w_down, tm):
    T, D = x.shape
    pad = ROUTER_LANES - N_EXPERTS - N_GROUPS
    wr = jnp.concatenate([w_expert, w_group, jnp.zeros((D, pad), F32)], axis=1)
    br = jnp.concatenate([b_expert, b_group, jnp.zeros((pad,), F32)]).reshape(1, ROUTER_LANES)
    wgu = jnp.concatenate([w_gate, w_up], axis=-1).astype(BF16)
    wd = w_down.astype(BF16)
    return pl.pallas_call(
        _moe_kernel,
        grid=(T // tm, N_EXPERTS),
        in_specs=[pl.BlockSpec((tm, D), lambda i, e: (i, 0)),
                  pl.BlockSpec((1, D), lambda i, e: (0, 0)),
                  pl.BlockSpec((D, ROUTER_LANES), lambda i, e: (0, 0)),
                  pl.BlockSpec((1, ROUTER_LANES), lambda i, e: (0, 0)),
                  pl.BlockSpec((1, D, 2 * D_EXPERT), lambda i, e: (e, 0, 0)),
                  pl.BlockSpec((1, D_EXPERT, D), lambda i, e: (e, 0, 0))],
        out_specs=pl.BlockSpec((tm, D), lambda i, e: (i, 0)),
        out_shape=jax.ShapeDtypeStruct((T, D), F32),
        scratch_shapes=[pltpu.VMEM((tm, D), BF16),
                        pltpu.VMEM((tm, ROUTER_LANES), F32),
                        pltpu.VMEM((tm, D), F32)],
        compiler_params=_cparams("parallel", "arbitrary"),
        name="moe",
    )(x, gain.reshape(1, D), wr, br, wgu, wd)


GDN_HALO = 8


def _gdn_conv_kernel(u_ref, up_ref, un_ref, w_ref, ab_ref, aexp_ref, dtb_ref, o_ref, gb_ref, *, tiles_per_seq):
    i = pl.program_id(0)
    j = pl.program_id(1)
    tm = u_ref.shape[0]
    first = (i % tiles_per_seq) == 0
    last = (i % tiles_per_seq) == tiles_per_seq - 1
    u = u_ref[...]
    prev = jnp.where(first, 0.0, up_ref[...])
    nxt = jnp.where(last, 0.0, un_ref[...])
    ext = jnp.concatenate([prev, u, nxt], axis=0)
    w = w_ref[...]
    h0 = GDN_HALO - B_CONV // 2
    y = w[0:1] * ext[h0:h0 + tm]
    for t in range(1, B_CONV):
        y = y + w[t:t + 1] * ext[h0 + t:h0 + t + tm]
    y = y * jax.nn.sigmoid(y)
    qscale = jnp.where(j == 0, B_HEAD_DIM ** -0.5, 1.0)
    for h in range(B_HEADS):
        slab = y[:, h * B_HEAD_DIM:(h + 1) * B_HEAD_DIM]
        inv = lax.rsqrt(jnp.sum(slab * slab, axis=-1, keepdims=True) + RMS_EPS) * qscale
        o_ref[0, :, h * B_HEAD_DIM:(h + 1) * B_HEAD_DIM] = slab * jnp.where(j < 2, inv, 1.0)

    @pl.when(j == 0)
    def _():
        ab = ab_ref[...]
        lane = lax.broadcasted_iota(jnp.int32, ab.shape, 1)
        z = ab + dtb_ref[...]
        softplus = jnp.maximum(z, 0.0) + jnp.log1p(jnp.exp(-jnp.abs(z)))
        gb_ref[...] = jnp.where(lane < 2 * B_HEADS, -aexp_ref[...] * softplus, jax.nn.sigmoid(ab))


def _gdn_conv(qkvz, ab, conv_w, a_log, dt_bias, seq, tm):
    T = qkvz.shape[0]
    D = B_HEADS * B_HEAD_DIM
    hb = tm // GDN_HALO
    n_halo = T // GDN_HALO
    pad = ROUTER_LANES - 2 * B_HEADS
    aexp = jnp.concatenate([jnp.exp(a_log.astype(F32)).reshape(-1), jnp.zeros((pad,), F32)]).reshape(1, -1)
    dtb = jnp.concatenate([dt_bias.astype(F32).reshape(-1), jnp.zeros((pad,), F32)]).reshape(1, -1)
    return pl.pallas_call(
        functools.partial(_gdn_conv_kernel, tiles_per_seq=seq // tm),
        grid=(T // tm, 3),
        in_specs=[pl.BlockSpec((tm, D), lambda i, j: (i, j)),
                  pl.BlockSpec((GDN_HALO, D), lambda i, j: (jnp.maximum(i * hb - 1, 0), j)),
                  pl.BlockSpec((GDN_HALO, D), lambda i, j: (jnp.minimum((i + 1) * hb, n_halo - 1), j)),
                  pl.BlockSpec((B_CONV, D), lambda i, j: (0, j)),
                  pl.BlockSpec((tm, ROUTER_LANES), lambda i, j: (i, 0)),
                  pl.BlockSpec((1, ROUTER_LANES), lambda i, j: (0, 0)),
                  pl.BlockSpec((1, ROUTER_LANES), lambda i, j: (0, 0))],
        out_specs=[pl.BlockSpec((1, tm, D), lambda i, j: (j, i, 0)),
                   pl.BlockSpec((tm, ROUTER_LANES), lambda i, j: (i, 0))],
        out_shape=[jax.ShapeDtypeStruct((3, T, D), F32), jax.ShapeDtypeStruct((T, ROUTER_LANES), F32)],
        compiler_params=_cparams("parallel", "arbitrary"),
        name="gdn_conv",
    )(qkvz, qkvz, qkvz, conv_w, ab, aexp, dtb)


def _inv_unit_triangular(L, eye):
    C = L.shape[0]
    x = -L
    p = eye + x
    n = 1
    while 2 * n < C:
        x = _bdot(x, x)
        n *= 2
        p = p + _bdot(p, x)
    return p


def _gdn_chunk(q, k, v, gb, s_ref, o_ref, reverse):
    C = q.shape[0]
    dk = B_HEAD_DIM
    row = lax.broadcasted_iota(jnp.int32, (C, C), 0)
    colm = lax.broadcasted_iota(jnp.int32, (C, C), 1)
    incl = (row <= colm) if reverse else (row >= colm)
    strict = (row < colm) if reverse else (row > colm)
    eye = (row == colm).astype(F32)
    tri = incl.astype(BF16)
    lane = lax.broadcasted_iota(jnp.int32, gb.shape, 1)
    g_hi, g_lo = _split_bf16(jnp.where(lane < 2 * B_HEADS, gb, 0.0))
    gc = jnp.dot(tri, g_hi, preferred_element_type=F32) + jnp.dot(tri, g_lo, preferred_element_type=F32)
    gct = gc.T
    glast = gc[0:1] if reverse else gc[C - 1:C]
    e_gc = jnp.exp(gc)
    e_rem = jnp.exp(glast - gc)
    e_all = jnp.exp(glast)
    dlane = B_HEADS if reverse else 0
    for h in range(B_HEADS):
        r = dlane + h
        gcol, grow = gc[:, r:r + 1], gct[r:r + 1, :]
        beta = gb[:, 2 * B_HEADS + r:2 * B_HEADS + r + 1]
        dec = jnp.exp(jnp.where(incl, gcol - grow, NEG_INF))
        qh, kh, vh = (t[:, h * dk:(h + 1) * dk] for t in (q, k, v))
        kb = kh * beta
        a = _bdot_nt(jnp.concatenate([kb, qh], axis=0), kh)
        L = jnp.where(strict, a[:C] * dec, 0.0)
        intra = a[C:] * dec
        tinv = _inv_unit_triangular(L, eye)
        sol = _bdot(tinv, jnp.concatenate([vh * beta, kb * e_gc[:, r:r + 1]], axis=1))
        u, w = sol[:, :dk], sol[:, dk:]
        state = s_ref[h]
        ws_qs = _bdot(jnp.concatenate([w, qh * e_gc[:, r:r + 1]], axis=0), state)
        v_new = u - ws_qs[:C]
        o_ref[:, h * dk:(h + 1) * dk] = ws_qs[C:] + _bdot(intra, v_new)
        s_ref[h] = state * e_all[0:1, r:r + 1] + _bdot_tn(kh * e_rem[:, r:r + 1], v_new)


def _gdn_scan_kernel(qf_ref, kf_ref, vf_ref, gf_ref, qb_ref, kb_ref, vb_ref, gbw_ref, of_ref, ob_ref, sf_ref, sb_ref):
    @pl.when(pl.program_id(1) == 0)
    def _():
        sf_ref[...] = jnp.zeros_like(sf_ref)
        sb_ref[...] = jnp.zeros_like(sb_ref)

    _gdn_chunk(qf_ref[0], kf_ref[0], vf_ref[0], gf_ref[...], sf_ref, of_ref, reverse=False)
    _gdn_chunk(qb_ref[0], kb_ref[0], vb_ref[0], gbw_ref[...], sb_ref, ob_ref, reverse=True)


def _gdn_scan(qkv, gb, batch, seq, chunk):
    _, T, D = qkv.shape
    nc = seq // chunk
    fwd = lambda b, c: b * nc + c
    bwd = lambda b, c: b * nc + (nc - 1 - c)
    part = lambda p, f: pl.BlockSpec((1, chunk, D), lambda b, c: (p, f(b, c), 0))
    gspec = lambda f: pl.BlockSpec((chunk, ROUTER_LANES), lambda b, c: (f(b, c), 0))
    ospec = lambda f: pl.BlockSpec((chunk, D), lambda b, c: (f(b, c), 0))
    return pl.pallas_call(
        _gdn_scan_kernel,
        grid=(batch, nc),
        in_specs=[part(0, fwd), part(1, fwd), part(2, fwd), gspec(fwd),
                  part(0, bwd), part(1, bwd), part(2, bwd), gspec(bwd)],
        out_specs=[ospec(fwd), ospec(bwd)],
        out_shape=[jax.ShapeDtypeStruct((T, D), F32), jax.ShapeDtypeStruct((T, D), F32)],
        scratch_shapes=[pltpu.VMEM((B_HEADS, B_HEAD_DIM, B_HEAD_DIM), F32),
                        pltpu.VMEM((B_HEADS, B_HEAD_DIM, B_HEAD_DIM), F32)],
        compiler_params=_cparams("parallel", "arbitrary"),
        name="gdn_scan",
    )(qkv, qkv, qkv, gb, qkv, qkv, qkv, gb)


def _gdn_out_kernel(of_ref, ob_ref, z_ref, og_ref, w_ref, x_ref, o_ref, a_ref):
    @pl.when(pl.program_id(1) == 0)
    def _():
        o = of_ref[...] + ob_ref[...]
        z = z_ref[...]
        for h in range(B_HEADS):
            sl = slice(h * B_HEAD_DIM, (h + 1) * B_HEAD_DIM)
            zh = z[:, sl]
            a_ref[:, sl] = (_rms(o[:, sl], og_ref[...]) * (zh * jax.nn.sigmoid(zh))).astype(BF16)

    o_ref[...] = x_ref[...] + jnp.dot(a_ref[...], w_ref[...], preferred_element_type=F32)


def _gdn_out(o_f, o_b, qkvz, o_gain, w, x, tm, tn):
    T, D = x.shape
    return pl.pallas_call(
        _gdn_out_kernel,
        grid=(T // tm, D // tn),
        in_specs=[pl.BlockSpec((tm, D), lambda i, j: (i, 0)),
                  pl.BlockSpec((tm, D), lambda i, j: (i, 0)),
                  pl.BlockSpec((tm, D), lambda i, j: (i, 3)),
                  pl.BlockSpec((1, B_HEAD_DIM), lambda i, j: (0, 0)),
                  pl.BlockSpec((D, tn), lambda i, j: (0, j)),
                  pl.BlockSpec((tm, tn), lambda i, j: (i, j))],
        out_specs=pl.BlockSpec((tm, tn), lambda i, j: (i, j)),
        out_shape=jax.ShapeDtypeStruct((T, D), F32),
        scratch_shapes=[pltpu.VMEM((tm, D), BF16)],
        compiler_params=_cparams("parallel", "arbitrary"),
        name="gdn_out",
    )(o_f, o_b, qkvz, o_gain.reshape(1, B_HEAD_DIM), w, x)


def _attention_layer(x, gain, w_in, q_gain, k_gain, sink, w_out, batch, seq):
    qkv = _norm_matmul(x, gain, w_in.astype(BF16), tm=512, tn=512)
    a = _attention(qkv, q_gain, k_gain, sink, batch, seq)
    return _matmul_residual(a, w_out.astype(BF16), x, tm=512, tn=512)


def _gdn_layer(x, gain, w_in, conv_w, a_log, dt_bias, o_gain, w_out, batch, seq):
    D = x.shape[1]
    w_main = w_in[:, :4 * D].astype(BF16)
    pad = ROUTER_LANES - 4 * B_HEADS
    w_ab = jnp.concatenate([w_in[:, 4 * D:], jnp.zeros((D, pad), F32)], axis=1)
    qkvz, ab = _norm_matmul2(x, gain, w_main, w_ab, tm=512, tn=512)
    qkv, gb = _gdn_conv(qkvz, ab, conv_w, a_log, dt_bias, seq, tm=256)
    o_f, o_b = _gdn_scan(qkv, gb, batch, seq, GDN_CHUNK)
    return _gdn_out(o_f, o_b, qkvz, o_gain, w_out.astype(BF16), x, tm=512, tn=512)


def kernel(x, norm_mix, norm_ffn, attn_w_in, attn_q_gain, attn_k_gain, attn_sink, attn_w_out, gdn_w_in, gdn_conv, gdn_a_log, gdn_dt_bias, gdn_o_gain, gdn_w_out, moe_w_group, moe_b_group, moe_w_expert, moe_b_expert, moe_w_gate, moe_w_up, moe_w_down):
    batch, seq, d_model = x.shape
    depth = norm_mix.shape[0]
    xt = x.reshape(batch * seq, d_model)
    for i in range(depth):
        j = i // 2
        if i % 2 == 0:
            xt = _attention_layer(xt, norm_mix[i], attn_w_in[j], attn_q_gain[j], attn_k_gain[j],
                                  attn_sink[j], attn_w_out[j], batch, seq)
        else:
            xt = _gdn_layer(xt, norm_mix[i], gdn_w_in[j], gdn_conv[j], gdn_a_log[j], gdn_dt_bias[j],
                            gdn_o_gain[j], gdn_w_out[j], batch, seq)
        xt = _moe(xt, norm_ffn[i], moe_w_group[i], moe_b_group[i], moe_w_expert[i], moe_b_expert[i],
                  moe_w_gate[i], moe_w_up[i], moe_w_down[i], tm=1024)
    return xt.reshape(batch, seq, d_model)
```

```python
import functools
import math

import jax
import jax.numpy as jnp
import numpy as np
from jax import lax
from jax.experimental import pallas as pl
from jax.experimental.pallas import tpu as pltpu

RMS_EPS = 1e-6
NEG_INF = -1e30
F32 = jnp.float32
BF16 = jnp.bfloat16

A_HEADS = 16
A_KV_HEADS = 4
A_HEAD_DIM = 64
A_REP = A_HEADS // A_KV_HEADS
A_BLOCK = 128
B_HEADS = 8
B_HEAD_DIM = 128
B_CONV = 4
GDN_CHUNK = 64
N_GROUPS = 4
EXPERTS_PER_GROUP = 8
N_EXPERTS = 32
D_EXPERT = 256
ROUTER_LANES = 128

V7X_VMEM_LIMIT_BYTES = 56 * 1024 * 1024


def _cparams(*sem):
    return pltpu.CompilerParams(dimension_semantics=sem, vmem_limit_bytes=V7X_VMEM_LIMIT_BYTES)


def _bdot(a, b):
    return jnp.dot(a.astype(BF16), b.astype(BF16), preferred_element_type=F32)


def _bdot_nt(a, b):
    return lax.dot_general(a.astype(BF16), b.astype(BF16), (((1,), (1,)), ((), ())),
                           preferred_element_type=F32)


def _bdot_tn(a, b):
    return lax.dot_general(a.astype(BF16), b.astype(BF16), (((0,), (0,)), ((), ())),
                           preferred_element_type=F32)


def _split_bf16(a):
    hi = a.astype(BF16)
    lo = (a - hi.astype(F32)).astype(BF16)
    return hi, lo


def _rms(x, gain):
    return x * lax.rsqrt(jnp.mean(x * x, axis=-1, keepdims=True) + RMS_EPS) * gain


def _norm_matmul_kernel(x_ref, g_ref, w_ref, o_ref, xn_ref):
    @pl.when(pl.program_id(1) == 0)
    def _():
        xn_ref[...] = _rms(x_ref[...], g_ref[...]).astype(BF16)

    o_ref[...] = jnp.dot(xn_ref[...], w_ref[...], preferred_element_type=F32)


def _norm_matmul(x, gain, w, tm, tn):
    T, D = x.shape
    N = w.shape[1]
    return pl.pallas_call(
        _norm_matmul_kernel,
        grid=(T // tm, N // tn),
        in_specs=[pl.BlockSpec((tm, D), lambda i, j: (i, 0)),
                  pl.BlockSpec((1, D), lambda i, j: (0, 0)),
                  pl.BlockSpec((D, tn), lambda i, j: (0, j))],
        out_specs=pl.BlockSpec((tm, tn), lambda i, j: (i, j)),
        out_shape=jax.ShapeDtypeStruct((T, N), F32),
        scratch_shapes=[pltpu.VMEM((tm, D), BF16)],
        compiler_params=_cparams("parallel", "arbitrary"),
        name="norm_matmul",
    )(x, gain.reshape(1, D), w)


def _norm_matmul2_kernel(x_ref, g_ref, w_ref, w2_ref, o_ref, o2_ref, xn_ref):
    @pl.when(pl.program_id(1) == 0)
    def _():
        xn = _rms(x_ref[...], g_ref[...])
        xn_ref[...] = xn.astype(BF16)
        x_hi, x_lo = _split_bf16(xn)
        w_hi, w_lo = _split_bf16(w2_ref[...])
        o2_ref[...] = (jnp.dot(x_hi, w_hi, preferred_element_type=F32)
                       + jnp.dot(x_lo, w_hi, preferred_element_type=F32)
                       + jnp.dot(x_hi, w_lo, preferred_element_type=F32))

    o_ref[...] = jnp.dot(xn_ref[...], w_ref[...], preferred_element_type=F32)


def _norm_matmul2(x, gain, w, w2, tm, tn):
    T, D = x.shape
    N = w.shape[1]
    N2 = w2.shape[1]
    return pl.pallas_call(
        _norm_matmul2_kernel,
        grid=(T // tm, N // tn),
        in_specs=[pl.BlockSpec((tm, D), lambda i, j: (i, 0)),
                  pl.BlockSpec((1, D), lambda i, j: (0, 0)),
                  pl.BlockSpec((D, tn), lambda i, j: (0, j)),
                  pl.BlockSpec((D, N2), lambda i, j: (0, 0))],
        out_specs=[pl.BlockSpec((tm, tn), lambda i, j: (i, j)),
                   pl.BlockSpec((tm, N2), lambda i, j: (i, 0))],
        out_shape=[jax.ShapeDtypeStruct((T, N), F32), jax.ShapeDtypeStruct((T, N2), F32)],
        scratch_shapes=[pltpu.VMEM((tm, D), BF16)],
        compiler_params=_cparams("parallel", "arbitrary"),
        name="norm_matmul2",
    )(x, gain.reshape(1, D), w, w2)


def _matmul_residual_kernel(a_ref, w_ref, x_ref, o_ref):
    o_ref[...] = x_ref[...] + jnp.dot(a_ref[...], w_ref[...], preferred_element_type=F32)


def _matmul_residual(a, w, x, tm, tn):
    T, K = a.shape
    N = w.shape[1]
    return pl.pallas_call(
        _matmul_residual_kernel,
        grid=(T // tm, N // tn),
        in_specs=[pl.BlockSpec((tm, K), lambda i, j: (i, 0)),
                  pl.BlockSpec((K, tn), lambda i, j: (0, j)),
                  pl.BlockSpec((tm, tn), lambda i, j: (i, j))],
        out_specs=pl.BlockSpec((tm, tn), lambda i, j: (i, j)),
        out_shape=jax.ShapeDtypeStruct((T, N), F32),
        compiler_params=_cparams("parallel", "arbitrary"),
        name="matmul_residual",
    )(a, w, x)


def _attn_kernel(main_ref, prev_ref, next_ref, qg_ref, kg_ref, sink_ref, bias_ref, o_ref):
    n = pl.program_id(1)
    nb = pl.num_programs(1)
    dh, blk = A_HEAD_DIM, A_BLOCK
    kv_cols = A_KV_HEADS * dh
    main = main_ref[...]
    kv = jnp.concatenate([prev_ref[...], main[:, A_HEADS * dh:], next_ref[...]], axis=0)
    col = lax.broadcasted_iota(jnp.int32, (1, 3 * blk), 1)
    outside = ((col < blk) & (n == 0)) | ((col >= 2 * blk) & (n == nb - 1))
    edge = jnp.where(outside, NEG_INF, 0.0)
    qg = qg_ref[...] * (dh ** -0.5)
    groups = range(A_KV_HEADS)
    kn = [_rms(kv[:, g * dh:(g + 1) * dh], kg_ref[...]).astype(BF16) for g in groups]
    q4 = [jnp.concatenate([_rms(main[:, (A_REP * g + r) * dh:(A_REP * g + r + 1) * dh], qg).astype(BF16)
                           for r in range(A_REP)], axis=0) for g in groups]
    s = [_bdot_nt(q4[g], kn[g]) + bias_ref[g] + edge for g in groups]
    m = [jnp.maximum(jnp.max(s[g], axis=-1, keepdims=True), sink_ref[g]) for g in groups]
    p = [jnp.exp(s[g] - m[g]) for g in groups]
    denom = [jnp.sum(p[g], axis=-1, keepdims=True) + jnp.exp(sink_ref[g] - m[g]) for g in groups]
    o = [jnp.dot(p[g].astype(BF16), kv[:, kv_cols + g * dh: kv_cols + (g + 1) * dh].astype(BF16),
                 preferred_element_type=F32) / denom[g] for g in groups]
    for g in groups:
        for r in range(A_REP):
            h = A_REP * g + r
            o_ref[:, h * dh:(h + 1) * dh] = o[g][r * blk:(r + 1) * blk].astype(o_ref.dtype)


def _attn_tables(sink):
    blk = A_BLOCK
    slopes = np.array([2.0 ** (-8.0 * (h + 1) / A_HEADS) for h in range(A_HEADS)], np.float32)
    qi = np.arange(blk)
    kj = np.arange(3 * blk)
    dist = np.abs(blk + qi[:, None] - kj[None, :]).astype(np.float32)
    bias = np.where(dist[None] <= blk, -slopes[:, None, None] * dist[None], np.float32(NEG_INF))
    bias = bias.astype(np.float32).reshape(A_KV_HEADS, A_REP * blk, 3 * blk)
    sink_rows = jnp.repeat(sink.astype(F32).reshape(A_KV_HEADS, A_REP), blk, axis=1)[..., None]
    return jnp.asarray(bias), sink_rows


def _attention(qkv, q_gain, k_gain, sink, batch, seq):
    T, W = qkv.shape
    blk, dh = A_BLOCK, A_HEAD_DIM
    nb = seq // blk
    kv_w = 2 * A_KV_HEADS * dh
    kv_blk = (A_HEADS * dh) // kv_w
    bias, sink_rows = _attn_tables(sink)
    return pl.pallas_call(
        _attn_kernel,
        grid=(batch, nb),
        in_specs=[
            pl.BlockSpec((blk, W), lambda b, n: (b * nb + n, 0)),
            pl.BlockSpec((blk, kv_w), lambda b, n: (b * nb + jnp.maximum(n - 1, 0), kv_blk)),
            pl.BlockSpec((blk, kv_w), lambda b, n: (b * nb + jnp.minimum(n + 1, nb - 1), kv_blk)),
            pl.BlockSpec((1, dh), lambda b, n: (0, 0)),
            pl.BlockSpec((1, dh), lambda b, n: (0, 0)),
            pl.BlockSpec((A_KV_HEADS, A_REP * blk, 1), lambda b, n: (0, 0, 0)),
            pl.BlockSpec((A_KV_HEADS, A_REP * blk, 3 * blk), lambda b, n: (0, 0, 0)),
        ],
        out_specs=pl.BlockSpec((blk, A_HEADS * dh), lambda b, n: (b * nb + n, 0)),
        out_shape=jax.ShapeDtypeStruct((T, A_HEADS * dh), BF16),
        compiler_params=_cparams("parallel", "parallel"),
        name="window_attention",
    )(qkv, qkv, qkv, q_gain.reshape(1, dh), k_gain.reshape(1, dh), sink_rows, bias)


def _route(logits):
    lane = lax.broadcasted_iota(jnp.int32, logits.shape, 1).astype(F32)
    big = jnp.float32(1e9)
    is_g = (lane >= N_EXPERTS) & (lane < N_EXPERTS + N_GROUPS)
    lg = jnp.where(is_g, logits, NEG_INF)
    gmax = jnp.max(lg, axis=-1, keepdims=True)
    gidx = jnp.min(jnp.where(is_g & (lg == gmax), lane, big), axis=-1, keepdims=True) - N_EXPERTS
    g_prob = 1.0 / jnp.sum(jnp.where(is_g, jnp.exp(lg - gmax), 0.0), axis=-1, keepdims=True)
    lo = gidx * EXPERTS_PER_GROUP
    in_grp = (lane >= lo) & (lane < lo + EXPERTS_PER_GROUP)
    le = jnp.where(in_grp, logits, NEG_INF)
    emax = jnp.max(le, axis=-1, keepdims=True)
    ex = jnp.where(in_grp, jnp.exp(le - emax), 0.0)
    prob = ex / jnp.sum(ex, axis=-1, keepdims=True)
    cand = jnp.where(in_grp, prob, -1.0)
    p1 = jnp.max(cand, axis=-1, keepdims=True)
    i1 = jnp.min(jnp.where(cand == p1, lane, big), axis=-1, keepdims=True)
    cand2 = jnp.where(lane == i1, -1.0, cand)
    p2 = jnp.max(cand2, axis=-1, keepdims=True)
    i2 = jnp.min(jnp.where(cand2 == p2, lane, big), axis=-1, keepdims=True)
    scale = g_prob / (p1 + p2)
    return jnp.where(lane == i1, p1 * scale, 0.0) + jnp.where(lane == i2, p2 * scale, 0.0)


def _moe_kernel(x_ref, g_ref, wr_ref, br_ref, wgu_ref, wd_ref, o_ref, h_ref, comb_ref, acc_ref):
    e = pl.program_id(1)

    @pl.when(e == 0)
    def _():
        hn = _rms(x_ref[...], g_ref[...])
        h_hi, h_lo = _split_bf16(hn)
        w_hi, w_lo = _split_bf16(wr_ref[...])
        h_ref[...] = h_hi
        logits = (jnp.dot(h_hi, w_hi, preferred_element_type=F32)
                  + jnp.dot(h_lo, w_hi, preferred_element_type=F32)
                  + jnp.dot(h_hi, w_lo, preferred_element_type=F32)) + br_ref[...]
        comb_ref[...] = _route(logits)
        acc_ref[...] = jnp.zeros_like(acc_ref)

    gu = jnp.dot(h_ref[...], wgu_ref[0], preferred_element_type=F32)
    gate, up = gu[:, :D_EXPERT], gu[:, D_EXPERT:]
    lane = lax.broadcasted_iota(jnp.int32, comb_ref.shape, 1)
    c = jnp.sum(jnp.where(lane == e, comb_ref[...], 0.0), axis=-1, keepdims=True)
    hid = (gate * jax.nn.sigmoid(gate)) * up * c
    acc_ref[...] += jnp.dot(hid.astype(BF16), wd_ref[0], preferred_element_type=F32)

    @pl.when(e == pl.num_programs(1) - 1)
    def _():
        o_ref[...] = x_ref[...] + acc_ref[...]


def _moe(x, gain, w_group, b_group, w_expert, b_expert, w_gate, w_up, w_down, tm):
    T, D = x.shape
    pad = ROUTER_LANES - N_EXPERTS - N_GROUPS
    wr = jnp.concatenate([w_expert, w_group, jnp.zeros((D, pad), F32)], axis=1)
    br = jnp.concatenate([b_expert, b_group, jnp.zeros((pad,), F32)]).reshape(1, ROUTER_LANES)
    wgu = jnp.concatenate([w_gate, w_up], axis=-1).astype(BF16)
    wd = w_down.astype(BF16)
    return pl.pallas_call(
        _moe_kernel,
        grid=(T // tm, N_EXPERTS),
        in_specs=[pl.BlockSpec((tm, D), lambda i, e: (i, 0)),
                  pl.BlockSpec((1, D), lambda i, e: (0, 0)),
                  pl.BlockSpec((D, ROUTER_LANES), lambda i, e: (0, 0)),
                  pl.BlockSpec((1, ROUTER_LANES), lambda i, e: (0, 0)),
                  pl.BlockSpec((1, D, 2 * D_EXPERT), lambda i, e: (e, 0, 0)),
                  pl.BlockSpec((1, D_EXPERT, D), lambda i, e: (e, 0, 0))],
        out_specs=pl.BlockSpec((tm, D), lambda i, e: (i, 0)),
        out_shape=jax.ShapeDtypeStruct((T, D), F32),
        scratch_shapes=[pltpu.VMEM((tm, D), BF16),
                        pltpu.VMEM((tm, ROUTER_LANES), F32),
                        pltpu.VMEM((tm, D), F32)],
        compiler_params=_cparams("parallel", "arbitrary"),
        name="moe",
    )(x, gain.reshape(1, D), wr, br, wgu, wd)


GDN_HALO = 8


def _gdn_conv_kernel(u_ref, up_ref, un_ref, w_ref, ab_ref, aexp_ref, dtb_ref, o_ref, gb_ref, *, tiles_per_seq):
    i = pl.program_id(0)
    j = pl.program_id(1)
    tm = u_ref.shape[0]
    first = (i % tiles_per_seq) == 0
    last = (i % tiles_per_seq) == tiles_per_seq - 1
    u = u_ref[...]
    prev = jnp.where(first, 0.0, up_ref[...])
    nxt = jnp.where(last, 0.0, un_ref[...])
    ext = jnp.concatenate([prev, u, nxt], axis=0)
    w = w_ref[...]
    h0 = GDN_HALO - B_CONV // 2
    y = w[0:1] * ext[h0:h0 + tm]
    for t in range(1, B_CONV):
        y = y + w[t:t + 1] * ext[h0 + t:h0 + t + tm]
    y = y * jax.nn.sigmoid(y)
    qscale = jnp.where(j == 0, B_HEAD_DIM ** -0.5, 1.0)
    for h in range(B_HEADS):
        slab = y[:, h * B_HEAD_DIM:(h + 1) * B_HEAD_DIM]
        inv = lax.rsqrt(jnp.sum(slab * slab, axis=-1, keepdims=True) + RMS_EPS) * qscale
        o_ref[0, :, h * B_HEAD_DIM:(h + 1) * B_HEAD_DIM] = slab * jnp.where(j < 2, inv, 1.0)

    @pl.when(j == 0)
    def _():
        ab = ab_ref[...]
        lane = lax.broadcasted_iota(jnp.int32, ab.shape, 1)
        z = ab + dtb_ref[...]
        softplus = jnp.maximum(z, 0.0) + jnp.log1p(jnp.exp(-jnp.abs(z)))
        gb_ref[...] = jnp.where(lane < 2 * B_HEADS, -aexp_ref[...] * softplus, jax.nn.sigmoid(ab))


def _gdn_conv(qkvz, ab, conv_w, a_log, dt_bias, seq, tm):
    T = qkvz.shape[0]
    D = B_HEADS * B_HEAD_DIM
    hb = tm // GDN_HALO
    n_halo = T // GDN_HALO
    pad = ROUTER_LANES - 2 * B_HEADS
    aexp = jnp.concatenate([jnp.exp(a_log.astype(F32)).reshape(-1), jnp.zeros((pad,), F32)]).reshape(1, -1)
    dtb = jnp.concatenate([dt_bias.astype(F32).reshape(-1), jnp.zeros((pad,), F32)]).reshape(1, -1)
    return pl.pallas_call(
        functools.partial(_gdn_conv_kernel, tiles_per_seq=seq // tm),
        grid=(T // tm, 3),
        in_specs=[pl.BlockSpec((tm, D), lambda i, j: (i, j)),
                  pl.BlockSpec((GDN_HALO, D), lambda i, j: (jnp.maximum(i * hb - 1, 0), j)),
                  pl.BlockSpec((GDN_HALO, D), lambda i, j: (jnp.minimum((i + 1) * hb, n_halo - 1), j)),
                  pl.BlockSpec((B_CONV, D), lambda i, j: (0, j)),
                  pl.BlockSpec((tm, ROUTER_LANES), lambda i, j: (i, 0)),
                  pl.BlockSpec((1, ROUTER_LANES), lambda i, j: (0, 0)),
                  pl.BlockSpec((1, ROUTER_LANES), lambda i, j: (0, 0))],
        out_specs=[pl.BlockSpec((1, tm, D), lambda i, j: (j, i, 0)),
                   pl.BlockSpec((tm, ROUTER_LANES), lambda i, j: (i, 0))],
        out_shape=[jax.ShapeDtypeStruct((3, T, D), F32), jax.ShapeDtypeStruct((T, ROUTER_LANES), F32)],
        compiler_params=_cparams("parallel", "arbitrary"),
        name="gdn_conv",
    )(qkvz, qkvz, qkvz, conv_w, ab, aexp, dtb)


def _gdn_gate_terms(gb, incl):
    C = gb.shape[0]
    lane = lax.broadcasted_iota(jnp.int32, gb.shape, 1)
    g_hi, g_lo = _split_bf16(jnp.where(lane < 2 * B_HEADS, gb, 0.0))
    tri = incl.astype(BF16)
    gc = jnp.dot(tri, g_hi, preferred_element_type=F32) + jnp.dot(tri, g_lo, preferred_element_type=F32)
    return gc, gc.T


def _gdn_scan_kernel(qf_ref, kf_ref, vf_ref, gf_ref, qb_ref, kb_ref, vb_ref, gbw_ref, of_ref, ob_ref, sf_ref, sb_ref):
    @pl.when(pl.program_id(1) == 0)
    def _():
        sf_ref[...] = jnp.zeros_like(sf_ref)
        sb_ref[...] = jnp.zeros_like(sb_ref)

    C = gf_ref.shape[0]
    dk = B_HEAD_DIM
    row = lax.broadcasted_iota(jnp.int32, (C, C), 0)
    colm = lax.broadcasted_iota(jnp.int32, (C, C), 1)
    eye = (row == colm).astype(F32)
    incl = (row >= colm, row <= colm)
    strict = (row > colm, row < colm)
    qkv_refs = ((qf_ref, kf_ref, vf_ref), (qb_ref, kb_ref, vb_ref))
    gbs = (gf_ref[...], gbw_ref[...])
    s_refs = (sf_ref, sb_ref)
    o_refs = (of_ref, ob_ref)
    gates = [_gdn_gate_terms(gbs[d], incl[d]) for d in range(2)]
    glast = [gates[0][0][C - 1:C], gates[1][0][0:1]]
    units = [(d, h) for d in range(2) for h in range(B_HEADS)]

    def lane_of(d, h):
        return d * B_HEADS + h

    def cols(h):
        return slice(h * dk, (h + 1) * dk)

    v_b, kb_l, qd_bf, kd_bf, a_l, dec_l, egc_l = [], [], [], [], [], [], []
    for d, h in units:
        r = lane_of(d, h)
        gc, gct = gates[d]
        gcol = gc[:, r:r + 1]
        beta = gbs[d][:, 2 * B_HEADS + r:2 * B_HEADS + r + 1]
        q_ref, k_ref, v_ref = qkv_refs[d]
        qh, kh, vh = q_ref[0, :, cols(h)], k_ref[0, :, cols(h)], v_ref[0, :, cols(h)]
        egc = jnp.exp(gcol)
        kb = kh * beta
        khb = kh.astype(BF16)
        a_l.append(_bdot_nt(jnp.concatenate([kb, qh], axis=0), khb))
        dec_l.append(jnp.exp(jnp.where(incl[d], gcol - gct[r:r + 1, :], NEG_INF)))
        v_b.append(vh * beta)
        kb_l.append(kb)
        egc_l.append(egc)
        qd_bf.append((qh * egc).astype(BF16))
        kd_bf.append((kh * jnp.exp(glast[d][:, r:r + 1] - gcol)).astype(BF16))
    x_l = [-jnp.where(strict[d], a[:C] * dec, 0.0) for (d, h), a, dec in zip(units, a_l, dec_l)]
    intra_bf = [(a[C:] * dec).astype(BF16) for a, dec in zip(a_l, dec_l)]
    p_l = [eye + x for x in x_l]
    n = 1
    while 2 * n < C:
        x_l = [_bdot(x, x) for x in x_l]
        n *= 2
        p_l = [p + _bdot(p, x) for p, x in zip(p_l, x_l)]
    sol_l = [_bdot(p, jnp.concatenate([vb, kb * egc], axis=1))
             for p, vb, kb, egc in zip(p_l, v_b, kb_l, egc_l)]
    st_l = [s_refs[d][h] for d, h in units]
    wq_l = [_bdot(jnp.concatenate([sol[:, dk:].astype(BF16), qd], axis=0), st)
            for sol, qd, st in zip(sol_l, qd_bf, st_l)]
    vn_l = [sol[:, :dk] - wq[:C] for sol, wq in zip(sol_l, wq_l)]
    for (d, h), wq, intra, vn in zip(units, wq_l, intra_bf, vn_l):
        o_refs[d][:, cols(h)] = wq[C:] + _bdot(intra, vn)
    for (d, h), st, kd, vn in zip(units, st_l, kd_bf, vn_l):
        r = lane_of(d, h)
        s_refs[d][h] = st * jnp.exp(glast[d][:, r:r + 1]) + _bdot_tn(kd, vn)


def _gdn_scan(qkv, gb, batch, seq, chunk):
    _, T, D = qkv.shape
    nc = seq // chunk
    fwd = lambda b, c: b * nc + c
    bwd = lambda b, c: b * nc + (nc - 1 - c)
    part = lambda p, f: pl.BlockSpec((1, chunk, D), lambda b, c: (p, f(b, c), 0))
    gspec = lambda f: pl.BlockSpec((chunk, ROUTER_LANES), lambda b, c: (f(b, c), 0))
    ospec = lambda f: pl.BlockSpec((chunk, D), lambda b, c: (f(b, c), 0))
    return pl.pallas_call(
        _gdn_scan_kernel,
        grid=(batch, nc),
        in_specs=[part(0, fwd), part(1, fwd), part(2, fwd), gspec(fwd),
                  part(0, bwd), part(1, bwd), part(2, bwd), gspec(bwd)],
        out_specs=[ospec(fwd), ospec(bwd)],
        out_shape=[jax.ShapeDtypeStruct((T, D), F32), jax.ShapeDtypeStruct((T, D), F32)],
        scratch_shapes=[pltpu.VMEM((B_HEADS, B_HEAD_DIM, B_HEAD_DIM), F32),
                        pltpu.VMEM((B_HEADS, B_HEAD_DIM, B_HEAD_DIM), F32)],
        compiler_params=_cparams("parallel", "arbitrary"),
        name="gdn_scan",
    )(qkv, qkv, qkv, gb, qkv, qkv, qkv, gb)


def _gdn_out_kernel(of_ref, ob_ref, z_ref, og_ref, w_ref, x_ref, o_ref, a_ref):
    @pl.when(pl.program_id(1) == 0)
    def _():
        o = of_ref[...] + ob_ref[...]
        z = z_ref[...]
        for h in range(B_HEADS):
            sl = slice(h * B_HEAD_DIM, (h + 1) * B_HEAD_DIM)
            zh = z[:, sl]
            a_ref[:, sl] = (_rms(o[:, sl], og_ref[...]) * (zh * jax.nn.sigmoid(zh))).astype(BF16)

    o_ref[...] = x_ref[...] + jnp.dot(a_ref[...], w_ref[...], preferred_element_type=F32)


def _gdn_out(o_f, o_b, qkvz, o_gain, w, x, tm, tn):
    T, D = x.shape
    return pl.pallas_call(
        _gdn_out_kernel,
        grid=(T // tm, D // tn),
        in_specs=[pl.BlockSpec((tm, D), lambda i, j: (i, 0)),
                  pl.BlockSpec((tm, D), lambda i, j: (i, 0)),
                  pl.BlockSpec((tm, D), lambda i, j: (i, 3)),
                  pl.BlockSpec((1, B_HEAD_DIM), lambda i, j: (0, 0)),
                  pl.BlockSpec((D, tn), lambda i, j: (0, j)),
                  pl.BlockSpec((tm, tn), lambda i, j: (i, j))],
        out_specs=pl.BlockSpec((tm, tn), lambda i, j: (i, j)),
        out_shape=jax.ShapeDtypeStruct((T, D), F32),
        scratch_shapes=[pltpu.VMEM((tm, D), BF16)],
        compiler_params=_cparams("parallel", "arbitrary"),
        name="gdn_out",
    )(o_f, o_b, qkvz, o_gain.reshape(1, B_HEAD_DIM), w, x)


def _attention_layer(x, gain, w_in, q_gain, k_gain, sink, w_out, batch, seq):
    qkv = _norm_matmul(x, gain, w_in.astype(BF16), tm=1024, tn=768)
    a = _attention(qkv, q_gain, k_gain, sink, batch, seq)
    return _matmul_residual(a, w_out.astype(BF16), x, tm=1024, tn=1024)


def _gdn_layer(x, gain, w_in, conv_w, a_log, dt_bias, o_gain, w_out, batch, seq):
    D = x.shape[1]
    w_main = w_in[:, :4 * D].astype(BF16)
    pad = ROUTER_LANES - 4 * B_HEADS
    w_ab = jnp.concatenate([w_in[:, 4 * D:], jnp.zeros((D, pad), F32)], axis=1)
    qkvz, ab = _norm_matmul2(x, gain, w_main, w_ab, tm=1024, tn=1024)
    qkv, gb = _gdn_conv(qkvz, ab, conv_w, a_log, dt_bias, seq, tm=512)
    o_f, o_b = _gdn_scan(qkv, gb, batch, seq, GDN_CHUNK)
    return _gdn_out(o_f, o_b, qkvz, o_gain, w_out.astype(BF16), x, tm=512, tn=1024)


def kernel(x, norm_mix, norm_ffn, attn_w_in, attn_q_gain, attn_k_gain, attn_sink, attn_w_out, gdn_w_in, gdn_conv, gdn_a_log, gdn_dt_bias, gdn_o_gain, gdn_w_out, moe_w_group, moe_b_group, moe_w_expert, moe_b_expert, moe_w_gate, moe_w_up, moe_w_down):
    batch, seq, d_model = x.shape
    depth = norm_mix.shape[0]
    xt = x.reshape(batch * seq, d_model)
    for i in range(depth):
        j = i // 2
        if i % 2 == 0:
            xt = _attention_layer(xt, norm_mix[i], attn_w_in[j], attn_q_gain[j], attn_k_gain[j],
                                  attn_sink[j], attn_w_out[j], batch, seq)
        else:
            xt = _gdn_layer(xt, norm_mix[i], gdn_w_in[j], gdn_conv[j], gdn_a_log[j], gdn_dt_bias[j],
                            gdn_o_gain[j], gdn_w_out[j], batch, seq)
        xt = _moe(xt, norm_ffn[i], moe_w_group[i], moe_b_group[i], moe_w_expert[i], moe_b_expert[i],
                  moe_w_gate[i], moe_w_up[i], moe_w_down[i], tm=1024)
    return xt.reshape(batch, seq, d_model)
```

```python
import functools
import math

import jax
import jax.numpy as jnp
import numpy as np
from jax import lax
from jax.experimental import pallas as pl
from jax.experimental.pallas import tpu as pltpu

RMS_EPS = 1e-6
NEG_INF = -1e30
F32 = jnp.float32
BF16 = jnp.bfloat16

A_HEADS = 16
A_KV_HEADS = 4
A_HEAD_DIM = 64
A_REP = A_HEADS // A_KV_HEADS
A_BLOCK = 128
B_HEADS = 8
B_HEAD_DIM = 128
B_CONV = 4
GDN_CHUNK = 64
N_GROUPS = 4
EXPERTS_PER_GROUP = 8
N_EXPERTS = 32
D_EXPERT = 256
ROUTER_LANES = 128

V7X_VMEM_LIMIT_BYTES = 56 * 1024 * 1024


def _cparams(*sem):
    return pltpu.CompilerParams(dimension_semantics=sem, vmem_limit_bytes=V7X_VMEM_LIMIT_BYTES)


def _bdot(a, b):
    return jnp.dot(a.astype(BF16), b.astype(BF16), preferred_element_type=F32)


def _bdot_nt(a, b):
    return lax.dot_general(a.astype(BF16), b.astype(BF16), (((1,), (1,)), ((), ())),
                           preferred_element_type=F32)


def _bdot_tn(a, b):
    return lax.dot_general(a.astype(BF16), b.astype(BF16), (((0,), (0,)), ((), ())),
                           preferred_element_type=F32)


def _split_bf16(a):
    hi = a.astype(BF16)
    lo = (a - hi.astype(F32)).astype(BF16)
    return hi, lo


def _rms(x, gain):
    return x * lax.rsqrt(jnp.mean(x * x, axis=-1, keepdims=True) + RMS_EPS) * gain


def _norm_matmul_kernel(x_ref, g_ref, w_ref, o_ref, xn_ref):
    @pl.when(pl.program_id(1) == 0)
    def _():
        xn_ref[...] = _rms(x_ref[...], g_ref[...]).astype(BF16)

    o_ref[...] = jnp.dot(xn_ref[...], w_ref[...], preferred_element_type=F32)


def _norm_matmul(x, gain, w, tm, tn):
    T, D = x.shape
    N = w.shape[1]
    return pl.pallas_call(
        _norm_matmul_kernel,
        grid=(T // tm, N // tn),
        in_specs=[pl.BlockSpec((tm, D), lambda i, j: (i, 0)),
                  pl.BlockSpec((1, D), lambda i, j: (0, 0)),
                  pl.BlockSpec((D, tn), lambda i, j: (0, j))],
        out_specs=pl.BlockSpec((tm, tn), lambda i, j: (i, j)),
        out_shape=jax.ShapeDtypeStruct((T, N), F32),
        scratch_shapes=[pltpu.VMEM((tm, D), BF16)],
        compiler_params=_cparams("parallel", "arbitrary"),
        name="norm_matmul",
    )(x, gain.reshape(1, D), w)


def _norm_matmul2_kernel(x_ref, g_ref, w_ref, w2_ref, o_ref, o2_ref, xn_ref):
    @pl.when(pl.program_id(1) == 0)
    def _():
        xn = _rms(x_ref[...], g_ref[...])
        xn_ref[...] = xn.astype(BF16)
        x_hi, x_lo = _split_bf16(xn)
        w_hi, w_lo = _split_bf16(w2_ref[...])
        o2_ref[...] = (jnp.dot(x_hi, w_hi, preferred_element_type=F32)
                       + jnp.dot(x_lo, w_hi, preferred_element_type=F32)
                       + jnp.dot(x_hi, w_lo, preferred_element_type=F32))

    o_ref[...] = jnp.dot(xn_ref[...], w_ref[...], preferred_element_type=F32)


def _norm_matmul2(x, gain, w, w2, tm, tn):
    T, D = x.shape
    N = w.shape[1]
    N2 = w2.shape[1]
    return pl.pallas_call(
        _norm_matmul2_kernel,
        grid=(T // tm, N // tn),
        in_specs=[pl.BlockSpec((tm, D), lambda i, j: (i, 0)),
                  pl.BlockSpec((1, D), lambda i, j: (0, 0)),
                  pl.BlockSpec((D, tn), lambda i, j: (0, j)),
                  pl.BlockSpec((D, N2), lambda i, j: (0, 0))],
        out_specs=[pl.BlockSpec((tm, tn), lambda i, j: (i, j)),
                   pl.BlockSpec((tm, N2), lambda i, j: (i, 0))],
        out_shape=[jax.ShapeDtypeStruct((T, N), F32), jax.ShapeDtypeStruct((T, N2), F32)],
        scratch_shapes=[pltpu.VMEM((tm, D), BF16)],
        compiler_params=_cparams("parallel", "arbitrary"),
        name="norm_matmul2",
    )(x, gain.reshape(1, D), w, w2)


def _matmul_residual_kernel(a_ref, w_ref, x_ref, o_ref):
    o_ref[...] = x_ref[...] + jnp.dot(a_ref[...], w_ref[...], preferred_element_type=F32)


def _matmul_residual(a, w, x, tm, tn):
    T, K = a.shape
    N = w.shape[1]
    return pl.pallas_call(
        _matmul_residual_kernel,
        grid=(T // tm, N // tn),
        in_specs=[pl.BlockSpec((tm, K), lambda i, j: (i, 0)),
                  pl.BlockSpec((K, tn), lambda i, j: (0, j)),
                  pl.BlockSpec((tm, tn), lambda i, j: (i, j))],
        out_specs=pl.BlockSpec((tm, tn), lambda i, j: (i, j)),
        out_shape=jax.ShapeDtypeStruct((T, N), F32),
        compiler_params=_cparams("parallel", "arbitrary"),
        name="matmul_residual",
    )(a, w, x)


def _attn_kernel(main_ref, prev_ref, next_ref, qg_ref, kg_ref, sink_ref, bias_ref, o_ref):
    n = pl.program_id(1)
    nb = pl.num_programs(1)
    dh, blk = A_HEAD_DIM, A_BLOCK
    kv_cols = A_KV_HEADS * dh
    main = main_ref[...]
    kv = jnp.concatenate([prev_ref[...], main[:, A_HEADS * dh:], next_ref[...]], axis=0)
    col = lax.broadcasted_iota(jnp.int32, (1, 3 * blk), 1)
    outside = ((col < blk) & (n == 0)) | ((col >= 2 * blk) & (n == nb - 1))
    edge = jnp.where(outside, NEG_INF, 0.0)
    qg = qg_ref[...] * (dh ** -0.5)
    groups = range(A_KV_HEADS)
    kn = [_rms(kv[:, g * dh:(g + 1) * dh], kg_ref[...]).astype(BF16) for g in groups]
    q4 = [jnp.concatenate([_rms(main[:, (A_REP * g + r) * dh:(A_REP * g + r + 1) * dh], qg).astype(BF16)
                           for r in range(A_REP)], axis=0) for g in groups]
    s = [_bdot_nt(q4[g], kn[g]) + bias_ref[g] + edge for g in groups]
    m = [jnp.maximum(jnp.max(s[g], axis=-1, keepdims=True), sink_ref[g]) for g in groups]
    p = [jnp.exp(s[g] - m[g]) for g in groups]
    denom = [jnp.sum(p[g], axis=-1, keepdims=True) + jnp.exp(sink_ref[g] - m[g]) for g in groups]
    o = [jnp.dot(p[g].astype(BF16), kv[:, kv_cols + g * dh: kv_cols + (g + 1) * dh].astype(BF16),
                 preferred_element_type=F32) / denom[g] for g in groups]
    for g in groups:
        for r in range(A_REP):
            h = A_REP * g + r
            o_ref[:, h * dh:(h + 1) * dh] = o[g][r * blk:(r + 1) * blk].astype(o_ref.dtype)


def _attn_tables(sink):
    blk = A_BLOCK
    slopes = np.array([2.0 ** (-8.0 * (h + 1) / A_HEADS) for h in range(A_HEADS)], np.float32)
    qi = np.arange(blk)
    kj = np.arange(3 * blk)
    dist = np.abs(blk + qi[:, None] - kj[None, :]).astype(np.float32)
    bias = np.where(dist[None] <= blk, -slopes[:, None, None] * dist[None], np.float32(NEG_INF))
    bias = bias.astype(np.float32).reshape(A_KV_HEADS, A_REP * blk, 3 * blk)
    sink_rows = jnp.repeat(sink.astype(F32).reshape(A_KV_HEADS, A_REP), blk, axis=1)[..., None]
    return jnp.asarray(bias), sink_rows


def _attention(qkv, q_gain, k_gain, sink, batch, seq):
    T, W = qkv.shape
    blk, dh = A_BLOCK, A_HEAD_DIM
    nb = seq // blk
    kv_w = 2 * A_KV_HEADS * dh
    kv_blk = (A_HEADS * dh) // kv_w
    bias, sink_rows = _attn_tables(sink)
    return pl.pallas_call(
        _attn_kernel,
        grid=(batch, nb),
        in_specs=[
            pl.BlockSpec((blk, W), lambda b, n: (b * nb + n, 0)),
            pl.BlockSpec((blk, kv_w), lambda b, n: (b * nb + jnp.maximum(n - 1, 0), kv_blk)),
            pl.BlockSpec((blk, kv_w), lambda b, n: (b * nb + jnp.minimum(n + 1, nb - 1), kv_blk)),
            pl.BlockSpec((1, dh), lambda b, n: (0, 0)),
            pl.BlockSpec((1, dh), lambda b, n: (0, 0)),
            pl.BlockSpec((A_KV_HEADS, A_REP * blk, 1), lambda b, n: (0, 0, 0)),
            pl.BlockSpec((A_KV_HEADS, A_REP * blk, 3 * blk), lambda b, n: (0, 0, 0)),
        ],
        out_specs=pl.BlockSpec((blk, A_HEADS * dh), lambda b, n: (b * nb + n, 0)),
        out_shape=jax.ShapeDtypeStruct((T, A_HEADS * dh), BF16),
        compiler_params=_cparams("parallel", "parallel"),
        name="window_attention",
    )(qkv, qkv, qkv, q_gain.reshape(1, dh), k_gain.reshape(1, dh), sink_rows, bias)


def _route(logits):
    lane = lax.broadcasted_iota(jnp.int32, logits.shape, 1).astype(F32)
    big = jnp.float32(1e9)
    is_g = (lane >= N_EXPERTS) & (lane < N_EXPERTS + N_GROUPS)
    lg = jnp.where(is_g, logits, NEG_INF)
    gmax = jnp.max(lg, axis=-1, keepdims=True)
    gidx = jnp.min(jnp.where(is_g & (lg == gmax), lane, big), axis=-1, keepdims=True) - N_EXPERTS
    g_prob = 1.0 / jnp.sum(jnp.where(is_g, jnp.exp(lg - gmax), 0.0), axis=-1, keepdims=True)
    lo = gidx * EXPERTS_PER_GROUP
    in_grp = (lane >= lo) & (lane < lo + EXPERTS_PER_GROUP)
    le = jnp.where(in_grp, logits, NEG_INF)
    emax = jnp.max(le, axis=-1, keepdims=True)
    ex = jnp.where(in_grp, jnp.exp(le - emax), 0.0)
    prob = ex / jnp.sum(ex, axis=-1, keepdims=True)
    cand = jnp.where(in_grp, prob, -1.0)
    p1 = jnp.max(cand, axis=-1, keepdims=True)
    i1 = jnp.min(jnp.where(cand == p1, lane, big), axis=-1, keepdims=True)
    cand2 = jnp.where(lane == i1, -1.0, cand)
    p2 = jnp.max(cand2, axis=-1, keepdims=True)
    i2 = jnp.min(jnp.where(cand2 == p2, lane, big), axis=-1, keepdims=True)
    scale = g_prob / (p1 + p2)
    return i1, i2, p1 * scale, p2 * scale


RT_E1, RT_E2, RT_G1, RT_G2, RT_R1, RT_R2 = range(6)


def _router_kernel(x_ref, g_ref, wr_ref, br_ref, h_ref, route_ref, counts_ref, carry_ref):
    @pl.when(pl.program_id(0) == 0)
    def _():
        carry_ref[...] = jnp.zeros_like(carry_ref)

    hn = _rms(x_ref[...], g_ref[...])
    h_ref[...] = hn
    h_hi, h_lo = _split_bf16(hn)
    w_hi, w_lo = _split_bf16(wr_ref[...])
    logits = (jnp.dot(h_hi, w_hi, preferred_element_type=F32)
              + jnp.dot(h_lo, w_hi, preferred_element_type=F32)
              + jnp.dot(h_hi, w_lo, preferred_element_type=F32)) + br_ref[...]
    i1, i2, g1, g2 = _route(logits)
    tm = logits.shape[0]
    lane = lax.broadcasted_iota(jnp.int32, logits.shape, 1).astype(F32)
    chosen = ((lane == i1) | (lane == i2)).astype(BF16)
    earlier = (lax.broadcasted_iota(jnp.int32, (tm, tm), 1)
               < lax.broadcasted_iota(jnp.int32, (tm, tm), 0)).astype(BF16)
    before = jnp.dot(earlier, chosen, preferred_element_type=F32) + carry_ref[...]
    r1 = jnp.sum(jnp.where(lane == i1, before, 0.0), axis=-1, keepdims=True)
    r2 = jnp.sum(jnp.where(lane == i2, before, 0.0), axis=-1, keepdims=True)
    carry_ref[...] += jnp.sum(chosen.astype(F32), axis=0, keepdims=True)
    rec = jnp.zeros_like(logits)
    for slot, val in ((RT_E1, i1), (RT_E2, i2), (RT_G1, g1), (RT_G2, g2), (RT_R1, r1), (RT_R2, r2)):
        rec = jnp.where(lane == slot, val, rec)
    route_ref[...] = rec
    counts_ref[...] = carry_ref[...]


def _router(x, gain, w_group, b_group, w_expert, b_expert, tm):
    T, D = x.shape
    pad = ROUTER_LANES - N_EXPERTS - N_GROUPS
    wr = jnp.concatenate([w_expert, w_group, jnp.zeros((D, pad), F32)], axis=1)
    br = jnp.concatenate([b_expert, b_group, jnp.zeros((pad,), F32)]).reshape(1, ROUTER_LANES)
    return pl.pallas_call(
        _router_kernel,
        grid=(T // tm,),
        in_specs=[pl.BlockSpec((tm, D), lambda i: (i, 0)),
                  pl.BlockSpec((1, D), lambda i: (0, 0)),
                  pl.BlockSpec((D, ROUTER_LANES), lambda i: (0, 0)),
                  pl.BlockSpec((1, ROUTER_LANES), lambda i: (0, 0))],
        out_specs=[pl.BlockSpec((tm, D), lambda i: (i, 0)),
                   pl.BlockSpec((tm, ROUTER_LANES), lambda i: (i, 0)),
                   pl.BlockSpec((1, ROUTER_LANES), lambda i: (0, 0))],
        out_shape=[jax.ShapeDtypeStruct((T, D), F32),
                   jax.ShapeDtypeStruct((T, ROUTER_LANES), F32),
                   jax.ShapeDtypeStruct((1, ROUTER_LANES), F32)],
        scratch_shapes=[pltpu.VMEM((1, ROUTER_LANES), F32)],
        compiler_params=_cparams("arbitrary"),
        name="moe_router",
    )(x, gain.reshape(1, D), wr, br)


MOE_TILE = 256
MOE_SLOTS = 2
DMA_UNROLL = 8


def _row_copy(src_hbm, src_row, dst, dst_row, sem):
    return pltpu.make_async_copy(src_hbm.at[pl.ds(src_row, 1)], dst.at[pl.ds(dst_row, 1)], sem)


def _dispatch_kernel(pos_ref, h_hbm, zeros_hbm, xs_hbm, sem):
    del zeros_hbm
    tm = pos_ref.shape[0] // MOE_SLOTS
    base = pl.program_id(0) * tm

    def start(r, c):
        for k in range(MOE_SLOTS):
            _row_copy(h_hbm, base + r, xs_hbm, pos_ref[MOE_SLOTS * r + k], sem).start()
        return c

    def wait(r, c):
        for k in range(MOE_SLOTS):
            _row_copy(h_hbm, 0, xs_hbm, 0, sem).wait()
        return c

    lax.fori_loop(0, tm, start, 0, unroll=DMA_UNROLL)
    lax.fori_loop(0, tm, wait, 0, unroll=DMA_UNROLL)


def _dispatch(h, pos, n_rows, tm):
    T, D = h.shape
    zeros = jnp.zeros((n_rows, D), F32)
    return pl.pallas_call(
        _dispatch_kernel,
        grid=(T // tm,),
        in_specs=[pl.BlockSpec((MOE_SLOTS * tm,), lambda i: (i,), memory_space=pltpu.SMEM),
                  pl.BlockSpec(memory_space=pl.ANY),
                  pl.BlockSpec(memory_space=pl.ANY)],
        out_specs=pl.BlockSpec(memory_space=pl.ANY),
        out_shape=jax.ShapeDtypeStruct((n_rows, D), F32),
        scratch_shapes=[pltpu.SemaphoreType.DMA(())],
        input_output_aliases={2: 0},
        compiler_params=_cparams("arbitrary"),
        name="moe_dispatch",
    )(pos, h, zeros)


def _expert_kernel(te_ref, nv_ref, xs_ref, wg_ref, wu_ref, wd_ref, ys_ref, wgu_s, wd_s):
    i = pl.program_id(0)
    nv = nv_ref[0]
    valid = i < nv
    ic = jnp.minimum(i, nv - 1)
    changed = (i == 0) | (te_ref[ic] != te_ref[jnp.maximum(ic - 1, 0)])

    @pl.when(valid & changed)
    def _():
        wgu_s[:, :D_EXPERT] = wg_ref[0].astype(BF16)
        wgu_s[:, D_EXPERT:] = wu_ref[0].astype(BF16)
        wd_s[...] = wd_ref[0].astype(BF16)

    @pl.when(valid)
    def _():
        gu = jnp.dot(xs_ref[...].astype(BF16), wgu_s[...], preferred_element_type=F32)
        gate, up = gu[:, :D_EXPERT], gu[:, D_EXPERT:]
        hid = (gate * jax.nn.sigmoid(gate)) * up
        ys_ref[...] = jnp.dot(hid.astype(BF16), wd_s[...], preferred_element_type=F32)

    @pl.when(jnp.logical_not(valid))
    def _():
        ys_ref[...] = jnp.zeros_like(ys_ref)


def _experts(xs, tile_expert, n_valid, w_gate, w_up, w_down):
    n_rows, D = xs.shape
    n_tiles = n_rows // MOE_TILE
    row = lambda i, te, nv: (jnp.minimum(i, nv[0] - 1), 0)
    wsel = lambda i, te, nv: (te[jnp.minimum(i, nv[0] - 1)], 0, 0)
    return pl.pallas_call(
        _expert_kernel,
        grid_spec=pltpu.PrefetchScalarGridSpec(
            num_scalar_prefetch=2,
            grid=(n_tiles,),
            in_specs=[pl.BlockSpec((MOE_TILE, D), row),
                      pl.BlockSpec((1, D, D_EXPERT), wsel),
                      pl.BlockSpec((1, D, D_EXPERT), wsel),
                      pl.BlockSpec((1, D_EXPERT, D), wsel)],
            out_specs=pl.BlockSpec((MOE_TILE, D), lambda i, te, nv: (i, 0)),
            scratch_shapes=[pltpu.VMEM((D, 2 * D_EXPERT), BF16), pltpu.VMEM((D_EXPERT, D), BF16)]),
        out_shape=jax.ShapeDtypeStruct((n_rows, D), F32),
        compiler_params=_cparams("arbitrary"),
        name="moe_experts",
    )(tile_expert, n_valid, xs, w_gate, w_up, w_down)


def _combine_kernel(pos_ref, x_ref, route_ref, ys_hbm, o_ref, buf, sem):
    tm = x_ref.shape[0]

    def start(r, c):
        for k in range(MOE_SLOTS):
            _row_copy(ys_hbm, pos_ref[MOE_SLOTS * r + k], buf.at[k], r, sem).start()
        return c

    def wait(r, c):
        for k in range(MOE_SLOTS):
            _row_copy(ys_hbm, 0, buf.at[k], 0, sem).wait()
        return c

    lax.fori_loop(0, tm, start, 0, unroll=DMA_UNROLL)
    lax.fori_loop(0, tm, wait, 0, unroll=DMA_UNROLL)
    rec = route_ref[...]
    o_ref[...] = (x_ref[...] + rec[:, RT_G1:RT_G1 + 1] * buf[0] + rec[:, RT_G2:RT_G2 + 1] * buf[1])


def _combine(x, route, pos, ys, tm):
    T, D = x.shape
    return pl.pallas_call(
        _combine_kernel,
        grid=(T // tm,),
        in_specs=[pl.BlockSpec((MOE_SLOTS * tm,), lambda i: (i,), memory_space=pltpu.SMEM),
                  pl.BlockSpec((tm, D), lambda i: (i, 0)),
                  pl.BlockSpec((tm, ROUTER_LANES), lambda i: (i, 0)),
                  pl.BlockSpec(memory_space=pl.ANY)],
        out_specs=pl.BlockSpec((tm, D), lambda i: (i, 0)),
        out_shape=jax.ShapeDtypeStruct((T, D), F32),
        scratch_shapes=[pltpu.VMEM((MOE_SLOTS, tm, D), F32), pltpu.SemaphoreType.DMA(())],
        compiler_params=_cparams("arbitrary"),
        name="moe_combine",
    )(pos, x, route, ys)


def _moe(x, gain, w_group, b_group, w_expert, b_expert, w_gate, w_up, w_down):
    T, D = x.shape
    n_tiles = (MOE_SLOTS * T + N_EXPERTS * (MOE_TILE - 1)) // MOE_TILE + 1
    h, route, counts = _router(x, gain, w_group, b_group, w_expert, b_expert, tm=512)
    counts = counts[0, :N_EXPERTS].astype(jnp.int32)
    padded = (counts + MOE_TILE - 1) // MOE_TILE * MOE_TILE
    ends = jnp.cumsum(padded)
    starts = ends - padded
    eid = route[:, RT_E1:RT_E2 + 1].astype(jnp.int32)
    rank = route[:, RT_R1:RT_R2 + 1].astype(jnp.int32)
    onehot = eid[..., None] == jnp.arange(N_EXPERTS, dtype=jnp.int32)
    pos = (rank + jnp.sum(jnp.where(onehot, starts, 0), axis=-1)).reshape(-1)
    tile_start = jnp.arange(n_tiles, dtype=jnp.int32) * MOE_TILE
    tile_expert = jnp.minimum(jnp.sum(tile_start[:, None] >= ends[None, :], axis=-1), N_EXPERTS - 1).astype(jnp.int32)
    n_valid = (ends[-1:] // MOE_TILE).astype(jnp.int32)
    xs = _dispatch(h, pos, n_tiles * MOE_TILE, tm=512)
    ys = _experts(xs, tile_expert, n_valid, w_gate, w_up, w_down)
    return _combine(x, route, pos, ys, tm=256)


GDN_HALO = 8


def _gdn_conv_kernel(u_ref, up_ref, un_ref, w_ref, ab_ref, aexp_ref, dtb_ref, o_ref, gb_ref, *, tiles_per_seq):
    i = pl.program_id(0)
    j = pl.program_id(1)
    tm = u_ref.shape[0]
    first = (i % tiles_per_seq) == 0
    last = (i % tiles_per_seq) == tiles_per_seq - 1
    u = u_ref[...]
    prev = jnp.where(first, 0.0, up_ref[...])
    nxt = jnp.where(last, 0.0, un_ref[...])
    ext = jnp.concatenate([prev, u, nxt], axis=0)
    w = w_ref[...]
    h0 = GDN_HALO - B_CONV // 2
    y = w[0:1] * ext[h0:h0 + tm]
    for t in range(1, B_CONV):
        y = y + w[t:t + 1] * ext[h0 + t:h0 + t + tm]
    y = y * jax.nn.sigmoid(y)
    qscale = jnp.where(j == 0, B_HEAD_DIM ** -0.5, 1.0)
    for h in range(B_HEADS):
        slab = y[:, h * B_HEAD_DIM:(h + 1) * B_HEAD_DIM]
        inv = lax.rsqrt(jnp.sum(slab * slab, axis=-1, keepdims=True) + RMS_EPS) * qscale
        o_ref[0, :, h * B_HEAD_DIM:(h + 1) * B_HEAD_DIM] = slab * jnp.where(j < 2, inv, 1.0)

    @pl.when(j == 0)
    def _():
        ab = ab_ref[...]
        lane = lax.broadcasted_iota(jnp.int32, ab.shape, 1)
        z = ab + dtb_ref[...]
        softplus = jnp.maximum(z, 0.0) + jnp.log1p(jnp.exp(-jnp.abs(z)))
        gb_ref[...] = jnp.where(lane < 2 * B_HEADS, -aexp_ref[...] * softplus, jax.nn.sigmoid(ab))


def _gdn_conv(qkvz, ab, conv_w, a_log, dt_bias, seq, tm):
    T = qkvz.shape[0]
    D = B_HEADS * B_HEAD_DIM
    hb = tm // GDN_HALO
    n_halo = T // GDN_HALO
    pad = ROUTER_LANES - 2 * B_HEADS
    aexp = jnp.concatenate([jnp.exp(a_log.astype(F32)).reshape(-1), jnp.zeros((pad,), F32)]).reshape(1, -1)
    dtb = jnp.concatenate([dt_bias.astype(F32).reshape(-1), jnp.zeros((pad,), F32)]).reshape(1, -1)
    return pl.pallas_call(
        functools.partial(_gdn_conv_kernel, tiles_per_seq=seq // tm),
        grid=(T // tm, 3),
        in_specs=[pl.BlockSpec((tm, D), lambda i, j: (i, j)),
                  pl.BlockSpec((GDN_HALO, D), lambda i, j: (jnp.maximum(i * hb - 1, 0), j)),
                  pl.BlockSpec((GDN_HALO, D), lambda i, j: (jnp.minimum((i + 1) * hb, n_halo - 1), j)),
                  pl.BlockSpec((B_CONV, D), lambda i, j: (0, j)),
                  pl.BlockSpec((tm, ROUTER_LANES), lambda i, j: (i, 0)),
                  pl.BlockSpec((1, ROUTER_LANES), lambda i, j: (0, 0)),
                  pl.BlockSpec((1, ROUTER_LANES), lambda i, j: (0, 0))],
        out_specs=[pl.BlockSpec((1, tm, D), lambda i, j: (j, i, 0)),
                   pl.BlockSpec((tm, ROUTER_LANES), lambda i, j: (i, 0))],
        out_shape=[jax.ShapeDtypeStruct((3, T, D), F32), jax.ShapeDtypeStruct((T, ROUTER_LANES), F32)],
        compiler_params=_cparams("parallel", "arbitrary"),
        name="gdn_conv",
    )(qkvz, qkvz, qkvz, conv_w, ab, aexp, dtb)


def _gdn_gate_terms(gb, incl):
    C = gb.shape[0]
    lane = lax.broadcasted_iota(jnp.int32, gb.shape, 1)
    g_hi, g_lo = _split_bf16(jnp.where(lane < 2 * B_HEADS, gb, 0.0))
    tri = incl.astype(BF16)
    gc = jnp.dot(tri, g_hi, preferred_element_type=F32) + jnp.dot(tri, g_lo, preferred_element_type=F32)
    return gc, gc.T


def _gdn_scan_kernel(qf_ref, kf_ref, vf_ref, gf_ref, qb_ref, kb_ref, vb_ref, gbw_ref, of_ref, ob_ref, sf_ref, sb_ref):
    @pl.when(pl.program_id(1) == 0)
    def _():
        sf_ref[...] = jnp.zeros_like(sf_ref)
        sb_ref[...] = jnp.zeros_like(sb_ref)

    C = gf_ref.shape[0]
    dk = B_HEAD_DIM
    row = lax.broadcasted_iota(jnp.int32, (C, C), 0)
    colm = lax.broadcasted_iota(jnp.int32, (C, C), 1)
    eye = (row == colm).astype(F32)
    incl = (row >= colm, row <= colm)
    strict = (row > colm, row < colm)
    qkv_refs = ((qf_ref, kf_ref, vf_ref), (qb_ref, kb_ref, vb_ref))
    gbs = (gf_ref[...], gbw_ref[...])
    s_refs = (sf_ref, sb_ref)
    o_refs = (of_ref, ob_ref)
    gates = [_gdn_gate_terms(gbs[d], incl[d]) for d in range(2)]
    glast = [gates[0][0][C - 1:C], gates[1][0][0:1]]
    units = [(d, h) for d in range(2) for h in range(B_HEADS)]

    def lane_of(d, h):
        return d * B_HEADS + h

    def cols(h):
        return slice(h * dk, (h + 1) * dk)

    v_b, kb_l, qd_bf, kd_bf, a_l, dec_l, egc_l = [], [], [], [], [], [], []
    for d, h in units:
        r = lane_of(d, h)
        gc, gct = gates[d]
        gcol = gc[:, r:r + 1]
        beta = gbs[d][:, 2 * B_HEADS + r:2 * B_HEADS + r + 1]
        q_ref, k_ref, v_ref = qkv_refs[d]
        qh, kh, vh = q_ref[0, :, cols(h)], k_ref[0, :, cols(h)], v_ref[0, :, cols(h)]
        egc = jnp.exp(gcol)
        kb = kh * beta
        khb = kh.astype(BF16)
        a_l.append(_bdot_nt(jnp.concatenate([kb, qh], axis=0), khb))
        dec_l.append(jnp.exp(jnp.where(incl[d], gcol - gct[r:r + 1, :], NEG_INF)))
        v_b.append(vh * beta)
        kb_l.append(kb)
        egc_l.append(egc)
        qd_bf.append((qh * egc).astype(BF16))
        kd_bf.append((kh * jnp.exp(glast[d][:, r:r + 1] - gcol)).astype(BF16))
    x_l = [-jnp.where(strict[d], a[:C] * dec, 0.0) for (d, h), a, dec in zip(units, a_l, dec_l)]
    intra_bf = [(a[C:] * dec).astype(BF16) for a, dec in zip(a_l, dec_l)]
    p_l = [eye + x for x in x_l]
    n = 1
    while 2 * n < C:
        x_l = [_bdot(x, x) for x in x_l]
        n *= 2
        p_l = [p + _bdot(p, x) for p, x in zip(p_l, x_l)]
    sol_l = [_bdot(p, jnp.concatenate([vb, kb * egc], axis=1))
             for p, vb, kb, egc in zip(p_l, v_b, kb_l, egc_l)]
    st_l = [s_refs[d][h] for d, h in units]
    wq_l = [_bdot(jnp.concatenate([sol[:, dk:].astype(BF16), qd], axis=0), st)
            for sol, qd, st in zip(sol_l, qd_bf, st_l)]
    vn_l = [sol[:, :dk] - wq[:C] for sol, wq in zip(sol_l, wq_l)]
    for (d, h), wq, intra, vn in zip(units, wq_l, intra_bf, vn_l):
        o_refs[d][:, cols(h)] = wq[C:] + _bdot(intra, vn)
    for (d, h), st, kd, vn in zip(units, st_l, kd_bf, vn_l):
        r = lane_of(d, h)
        s_refs[d][h] = st * jnp.exp(glast[d][:, r:r + 1]) + _bdot_tn(kd, vn)


def _gdn_scan(qkv, gb, batch, seq, chunk):
    _, T, D = qkv.shape
    nc = seq // chunk
    fwd = lambda b, c: b * nc + c
    bwd = lambda b, c: b * nc + (nc - 1 - c)
    part = lambda p, f: pl.BlockSpec((1, chunk, D), lambda b, c: (p, f(b, c), 0))
    gspec = lambda f: pl.BlockSpec((chunk, ROUTER_LANES), lambda b, c: (f(b, c), 0))
    ospec = lambda f: pl.BlockSpec((chunk, D), lambda b, c: (f(b, c), 0))
    return pl.pallas_call(
        _gdn_scan_kernel,
        grid=(batch, nc),
        in_specs=[part(0, fwd), part(1, fwd), part(2, fwd), gspec(fwd),
                  part(0, bwd), part(1, bwd), part(2, bwd), gspec(bwd)],
        out_specs=[ospec(fwd), ospec(bwd)],
        out_shape=[jax.ShapeDtypeStruct((T, D), F32), jax.ShapeDtypeStruct((T, D), F32)],
        scratch_shapes=[pltpu.VMEM((B_HEADS, B_HEAD_DIM, B_HEAD_DIM), F32),
                        pltpu.VMEM((B_HEADS, B_HEAD_DIM, B_HEAD_DIM), F32)],
        compiler_params=_cparams("parallel", "arbitrary"),
        name="gdn_scan",
    )(qkv, qkv, qkv, gb, qkv, qkv, qkv, gb)


def _gdn_out_kernel(of_ref, ob_ref, z_ref, og_ref, w_ref, x_ref, o_ref, a_ref):
    @pl.when(pl.program_id(1) == 0)
    def _():
        o = of_ref[...] + ob_ref[...]
        z = z_ref[...]
        for h in range(B_HEADS):
            sl = slice(h * B_HEAD_DIM, (h + 1) * B_HEAD_DIM)
            zh = z[:, sl]
            a_ref[:, sl] = (_rms(o[:, sl], og_ref[...]) * (zh * jax.nn.sigmoid(zh))).astype(BF16)

    o_ref[...] = x_ref[...] + jnp.dot(a_ref[...], w_ref[...], preferred_element_type=F32)


def _gdn_out(o_f, o_b, qkvz, o_gain, w, x, tm, tn):
    T, D = x.shape
    return pl.pallas_call(
        _gdn_out_kernel,
        grid=(T // tm, D // tn),
        in_specs=[pl.BlockSpec((tm, D), lambda i, j: (i, 0)),
                  pl.BlockSpec((tm, D), lambda i, j: (i, 0)),
                  pl.BlockSpec((tm, D), lambda i, j: (i, 3)),
                  pl.BlockSpec((1, B_HEAD_DIM), lambda i, j: (0, 0)),
                  pl.BlockSpec((D, tn), lambda i, j: (0, j)),
                  pl.BlockSpec((tm, tn), lambda i, j: (i, j))],
        out_specs=pl.BlockSpec((tm, tn), lambda i, j: (i, j)),
        out_shape=jax.ShapeDtypeStruct((T, D), F32),
        scratch_shapes=[pltpu.VMEM((tm, D), BF16)],
        compiler_params=_cparams("parallel", "arbitrary"),
        name="gdn_out",
    )(o_f, o_b, qkvz, o_gain.reshape(1, B_HEAD_DIM), w, x)


def _attention_layer(x, gain, w_in, q_gain, k_gain, sink, w_out, batch, seq):
    qkv = _norm_matmul(x, gain, w_in.astype(BF16), tm=1024, tn=768)
    a = _attention(qkv, q_gain, k_gain, sink, batch, seq)
    return _matmul_residual(a, w_out.astype(BF16), x, tm=1024, tn=1024)


def _gdn_layer(x, gain, w_in, conv_w, a_log, dt_bias, o_gain, w_out, batch, seq):
    D = x.shape[1]
    w_main = w_in[:, :4 * D].astype(BF16)
    pad = ROUTER_LANES - 4 * B_HEADS
    w_ab = jnp.concatenate([w_in[:, 4 * D:], jnp.zeros((D, pad), F32)], axis=1)
    qkvz, ab = _norm_matmul2(x, gain, w_main, w_ab, tm=1024, tn=1024)
    qkv, gb = _gdn_conv(qkvz, ab, conv_w, a_log, dt_bias, seq, tm=512)
    o_f, o_b = _gdn_scan(qkv, gb, batch, seq, GDN_CHUNK)
    return _gdn_out(o_f, o_b, qkvz, o_gain, w_out.astype(BF16), x, tm=512, tn=1024)


def kernel(x, norm_mix, norm_ffn, attn_w_in, attn_q_gain, attn_k_gain, attn_sink, attn_w_out, gdn_w_in, gdn_conv, gdn_a_log, gdn_dt_bias, gdn_o_gain, gdn_w_out, moe_w_group, moe_b_group, moe_w_expert, moe_b_expert, moe_w_gate, moe_w_up, moe_w_down):
    batch, seq, d_model = x.shape
    depth = norm_mix.shape[0]
    xt = x.reshape(batch * seq, d_model)
    for i in range(depth):
        j = i // 2
        if i % 2 == 0:
            xt = _attention_layer(xt, norm_mix[i], attn_w_in[j], attn_q_gain[j], attn_k_gain[j],
                                  attn_sink[j], attn_w_out[j], batch, seq)
        else:
            xt = _gdn_layer(xt, norm_mix[i], gdn_w_in[j], gdn_conv[j], gdn_a_log[j], gdn_dt_bias[j],
                            gdn_o_gain[j], gdn_w_out[j], batch, seq)
        xt = _moe(xt, norm_ffn[i], moe_w_group[i], moe_b_group[i], moe_w_expert[i], moe_b_expert[i],
                  moe_w_gate[i], moe_w_up[i], moe_w_down[i])
    return xt.reshape(batch, seq, d_model)
```

```python
import functools
import math

import jax
import jax.numpy as jnp
import numpy as np
from jax import lax
from jax.experimental import pallas as pl
from jax.experimental.pallas import tpu as pltpu

RMS_EPS = 1e-6
NEG_INF = -1e30
F32 = jnp.float32
BF16 = jnp.bfloat16

A_HEADS = 16
A_KV_HEADS = 4
A_HEAD_DIM = 64
A_REP = A_HEADS // A_KV_HEADS
A_BLOCK = 128
B_HEADS = 8
B_HEAD_DIM = 128
B_CONV = 4
GDN_CHUNK = 64
N_GROUPS = 4
EXPERTS_PER_GROUP = 8
N_EXPERTS = 32
D_EXPERT = 256
ROUTER_LANES = 128

V7X_VMEM_LIMIT_BYTES = 56 * 1024 * 1024


def _cparams(*sem):
    return pltpu.CompilerParams(dimension_semantics=sem, vmem_limit_bytes=V7X_VMEM_LIMIT_BYTES)


def _bdot(a, b):
    return jnp.dot(a.astype(BF16), b.astype(BF16), preferred_element_type=F32)


def _bdot_nt(a, b):
    return lax.dot_general(a.astype(BF16), b.astype(BF16), (((1,), (1,)), ((), ())),
                           preferred_element_type=F32)


def _bdot_tn(a, b):
    return lax.dot_general(a.astype(BF16), b.astype(BF16), (((0,), (0,)), ((), ())),
                           preferred_element_type=F32)


def _split_bf16(a):
    hi = a.astype(BF16)
    lo = (a - hi.astype(F32)).astype(BF16)
    return hi, lo


def _rms(x, gain):
    return x * lax.rsqrt(jnp.mean(x * x, axis=-1, keepdims=True) + RMS_EPS) * gain


def _norm_matmul_kernel(x_ref, g_ref, w_ref, o_ref, xn_ref):
    @pl.when(pl.program_id(1) == 0)
    def _():
        xn_ref[...] = _rms(x_ref[...], g_ref[...]).astype(BF16)

    o_ref[...] = jnp.dot(xn_ref[...], w_ref[...], preferred_element_type=F32)


def _norm_matmul(x, gain, w, tm, tn):
    T, D = x.shape
    N = w.shape[1]
    return pl.pallas_call(
        _norm_matmul_kernel,
        grid=(T // tm, N // tn),
        in_specs=[pl.BlockSpec((tm, D), lambda i, j: (i, 0)),
                  pl.BlockSpec((1, D), lambda i, j: (0, 0)),
                  pl.BlockSpec((D, tn), lambda i, j: (0, j))],
        out_specs=pl.BlockSpec((tm, tn), lambda i, j: (i, j)),
        out_shape=jax.ShapeDtypeStruct((T, N), F32),
        scratch_shapes=[pltpu.VMEM((tm, D), BF16)],
        compiler_params=_cparams("parallel", "arbitrary"),
        name="norm_matmul",
    )(x, gain.reshape(1, D), w)


def _norm_matmul2_kernel(x_ref, g_ref, w_ref, w2_ref, o_ref, o2_ref, xn_ref):
    @pl.when(pl.program_id(1) == 0)
    def _():
        xn = _rms(x_ref[...], g_ref[...])
        xn_ref[...] = xn.astype(BF16)
        x_hi, x_lo = _split_bf16(xn)
        w_hi, w_lo = _split_bf16(w2_ref[...])
        o2_ref[...] = (jnp.dot(x_hi, w_hi, preferred_element_type=F32)
                       + jnp.dot(x_lo, w_hi, preferred_element_type=F32)
                       + jnp.dot(x_hi, w_lo, preferred_element_type=F32))

    o_ref[...] = jnp.dot(xn_ref[...], w_ref[...], preferred_element_type=F32)


def _norm_matmul2(x, gain, w, w2, tm, tn):
    T, D = x.shape
    N = w.shape[1]
    N2 = w2.shape[1]
    return pl.pallas_call(
        _norm_matmul2_kernel,
        grid=(T // tm, N // tn),
        in_specs=[pl.BlockSpec((tm, D), lambda i, j: (i, 0)),
                  pl.BlockSpec((1, D), lambda i, j: (0, 0)),
                  pl.BlockSpec((D, tn), lambda i, j: (0, j)),
                  pl.BlockSpec((D, N2), lambda i, j: (0, 0))],
        out_specs=[pl.BlockSpec((tm, tn), lambda i, j: (i, j)),
                   pl.BlockSpec((tm, N2), lambda i, j: (i, 0))],
        out_shape=[jax.ShapeDtypeStruct((T, N), F32), jax.ShapeDtypeStruct((T, N2), F32)],
        scratch_shapes=[pltpu.VMEM((tm, D), BF16)],
        compiler_params=_cparams("parallel", "arbitrary"),
        name="norm_matmul2",
    )(x, gain.reshape(1, D), w, w2)


def _matmul_residual_kernel(a_ref, w_ref, x_ref, o_ref):
    o_ref[...] = x_ref[...] + jnp.dot(a_ref[...], w_ref[...], preferred_element_type=F32)


def _matmul_residual(a, w, x, tm, tn):
    T, K = a.shape
    N = w.shape[1]
    return pl.pallas_call(
        _matmul_residual_kernel,
        grid=(T // tm, N // tn),
        in_specs=[pl.BlockSpec((tm, K), lambda i, j: (i, 0)),
                  pl.BlockSpec((K, tn), lambda i, j: (0, j)),
                  pl.BlockSpec((tm, tn), lambda i, j: (i, j))],
        out_specs=pl.BlockSpec((tm, tn), lambda i, j: (i, j)),
        out_shape=jax.ShapeDtypeStruct((T, N), F32),
        compiler_params=_cparams("parallel", "arbitrary"),
        name="matmul_residual",
    )(a, w, x)


def _attn_kernel(main_ref, prev_ref, next_ref, qg_ref, kg_ref, sink_ref, bias_ref, o_ref):
    n = pl.program_id(1)
    nb = pl.num_programs(1)
    dh, blk = A_HEAD_DIM, A_BLOCK
    kv_cols = A_KV_HEADS * dh
    main = main_ref[...]
    kv = jnp.concatenate([prev_ref[...], main[:, A_HEADS * dh:], next_ref[...]], axis=0)
    col = lax.broadcasted_iota(jnp.int32, (1, 3 * blk), 1)
    outside = ((col < blk) & (n == 0)) | ((col >= 2 * blk) & (n == nb - 1))
    edge = jnp.where(outside, NEG_INF, 0.0)
    qg = qg_ref[...] * (dh ** -0.5)
    groups = range(A_KV_HEADS)
    kn = [_rms(kv[:, g * dh:(g + 1) * dh], kg_ref[...]).astype(BF16) for g in groups]
    q4 = [jnp.concatenate([_rms(main[:, (A_REP * g + r) * dh:(A_REP * g + r + 1) * dh], qg).astype(BF16)
                           for r in range(A_REP)], axis=0) for g in groups]
    s = [_bdot_nt(q4[g], kn[g]) + bias_ref[g] + edge for g in groups]
    m = [jnp.maximum(jnp.max(s[g], axis=-1, keepdims=True), sink_ref[g]) for g in groups]
    p = [jnp.exp(s[g] - m[g]) for g in groups]
    denom = [jnp.sum(p[g], axis=-1, keepdims=True) + jnp.exp(sink_ref[g] - m[g]) for g in groups]
    o = [jnp.dot(p[g].astype(BF16), kv[:, kv_cols + g * dh: kv_cols + (g + 1) * dh].astype(BF16),
                 preferred_element_type=F32) / denom[g] for g in groups]
    for g in groups:
        for r in range(A_REP):
            h = A_REP * g + r
            o_ref[:, h * dh:(h + 1) * dh] = o[g][r * blk:(r + 1) * blk].astype(o_ref.dtype)


def _attn_tables(sink):
    blk = A_BLOCK
    slopes = np.array([2.0 ** (-8.0 * (h + 1) / A_HEADS) for h in range(A_HEADS)], np.float32)
    qi = np.arange(blk)
    kj = np.arange(3 * blk)
    dist = np.abs(blk + qi[:, None] - kj[None, :]).astype(np.float32)
    bias = np.where(dist[None] <= blk, -slopes[:, None, None] * dist[None], np.float32(NEG_INF))
    bias = bias.astype(np.float32).reshape(A_KV_HEADS, A_REP * blk, 3 * blk)
    sink_rows = jnp.repeat(sink.astype(F32).reshape(A_KV_HEADS, A_REP), blk, axis=1)[..., None]
    return jnp.asarray(bias), sink_rows


def _attention(qkv, q_gain, k_gain, sink, batch, seq):
    T, W = qkv.shape
    blk, dh = A_BLOCK, A_HEAD_DIM
    nb = seq // blk
    kv_w = 2 * A_KV_HEADS * dh
    kv_blk = (A_HEADS * dh) // kv_w
    bias, sink_rows = _attn_tables(sink)
    return pl.pallas_call(
        _attn_kernel,
        grid=(batch, nb),
        in_specs=[
            pl.BlockSpec((blk, W), lambda b, n: (b * nb + n, 0)),
            pl.BlockSpec((blk, kv_w), lambda b, n: (b * nb + jnp.maximum(n - 1, 0), kv_blk)),
            pl.BlockSpec((blk, kv_w), lambda b, n: (b * nb + jnp.minimum(n + 1, nb - 1), kv_blk)),
            pl.BlockSpec((1, dh), lambda b, n: (0, 0)),
            pl.BlockSpec((1, dh), lambda b, n: (0, 0)),
            pl.BlockSpec((A_KV_HEADS, A_REP * blk, 1), lambda b, n: (0, 0, 0)),
            pl.BlockSpec((A_KV_HEADS, A_REP * blk, 3 * blk), lambda b, n: (0, 0, 0)),
        ],
        out_specs=pl.BlockSpec((blk, A_HEADS * dh), lambda b, n: (b * nb + n, 0)),
        out_shape=jax.ShapeDtypeStruct((T, A_HEADS * dh), BF16),
        compiler_params=_cparams("parallel", "parallel"),
        name="window_attention",
    )(qkv, qkv, qkv, q_gain.reshape(1, dh), k_gain.reshape(1, dh), sink_rows, bias)


def _route(logits):
    lane = lax.broadcasted_iota(jnp.int32, logits.shape, 1).astype(F32)
    big = jnp.float32(1e9)
    is_g = (lane >= N_EXPERTS) & (lane < N_EXPERTS + N_GROUPS)
    lg = jnp.where(is_g, logits, NEG_INF)
    gmax = jnp.max(lg, axis=-1, keepdims=True)
    gidx = jnp.min(jnp.where(is_g & (lg == gmax), lane, big), axis=-1, keepdims=True) - N_EXPERTS
    g_prob = 1.0 / jnp.sum(jnp.where(is_g, jnp.exp(lg - gmax), 0.0), axis=-1, keepdims=True)
    lo = gidx * EXPERTS_PER_GROUP
    in_grp = (lane >= lo) & (lane < lo + EXPERTS_PER_GROUP)
    le = jnp.where(in_grp, logits, NEG_INF)
    emax = jnp.max(le, axis=-1, keepdims=True)
    ex = jnp.where(in_grp, jnp.exp(le - emax), 0.0)
    prob = ex / jnp.sum(ex, axis=-1, keepdims=True)
    cand = jnp.where(in_grp, prob, -1.0)
    p1 = jnp.max(cand, axis=-1, keepdims=True)
    i1 = jnp.min(jnp.where(cand == p1, lane, big), axis=-1, keepdims=True)
    cand2 = jnp.where(lane == i1, -1.0, cand)
    p2 = jnp.max(cand2, axis=-1, keepdims=True)
    i2 = jnp.min(jnp.where(cand2 == p2, lane, big), axis=-1, keepdims=True)
    scale = g_prob / (p1 + p2)
    return i1, i2, p1 * scale, p2 * scale


RT_E1, RT_E2, RT_G1, RT_G2, RT_R1, RT_R2 = range(6)


def _pack_bf16_pairs(a):
    n = a.shape[1] // 2
    hi = lax.bitcast_convert_type(a[:, :n].astype(BF16).astype(F32), jnp.uint32)
    lo = lax.bitcast_convert_type(a[:, n:].astype(BF16).astype(F32), jnp.uint32)
    return hi | (lo >> 16)


def _unpack_bf16_pairs(p):
    hi = lax.bitcast_convert_type(p & jnp.uint32(0xFFFF0000), F32)
    lo = lax.bitcast_convert_type(p << 16, F32)
    return jnp.concatenate([hi, lo], axis=1)


LANES = 128
D_MODEL = 1024
TOKEN_ROWS = D_MODEL // 2 // LANES


def _store_token_major(ref, packed):
    m, w = packed.shape
    s_per = w // LANES
    for s in range(s_per):
        ref[pl.ds(s, m, stride=s_per), :] = packed[:, s * LANES:(s + 1) * LANES]


def _load_token_major(ref, m, s_per):
    return jnp.concatenate([ref[pl.ds(s, m, stride=s_per), :] for s in range(s_per)], axis=1)


def _router_kernel(x_ref, g_ref, wr_ref, br_ref, h_ref, route_ref, counts_ref, carry_ref):
    @pl.when(pl.program_id(0) == 0)
    def _():
        carry_ref[...] = jnp.zeros_like(carry_ref)

    hn = _rms(x_ref[...], g_ref[...])
    _store_token_major(h_ref, _pack_bf16_pairs(hn))
    h_hi, h_lo = _split_bf16(hn)
    w_hi, w_lo = _split_bf16(wr_ref[...])
    logits = (jnp.dot(h_hi, w_hi, preferred_element_type=F32)
              + jnp.dot(h_lo, w_hi, preferred_element_type=F32)
              + jnp.dot(h_hi, w_lo, preferred_element_type=F32)) + br_ref[...]
    i1, i2, g1, g2 = _route(logits)
    tm = logits.shape[0]
    lane = lax.broadcasted_iota(jnp.int32, logits.shape, 1).astype(F32)
    chosen = ((lane == i1) | (lane == i2)).astype(BF16)
    earlier = (lax.broadcasted_iota(jnp.int32, (tm, tm), 1)
               < lax.broadcasted_iota(jnp.int32, (tm, tm), 0)).astype(BF16)
    before = jnp.dot(earlier, chosen, preferred_element_type=F32) + carry_ref[...]
    r1 = jnp.sum(jnp.where(lane == i1, before, 0.0), axis=-1, keepdims=True)
    r2 = jnp.sum(jnp.where(lane == i2, before, 0.0), axis=-1, keepdims=True)
    carry_ref[...] += jnp.sum(chosen.astype(F32), axis=0, keepdims=True)
    rec = jnp.zeros_like(logits)
    for slot, val in ((RT_E1, i1), (RT_E2, i2), (RT_G1, g1), (RT_G2, g2), (RT_R1, r1), (RT_R2, r2)):
        rec = jnp.where(lane == slot, val, rec)
    route_ref[...] = rec
    counts_ref[...] = carry_ref[...]


def _router(x, gain, w_group, b_group, w_expert, b_expert, tm):
    T, D = x.shape
    pad = ROUTER_LANES - N_EXPERTS - N_GROUPS
    wr = jnp.concatenate([w_expert, w_group, jnp.zeros((D, pad), F32)], axis=1)
    br = jnp.concatenate([b_expert, b_group, jnp.zeros((pad,), F32)]).reshape(1, ROUTER_LANES)
    return pl.pallas_call(
        _router_kernel,
        grid=(T // tm,),
        in_specs=[pl.BlockSpec((tm, D), lambda i: (i, 0)),
                  pl.BlockSpec((1, D), lambda i: (0, 0)),
                  pl.BlockSpec((D, ROUTER_LANES), lambda i: (0, 0)),
                  pl.BlockSpec((1, ROUTER_LANES), lambda i: (0, 0))],
        out_specs=[pl.BlockSpec((tm * TOKEN_ROWS, LANES), lambda i: (i, 0)),
                   pl.BlockSpec((tm, ROUTER_LANES), lambda i: (i, 0)),
                   pl.BlockSpec((1, ROUTER_LANES), lambda i: (0, 0))],
        out_shape=[jax.ShapeDtypeStruct((T * TOKEN_ROWS, LANES), jnp.uint32),
                   jax.ShapeDtypeStruct((T, ROUTER_LANES), F32),
                   jax.ShapeDtypeStruct((1, ROUTER_LANES), F32)],
        scratch_shapes=[pltpu.VMEM((1, ROUTER_LANES), F32)],
        compiler_params=_cparams("arbitrary"),
        name="moe_router",
    )(x, gain.reshape(1, D), wr, br)


MOE_TILE = 256
MOE_SLOTS = 2
DMA_UNROLL = 8


def _token_copy(src, src_row, dst, dst_row, sem):
    return pltpu.make_async_copy(src.at[pl.ds(pl.multiple_of(src_row, TOKEN_ROWS), TOKEN_ROWS)],
                                 dst.at[pl.ds(pl.multiple_of(dst_row, TOKEN_ROWS), TOKEN_ROWS)], sem)


def _dispatch_kernel(pos_ref, h_ref, zeros_hbm, xs_hbm, sem):
    del zeros_hbm
    tm = h_ref.shape[0] // TOKEN_ROWS

    def start(g, c):
        r0 = pl.multiple_of(g * DMA_UNROLL, DMA_UNROLL)
        for j in range(DMA_UNROLL):
            for k in range(MOE_SLOTS):
                _token_copy(h_ref, (r0 + j) * TOKEN_ROWS, xs_hbm, pos_ref[MOE_SLOTS * (r0 + j) + k], sem).start()
        return c

    def wait(r, c):
        for k in range(MOE_SLOTS):
            _token_copy(h_ref, 0, xs_hbm, 0, sem).wait()
        return c

    lax.fori_loop(0, tm // DMA_UNROLL, start, 0)
    lax.fori_loop(0, tm, wait, 0, unroll=DMA_UNROLL)


def _dispatch(h, pos, n_rows, tm):
    T = h.shape[0] // TOKEN_ROWS
    zeros = jnp.zeros((n_rows * TOKEN_ROWS, LANES), h.dtype)
    return pl.pallas_call(
        _dispatch_kernel,
        grid=(T // tm,),
        in_specs=[pl.BlockSpec((MOE_SLOTS * tm,), lambda i: (i,), memory_space=pltpu.SMEM),
                  pl.BlockSpec((tm * TOKEN_ROWS, LANES), lambda i: (i, 0)),
                  pl.BlockSpec(memory_space=pl.ANY)],
        out_specs=pl.BlockSpec(memory_space=pl.ANY),
        out_shape=jax.ShapeDtypeStruct(zeros.shape, h.dtype),
        scratch_shapes=[pltpu.SemaphoreType.DMA(())],
        input_output_aliases={2: 0},
        compiler_params=_cparams("arbitrary"),
        name="moe_dispatch",
    )(pos, h, zeros)


def _expert_kernel(te_ref, nv_ref, xs_ref, wg_ref, wu_ref, wd_ref, ys_ref, wgu_s, wd_s):
    i = pl.program_id(0)
    nv = nv_ref[0]
    valid = i < nv
    ic = jnp.minimum(i, nv - 1)
    changed = (i == 0) | (te_ref[ic] != te_ref[jnp.maximum(ic - 1, 0)])

    @pl.when(valid & changed)
    def _():
        wgu_s[:, :D_EXPERT] = wg_ref[0, 0].astype(BF16)
        wgu_s[:, D_EXPERT:] = wu_ref[0, 0].astype(BF16)
        wd_s[...] = wd_ref[0, 0].astype(BF16)

    @pl.when(valid)
    def _():
        x = _unpack_bf16_pairs(_load_token_major(xs_ref, MOE_TILE, TOKEN_ROWS)).astype(BF16)
        gu = jnp.dot(x, wgu_s[...], preferred_element_type=F32)
        gate, up = gu[:, :D_EXPERT], gu[:, D_EXPERT:]
        hid = (gate * jax.nn.sigmoid(gate)) * up
        y = jnp.dot(hid.astype(BF16), wd_s[...], preferred_element_type=F32)
        _store_token_major(ys_ref, _pack_bf16_pairs(y))

    @pl.when(jnp.logical_not(valid))
    def _():
        ys_ref[...] = jnp.zeros_like(ys_ref)


def _experts(xs, tile_expert, n_valid, w_gate, w_up, w_down, layer):
    D = TOKEN_ROWS * LANES * 2
    blk = MOE_TILE * TOKEN_ROWS
    n_tiles = xs.shape[0] // blk
    row = lambda i, te, nv: (jnp.minimum(i, nv[0] - 1), 0)
    wsel = lambda i, te, nv: (layer, te[jnp.minimum(i, nv[0] - 1)], 0, 0)
    return pl.pallas_call(
        _expert_kernel,
        grid_spec=pltpu.PrefetchScalarGridSpec(
            num_scalar_prefetch=2,
            grid=(n_tiles,),
            in_specs=[pl.BlockSpec((blk, LANES), row),
                      pl.BlockSpec((1, 1, D, D_EXPERT), wsel),
                      pl.BlockSpec((1, 1, D, D_EXPERT), wsel),
                      pl.BlockSpec((1, 1, D_EXPERT, D), wsel)],
            out_specs=pl.BlockSpec((blk, LANES), lambda i, te, nv: (i, 0)),
            scratch_shapes=[pltpu.VMEM((D, 2 * D_EXPERT), BF16), pltpu.VMEM((D_EXPERT, D), BF16)]),
        out_shape=jax.ShapeDtypeStruct(xs.shape, jnp.uint32),
        compiler_params=_cparams("arbitrary"),
        name="moe_experts",
    )(tile_expert, n_valid, xs, w_gate, w_up, w_down)


def _combine_kernel(pos_ref, x_ref, route_ref, ys_hbm, o_ref, buf, sem):
    tm = x_ref.shape[0]

    def start(g, c):
        r0 = pl.multiple_of(g * DMA_UNROLL, DMA_UNROLL)
        for j in range(DMA_UNROLL):
            for k in range(MOE_SLOTS):
                _token_copy(ys_hbm, pos_ref[MOE_SLOTS * (r0 + j) + k], buf.at[k], (r0 + j) * TOKEN_ROWS, sem).start()
        return c

    def wait(r, c):
        for k in range(MOE_SLOTS):
            _token_copy(ys_hbm, 0, buf.at[k], 0, sem).wait()
        return c

    lax.fori_loop(0, tm // DMA_UNROLL, start, 0)
    lax.fori_loop(0, tm, wait, 0, unroll=DMA_UNROLL)
    rec = route_ref[...]
    y1 = _unpack_bf16_pairs(_load_token_major(buf.at[0], tm, TOKEN_ROWS))
    y2 = _unpack_bf16_pairs(_load_token_major(buf.at[1], tm, TOKEN_ROWS))
    o_ref[...] = x_ref[...] + rec[:, RT_G1:RT_G1 + 1] * y1 + rec[:, RT_G2:RT_G2 + 1] * y2


def _combine(x, route, pos, ys, tm):
    T, D = x.shape
    return pl.pallas_call(
        _combine_kernel,
        grid=(T // tm,),
        in_specs=[pl.BlockSpec((MOE_SLOTS * tm,), lambda i: (i,), memory_space=pltpu.SMEM),
                  pl.BlockSpec((tm, D), lambda i: (i, 0)),
                  pl.BlockSpec((tm, ROUTER_LANES), lambda i: (i, 0)),
                  pl.BlockSpec(memory_space=pl.ANY)],
        out_specs=pl.BlockSpec((tm, D), lambda i: (i, 0)),
        out_shape=jax.ShapeDtypeStruct((T, D), F32),
        scratch_shapes=[pltpu.VMEM((MOE_SLOTS, tm * TOKEN_ROWS, LANES), ys.dtype), pltpu.SemaphoreType.DMA(())],
        compiler_params=_cparams("arbitrary"),
        name="moe_combine",
    )(pos, x, route, ys)


def _moe(x, gain, w_group, b_group, w_expert, b_expert, w_gate, w_up, w_down, layer):
    T, D = x.shape
    n_tiles = (MOE_SLOTS * T + N_EXPERTS * (MOE_TILE - 1)) // MOE_TILE + 1
    h, route, counts = _router(x, gain, w_group, b_group, w_expert, b_expert, tm=512)
    counts = counts[0, :N_EXPERTS].astype(jnp.int32)
    padded = (counts + MOE_TILE - 1) // MOE_TILE * MOE_TILE
    ends = jnp.cumsum(padded)
    starts = ends - padded
    eid = route[:, RT_E1:RT_E2 + 1].astype(jnp.int32)
    rank = route[:, RT_R1:RT_R2 + 1].astype(jnp.int32)
    onehot = eid[..., None] == jnp.arange(N_EXPERTS, dtype=jnp.int32)
    pos = (rank + jnp.sum(jnp.where(onehot, starts, 0), axis=-1)).reshape(-1)
    pos = pos * TOKEN_ROWS
    tile_start = jnp.arange(n_tiles, dtype=jnp.int32) * MOE_TILE
    tile_expert = jnp.minimum(jnp.sum(tile_start[:, None] >= ends[None, :], axis=-1), N_EXPERTS - 1).astype(jnp.int32)
    n_valid = (ends[-1:] // MOE_TILE).astype(jnp.int32)
    xs = _dispatch(h, pos, n_tiles * MOE_TILE, tm=512)
    ys = _experts(xs, tile_expert, n_valid, w_gate, w_up, w_down, layer)
    return _combine(x, route, pos, ys, tm=256)


GDN_HALO = 8


def _gdn_conv_kernel(u_ref, up_ref, un_ref, w_ref, ab_ref, aexp_ref, dtb_ref, o_ref, gb_ref, *, tiles_per_seq):
    i = pl.program_id(0)
    j = pl.program_id(1)
    tm = u_ref.shape[0]
    first = (i % tiles_per_seq) == 0
    last = (i % tiles_per_seq) == tiles_per_seq - 1
    u = u_ref[...]
    prev = jnp.where(first, 0.0, up_ref[...])
    nxt = jnp.where(last, 0.0, un_ref[...])
    ext = jnp.concatenate([prev, u, nxt], axis=0)
    w = w_ref[...]
    h0 = GDN_HALO - B_CONV // 2
    y = w[0:1] * ext[h0:h0 + tm]
    for t in range(1, B_CONV):
        y = y + w[t:t + 1] * ext[h0 + t:h0 + t + tm]
    y = y * jax.nn.sigmoid(y)
    qscale = jnp.where(j == 0, B_HEAD_DIM ** -0.5, 1.0)
    for h in range(B_HEADS):
        slab = y[:, h * B_HEAD_DIM:(h + 1) * B_HEAD_DIM]
        inv = lax.rsqrt(jnp.sum(slab * slab, axis=-1, keepdims=True) + RMS_EPS) * qscale
        o_ref[0, :, h * B_HEAD_DIM:(h + 1) * B_HEAD_DIM] = slab * jnp.where(j < 2, inv, 1.0)

    @pl.when(j == 0)
    def _():
        ab = ab_ref[...]
        lane = lax.broadcasted_iota(jnp.int32, ab.shape, 1)
        z = ab + dtb_ref[...]
        softplus = jnp.maximum(z, 0.0) + jnp.log1p(jnp.exp(-jnp.abs(z)))
        gb_ref[...] = jnp.where(lane < 2 * B_HEADS, -aexp_ref[...] * softplus, jax.nn.sigmoid(ab))


def _gdn_conv(qkvz, ab, conv_w, a_log, dt_bias, seq, tm):
    T = qkvz.shape[0]
    D = B_HEADS * B_HEAD_DIM
    hb = tm // GDN_HALO
    n_halo = T // GDN_HALO
    pad = ROUTER_LANES - 2 * B_HEADS
    aexp = jnp.concatenate([jnp.exp(a_log.astype(F32)).reshape(-1), jnp.zeros((pad,), F32)]).reshape(1, -1)
    dtb = jnp.concatenate([dt_bias.astype(F32).reshape(-1), jnp.zeros((pad,), F32)]).reshape(1, -1)
    return pl.pallas_call(
        functools.partial(_gdn_conv_kernel, tiles_per_seq=seq // tm),
        grid=(T // tm, 3),
        in_specs=[pl.BlockSpec((tm, D), lambda i, j: (i, j)),
                  pl.BlockSpec((GDN_HALO, D), lambda i, j: (jnp.maximum(i * hb - 1, 0), j)),
                  pl.BlockSpec((GDN_HALO, D), lambda i, j: (jnp.minimum((i + 1) * hb, n_halo - 1), j)),
                  pl.BlockSpec((B_CONV, D), lambda i, j: (0, j)),
                  pl.BlockSpec((tm, ROUTER_LANES), lambda i, j: (i, 0)),
                  pl.BlockSpec((1, ROUTER_LANES), lambda i, j: (0, 0)),
                  pl.BlockSpec((1, ROUTER_LANES), lambda i, j: (0, 0))],
        out_specs=[pl.BlockSpec((1, tm, D), lambda i, j: (j, i, 0)),
                   pl.BlockSpec((tm, ROUTER_LANES), lambda i, j: (i, 0))],
        out_shape=[jax.ShapeDtypeStruct((3, T, D), F32), jax.ShapeDtypeStruct((T, ROUTER_LANES), F32)],
        compiler_params=_cparams("parallel", "arbitrary"),
        name="gdn_conv",
    )(qkvz, qkvz, qkvz, conv_w, ab, aexp, dtb)


def _gdn_gate_terms(gb, incl):
    C = gb.shape[0]
    lane = lax.broadcasted_iota(jnp.int32, gb.shape, 1)
    g_hi, g_lo = _split_bf16(jnp.where(lane < 2 * B_HEADS, gb, 0.0))
    tri = incl.astype(BF16)
    gc = jnp.dot(tri, g_hi, preferred_element_type=F32) + jnp.dot(tri, g_lo, preferred_element_type=F32)
    return gc, gc.T


def _gdn_scan_kernel(qf_ref, kf_ref, vf_ref, gf_ref, qb_ref, kb_ref, vb_ref, gbw_ref, of_ref, ob_ref, sf_ref, sb_ref):
    @pl.when(pl.program_id(1) == 0)
    def _():
        sf_ref[...] = jnp.zeros_like(sf_ref)
        sb_ref[...] = jnp.zeros_like(sb_ref)

    C = gf_ref.shape[0]
    dk = B_HEAD_DIM
    row = lax.broadcasted_iota(jnp.int32, (C, C), 0)
    colm = lax.broadcasted_iota(jnp.int32, (C, C), 1)
    eye = (row == colm).astype(F32)
    incl = (row >= colm, row <= colm)
    strict = (row > colm, row < colm)
    qkv_refs = ((qf_ref, kf_ref, vf_ref), (qb_ref, kb_ref, vb_ref))
    gbs = (gf_ref[...], gbw_ref[...])
    s_refs = (sf_ref, sb_ref)
    o_refs = (of_ref, ob_ref)
    gates = [_gdn_gate_terms(gbs[d], incl[d]) for d in range(2)]
    glast = [gates[0][0][C - 1:C], gates[1][0][0:1]]
    units = [(d, h) for d in range(2) for h in range(B_HEADS)]

    def lane_of(d, h):
        return d * B_HEADS + h

    def cols(h):
        return slice(h * dk, (h + 1) * dk)

    v_b, kb_l, qd_bf, kd_bf, a_l, dec_l, egc_l = [], [], [], [], [], [], []
    for d, h in units:
        r = lane_of(d, h)
        gc, gct = gates[d]
        gcol = gc[:, r:r + 1]
        beta = gbs[d][:, 2 * B_HEADS + r:2 * B_HEADS + r + 1]
        q_ref, k_ref, v_ref = qkv_refs[d]
        qh, kh, vh = q_ref[0, :, cols(h)], k_ref[0, :, cols(h)], v_ref[0, :, cols(h)]
        egc = jnp.exp(gcol)
        kb = kh * beta
        khb = kh.astype(BF16)
        a_l.append(_bdot_nt(jnp.concatenate([kb, qh], axis=0), khb))
        dec_l.append(jnp.exp(jnp.where(incl[d], gcol - gct[r:r + 1, :], NEG_INF)))
        v_b.append(vh * beta)
        kb_l.append(kb)
        egc_l.append(egc)
        qd_bf.append((qh * egc).astype(BF16))
        kd_bf.append((kh * jnp.exp(glast[d][:, r:r + 1] - gcol)).astype(BF16))
    x_l = [-jnp.where(strict[d], a[:C] * dec, 0.0) for (d, h), a, dec in zip(units, a_l, dec_l)]
    intra_bf = [(a[C:] * dec).astype(BF16) for a, dec in zip(a_l, dec_l)]
    p_l = [eye + x for x in x_l]
    n = 1
    while 2 * n < C:
        x_l = [_bdot(x, x) for x in x_l]
        n *= 2
        p_l = [p + _bdot(p, x) for p, x in zip(p_l, x_l)]
    sol_l = [_bdot(p, jnp.concatenate([vb, kb * egc], axis=1))
             for p, vb, kb, egc in zip(p_l, v_b, kb_l, egc_l)]
    st_l = [s_refs[d][h] for d, h in units]
    wq_l = [_bdot(jnp.concatenate([sol[:, dk:].astype(BF16), qd], axis=0), st)
            for sol, qd, st in zip(sol_l, qd_bf, st_l)]
    vn_l = [sol[:, :dk] - wq[:C] for sol, wq in zip(sol_l, wq_l)]
    for (d, h), wq, intra, vn in zip(units, wq_l, intra_bf, vn_l):
        o_refs[d][:, cols(h)] = wq[C:] + _bdot(intra, vn)
    for (d, h), st, kd, vn in zip(units, st_l, kd_bf, vn_l):
        r = lane_of(d, h)
        s_refs[d][h] = st * jnp.exp(glast[d][:, r:r + 1]) + _bdot_tn(kd, vn)


def _gdn_scan(qkv, gb, batch, seq, chunk):
    _, T, D = qkv.shape
    nc = seq // chunk
    fwd = lambda b, c: b * nc + c
    bwd = lambda b, c: b * nc + (nc - 1 - c)
    part = lambda p, f: pl.BlockSpec((1, chunk, D), lambda b, c: (p, f(b, c), 0))
    gspec = lambda f: pl.BlockSpec((chunk, ROUTER_LANES), lambda b, c: (f(b, c), 0))
    ospec = lambda f: pl.BlockSpec((chunk, D), lambda b, c: (f(b, c), 0))
    return pl.pallas_call(
        _gdn_scan_kernel,
        grid=(batch, nc),
        in_specs=[part(0, fwd), part(1, fwd), part(2, fwd), gspec(fwd),
                  part(0, bwd), part(1, bwd), part(2, bwd), gspec(bwd)],
        out_specs=[ospec(fwd), ospec(bwd)],
        out_shape=[jax.ShapeDtypeStruct((T, D), F32), jax.ShapeDtypeStruct((T, D), F32)],
        scratch_shapes=[pltpu.VMEM((B_HEADS, B_HEAD_DIM, B_HEAD_DIM), F32),
                        pltpu.VMEM((B_HEADS, B_HEAD_DIM, B_HEAD_DIM), F32)],
        compiler_params=_cparams("parallel", "arbitrary"),
        name="gdn_scan",
    )(qkv, qkv, qkv, gb, qkv, qkv, qkv, gb)


def _gdn_out_kernel(of_ref, ob_ref, z_ref, og_ref, w_ref, x_ref, o_ref, a_ref):
    @pl.when(pl.program_id(1) == 0)
    def _():
        o = of_ref[...] + ob_ref[...]
        z = z_ref[...]
        for h in range(B_HEADS):
            sl = slice(h * B_HEAD_DIM, (h + 1) * B_HEAD_DIM)
            zh = z[:, sl]
            a_ref[:, sl] = (_rms(o[:, sl], og_ref[...]) * (zh * jax.nn.sigmoid(zh))).astype(BF16)

    o_ref[...] = x_ref[...] + jnp.dot(a_ref[...], w_ref[...], preferred_element_type=F32)


def _gdn_out(o_f, o_b, qkvz, o_gain, w, x, tm, tn):
    T, D = x.shape
    return pl.pallas_call(
        _gdn_out_kernel,
        grid=(T // tm, D // tn),
        in_specs=[pl.BlockSpec((tm, D), lambda i, j: (i, 0)),
                  pl.BlockSpec((tm, D), lambda i, j: (i, 0)),
                  pl.BlockSpec((tm, D), lambda i, j: (i, 3)),
                  pl.BlockSpec((1, B_HEAD_DIM), lambda i, j: (0, 0)),
                  pl.BlockSpec((D, tn), lambda i, j: (0, j)),
                  pl.BlockSpec((tm, tn), lambda i, j: (i, j))],
        out_specs=pl.BlockSpec((tm, tn), lambda i, j: (i, j)),
        out_shape=jax.ShapeDtypeStruct((T, D), F32),
        scratch_shapes=[pltpu.VMEM((tm, D), BF16)],
        compiler_params=_cparams("parallel", "arbitrary"),
        name="gdn_out",
    )(o_f, o_b, qkvz, o_gain.reshape(1, B_HEAD_DIM), w, x)


def _attention_layer(x, gain, w_in, q_gain, k_gain, sink, w_out, batch, seq):
    qkv = _norm_matmul(x, gain, w_in.astype(BF16), tm=1024, tn=768)
    a = _attention(qkv, q_gain, k_gain, sink, batch, seq)
    return _matmul_residual(a, w_out.astype(BF16), x, tm=1024, tn=1024)


def _gdn_layer(x, gain, w_in, conv_w, a_log, dt_bias, o_gain, w_out, batch, seq):
    D = x.shape[1]
    w_main = w_in[:, :4 * D].astype(BF16)
    pad = ROUTER_LANES - 4 * B_HEADS
    w_ab = jnp.concatenate([w_in[:, 4 * D:], jnp.zeros((D, pad), F32)], axis=1)
    qkvz, ab = _norm_matmul2(x, gain, w_main, w_ab, tm=1024, tn=1024)
    qkv, gb = _gdn_conv(qkvz, ab, conv_w, a_log, dt_bias, seq, tm=512)
    o_f, o_b = _gdn_scan(qkv, gb, batch, seq, GDN_CHUNK)
    return _gdn_out(o_f, o_b, qkvz, o_gain, w_out.astype(BF16), x, tm=512, tn=1024)


def kernel(x, norm_mix, norm_ffn, attn_w_in, attn_q_gain, attn_k_gain, attn_sink, attn_w_out, gdn_w_in, gdn_conv, gdn_a_log, gdn_dt_bias, gdn_o_gain, gdn_w_out, moe_w_group, moe_b_group, moe_w_expert, moe_b_expert, moe_w_gate, moe_w_up, moe_w_down):
    batch, seq, d_model = x.shape
    depth = norm_mix.shape[0]
    xt = x.reshape(batch * seq, d_model)
    for i in range(depth):
        j = i // 2
        if i % 2 == 0:
            xt = _attention_layer(xt, norm_mix[i], attn_w_in[j], attn_q_gain[j], attn_k_gain[j],
                                  attn_sink[j], attn_w_out[j], batch, seq)
        else:
            xt = _gdn_layer(xt, norm_mix[i], gdn_w_in[j], gdn_conv[j], gdn_a_log[j], gdn_dt_bias[j],
                            gdn_o_gain[j], gdn_w_out[j], batch, seq)
        xt = _moe(xt, norm_ffn[i], moe_w_group[i], moe_b_group[i], moe_w_expert[i], moe_b_expert[i],
                  moe_w_gate, moe_w_up, moe_w_down, layer=i)
    return xt.reshape(batch, seq, d_model)
```

```python
import functools
import math

import jax
import jax.numpy as jnp
import numpy as np
from jax import lax
from jax.experimental import pallas as pl
from jax.experimental.pallas import tpu as pltpu

RMS_EPS = 1e-6
NEG_INF = -1e30
F32 = jnp.float32
BF16 = jnp.bfloat16

A_HEADS = 16
A_KV_HEADS = 4
A_HEAD_DIM = 64
A_REP = A_HEADS // A_KV_HEADS
A_BLOCK = 128
B_HEADS = 8
B_HEAD_DIM = 128
B_CONV = 4
GDN_CHUNK = 64
N_GROUPS = 4
EXPERTS_PER_GROUP = 8
N_EXPERTS = 32
D_EXPERT = 256
ROUTER_LANES = 128

V7X_VMEM_LIMIT_BYTES = 56 * 1024 * 1024


def _cparams(*sem):
    return pltpu.CompilerParams(dimension_semantics=sem, vmem_limit_bytes=V7X_VMEM_LIMIT_BYTES)


def _bdot(a, b):
    return jnp.dot(a.astype(BF16), b.astype(BF16), preferred_element_type=F32)


def _bdot_nt(a, b):
    return lax.dot_general(a.astype(BF16), b.astype(BF16), (((1,), (1,)), ((), ())),
                           preferred_element_type=F32)


def _bdot_tn(a, b):
    return lax.dot_general(a.astype(BF16), b.astype(BF16), (((0,), (0,)), ((), ())),
                           preferred_element_type=F32)


def _split_bf16(a):
    hi = a.astype(BF16)
    lo = (a - hi.astype(F32)).astype(BF16)
    return hi, lo


def _rms(x, gain):
    return x * lax.rsqrt(jnp.mean(x * x, axis=-1, keepdims=True) + RMS_EPS) * gain


def _norm_matmul_kernel(x_ref, g_ref, w_ref, o_ref, xn_ref):
    @pl.when(pl.program_id(1) == 0)
    def _():
        xn_ref[...] = _rms(x_ref[...], g_ref[...]).astype(BF16)

    o_ref[...] = jnp.dot(xn_ref[...], w_ref[...], preferred_element_type=F32)


def _norm_matmul(x, gain, w, tm, tn):
    T, D = x.shape
    N = w.shape[1]
    return pl.pallas_call(
        _norm_matmul_kernel,
        grid=(T // tm, N // tn),
        in_specs=[pl.BlockSpec((tm, D), lambda i, j: (i, 0)),
                  pl.BlockSpec((1, D), lambda i, j: (0, 0)),
                  pl.BlockSpec((D, tn), lambda i, j: (0, j))],
        out_specs=pl.BlockSpec((tm, tn), lambda i, j: (i, j)),
        out_shape=jax.ShapeDtypeStruct((T, N), F32),
        scratch_shapes=[pltpu.VMEM((tm, D), BF16)],
        compiler_params=_cparams("parallel", "arbitrary"),
        name="norm_matmul",
    )(x, gain.reshape(1, D), w)


def _norm_matmul2_kernel(x_ref, g_ref, w_ref, w2_ref, o_ref, o2_ref, xn_ref):
    @pl.when(pl.program_id(1) == 0)
    def _():
        xn = _rms(x_ref[...], g_ref[...])
        xn_ref[...] = xn.astype(BF16)
        x_hi, x_lo = _split_bf16(xn)
        w_hi, w_lo = _split_bf16(w2_ref[...])
        o2_ref[...] = (jnp.dot(x_hi, w_hi, preferred_element_type=F32)
                       + jnp.dot(x_lo, w_hi, preferred_element_type=F32)
                       + jnp.dot(x_hi, w_lo, preferred_element_type=F32))

    o_ref[...] = jnp.dot(xn_ref[...], w_ref[...], preferred_element_type=F32).astype(o_ref.dtype)


def _norm_matmul2(x, gain, w, w2, tm, tn):
    T, D = x.shape
    N = w.shape[1]
    N2 = w2.shape[1]
    return pl.pallas_call(
        _norm_matmul2_kernel,
        grid=(T // tm, N // tn),
        in_specs=[pl.BlockSpec((tm, D), lambda i, j: (i, 0)),
                  pl.BlockSpec((1, D), lambda i, j: (0, 0)),
                  pl.BlockSpec((D, tn), lambda i, j: (0, j)),
                  pl.BlockSpec((D, N2), lambda i, j: (0, 0))],
        out_specs=[pl.BlockSpec((tm, tn), lambda i, j: (i, j)),
                   pl.BlockSpec((tm, N2), lambda i, j: (i, 0))],
        out_shape=[jax.ShapeDtypeStruct((T, N), BF16), jax.ShapeDtypeStruct((T, N2), F32)],
        scratch_shapes=[pltpu.VMEM((tm, D), BF16)],
        compiler_params=_cparams("parallel", "arbitrary"),
        name="norm_matmul2",
    )(x, gain.reshape(1, D), w, w2)


def _matmul_residual_kernel(a_ref, w_ref, x_ref, o_ref):
    o_ref[...] = x_ref[...] + jnp.dot(a_ref[...], w_ref[...], preferred_element_type=F32)


def _matmul_residual(a, w, x, tm, tn):
    T, K = a.shape
    N = w.shape[1]
    return pl.pallas_call(
        _matmul_residual_kernel,
        grid=(T // tm, N // tn),
        in_specs=[pl.BlockSpec((tm, K), lambda i, j: (i, 0)),
                  pl.BlockSpec((K, tn), lambda i, j: (0, j)),
                  pl.BlockSpec((tm, tn), lambda i, j: (i, j))],
        out_specs=pl.BlockSpec((tm, tn), lambda i, j: (i, j)),
        out_shape=jax.ShapeDtypeStruct((T, N), F32),
        compiler_params=_cparams("parallel", "arbitrary"),
        name="matmul_residual",
    )(a, w, x)


def _attn_kernel(main_ref, prev_ref, next_ref, qg_ref, kg_ref, sink_ref, bias_ref, o_ref):
    n = pl.program_id(1)
    nb = pl.num_programs(1)
    dh, blk = A_HEAD_DIM, A_BLOCK
    kv_cols = A_KV_HEADS * dh
    main = main_ref[...]
    kv = jnp.concatenate([prev_ref[...], main[:, A_HEADS * dh:], next_ref[...]], axis=0)
    col = lax.broadcasted_iota(jnp.int32, (1, 3 * blk), 1)
    outside = ((col < blk) & (n == 0)) | ((col >= 2 * blk) & (n == nb - 1))
    edge = jnp.where(outside, NEG_INF, 0.0)
    qg = qg_ref[...] * (dh ** -0.5)
    groups = range(A_KV_HEADS)
    kn = [_rms(kv[:, g * dh:(g + 1) * dh], kg_ref[...]).astype(BF16) for g in groups]
    q4 = [jnp.concatenate([_rms(main[:, (A_REP * g + r) * dh:(A_REP * g + r + 1) * dh], qg).astype(BF16)
                           for r in range(A_REP)], axis=0) for g in groups]
    s = [_bdot_nt(q4[g], kn[g]) + bias_ref[g] + edge for g in groups]
    m = [jnp.maximum(jnp.max(s[g], axis=-1, keepdims=True), sink_ref[g]) for g in groups]
    p = [jnp.exp(s[g] - m[g]) for g in groups]
    denom = [jnp.sum(p[g], axis=-1, keepdims=True) + jnp.exp(sink_ref[g] - m[g]) for g in groups]
    o = [jnp.dot(p[g].astype(BF16), kv[:, kv_cols + g * dh: kv_cols + (g + 1) * dh].astype(BF16),
                 preferred_element_type=F32) / denom[g] for g in groups]
    for g in groups:
        for r in range(A_REP):
            h = A_REP * g + r
            o_ref[:, h * dh:(h + 1) * dh] = o[g][r * blk:(r + 1) * blk].astype(o_ref.dtype)


def _attn_tables(sink):
    blk = A_BLOCK
    slopes = np.array([2.0 ** (-8.0 * (h + 1) / A_HEADS) for h in range(A_HEADS)], np.float32)
    qi = np.arange(blk)
    kj = np.arange(3 * blk)
    dist = np.abs(blk + qi[:, None] - kj[None, :]).astype(np.float32)
    bias = np.where(dist[None] <= blk, -slopes[:, None, None] * dist[None], np.float32(NEG_INF))
    bias = bias.astype(np.float32).reshape(A_KV_HEADS, A_REP * blk, 3 * blk)
    sink_rows = jnp.repeat(sink.astype(F32).reshape(A_KV_HEADS, A_REP), blk, axis=1)[..., None]
    return jnp.asarray(bias), sink_rows


def _attention(qkv, q_gain, k_gain, sink, batch, seq):
    T, W = qkv.shape
    blk, dh = A_BLOCK, A_HEAD_DIM
    nb = seq // blk
    kv_w = 2 * A_KV_HEADS * dh
    kv_blk = (A_HEADS * dh) // kv_w
    bias, sink_rows = _attn_tables(sink)
    return pl.pallas_call(
        _attn_kernel,
        grid=(batch, nb),
        in_specs=[
            pl.BlockSpec((blk, W), lambda b, n: (b * nb + n, 0)),
            pl.BlockSpec((blk, kv_w), lambda b, n: (b * nb + jnp.maximum(n - 1, 0), kv_blk)),
            pl.BlockSpec((blk, kv_w), lambda b, n: (b * nb + jnp.minimum(n + 1, nb - 1), kv_blk)),
            pl.BlockSpec((1, dh), lambda b, n: (0, 0)),
            pl.BlockSpec((1, dh), lambda b, n: (0, 0)),
            pl.BlockSpec((A_KV_HEADS, A_REP * blk, 1), lambda b, n: (0, 0, 0)),
            pl.BlockSpec((A_KV_HEADS, A_REP * blk, 3 * blk), lambda b, n: (0, 0, 0)),
        ],
        out_specs=pl.BlockSpec((blk, A_HEADS * dh), lambda b, n: (b * nb + n, 0)),
        out_shape=jax.ShapeDtypeStruct((T, A_HEADS * dh), BF16),
        compiler_params=_cparams("parallel", "parallel"),
        name="window_attention",
    )(qkv, qkv, qkv, q_gain.reshape(1, dh), k_gain.reshape(1, dh), sink_rows, bias)


def _route(logits):
    lane = lax.broadcasted_iota(jnp.int32, logits.shape, 1).astype(F32)
    big = jnp.float32(1e9)
    is_g = (lane >= N_EXPERTS) & (lane < N_EXPERTS + N_GROUPS)
    lg = jnp.where(is_g, logits, NEG_INF)
    gmax = jnp.max(lg, axis=-1, keepdims=True)
    gidx = jnp.min(jnp.where(is_g & (lg == gmax), lane, big), axis=-1, keepdims=True) - N_EXPERTS
    g_prob = 1.0 / jnp.sum(jnp.where(is_g, jnp.exp(lg - gmax), 0.0), axis=-1, keepdims=True)
    lo = gidx * EXPERTS_PER_GROUP
    in_grp = (lane >= lo) & (lane < lo + EXPERTS_PER_GROUP)
    le = jnp.where(in_grp, logits, NEG_INF)
    emax = jnp.max(le, axis=-1, keepdims=True)
    ex = jnp.where(in_grp, jnp.exp(le - emax), 0.0)
    prob = ex / jnp.sum(ex, axis=-1, keepdims=True)
    cand = jnp.where(in_grp, prob, -1.0)
    p1 = jnp.max(cand, axis=-1, keepdims=True)
    i1 = jnp.min(jnp.where(cand == p1, lane, big), axis=-1, keepdims=True)
    cand2 = jnp.where(lane == i1, -1.0, cand)
    p2 = jnp.max(cand2, axis=-1, keepdims=True)
    i2 = jnp.min(jnp.where(cand2 == p2, lane, big), axis=-1, keepdims=True)
    scale = g_prob / (p1 + p2)
    return i1, i2, p1 * scale, p2 * scale


RT_E1, RT_E2, RT_G1, RT_G2, RT_R1, RT_R2 = range(6)


def _pack_bf16_pairs(a):
    n = a.shape[1] // 2
    hi = lax.bitcast_convert_type(a[:, :n].astype(BF16).astype(F32), jnp.uint32)
    lo = lax.bitcast_convert_type(a[:, n:].astype(BF16).astype(F32), jnp.uint32)
    return hi | (lo >> 16)


def _unpack_bf16_pairs(p):
    hi = lax.bitcast_convert_type(p & jnp.uint32(0xFFFF0000), F32)
    lo = lax.bitcast_convert_type(p << 16, F32)
    return jnp.concatenate([hi, lo], axis=1)


LANES = 128
D_MODEL = 1024
TOKEN_ROWS = D_MODEL // 2 // LANES


def _store_token_major(ref, packed):
    m, w = packed.shape
    s_per = w // LANES
    for s in range(s_per):
        ref[pl.ds(s, m, stride=s_per), :] = packed[:, s * LANES:(s + 1) * LANES]


def _load_token_major(ref, m, s_per):
    return jnp.concatenate([ref[pl.ds(s, m, stride=s_per), :] for s in range(s_per)], axis=1)


def _router_kernel(x_ref, g_ref, wr_ref, br_ref, h_ref, route_ref, counts_ref, carry_ref):
    @pl.when(pl.program_id(0) == 0)
    def _():
        carry_ref[...] = jnp.zeros_like(carry_ref)

    hn = _rms(x_ref[...], g_ref[...])
    _store_token_major(h_ref, _pack_bf16_pairs(hn))
    h_hi, h_lo = _split_bf16(hn)
    w_hi, w_lo = _split_bf16(wr_ref[...])
    logits = (jnp.dot(h_hi, w_hi, preferred_element_type=F32)
              + jnp.dot(h_lo, w_hi, preferred_element_type=F32)
              + jnp.dot(h_hi, w_lo, preferred_element_type=F32)) + br_ref[...]
    i1, i2, g1, g2 = _route(logits)
    tm = logits.shape[0]
    lane = lax.broadcasted_iota(jnp.int32, logits.shape, 1).astype(F32)
    chosen = ((lane == i1) | (lane == i2)).astype(BF16)
    earlier = (lax.broadcasted_iota(jnp.int32, (tm, tm), 1)
               < lax.broadcasted_iota(jnp.int32, (tm, tm), 0)).astype(BF16)
    before = jnp.dot(earlier, chosen, preferred_element_type=F32) + carry_ref[...]
    r1 = jnp.sum(jnp.where(lane == i1, before, 0.0), axis=-1, keepdims=True)
    r2 = jnp.sum(jnp.where(lane == i2, before, 0.0), axis=-1, keepdims=True)
    carry_ref[...] += jnp.sum(chosen.astype(F32), axis=0, keepdims=True)
    rec = jnp.zeros_like(logits)
    for slot, val in ((RT_E1, i1), (RT_E2, i2), (RT_G1, g1), (RT_G2, g2), (RT_R1, r1), (RT_R2, r2)):
        rec = jnp.where(lane == slot, val, rec)
    route_ref[...] = rec
    counts_ref[...] = carry_ref[...]


def _router(x, gain, w_group, b_group, w_expert, b_expert, tm):
    T, D = x.shape
    pad = ROUTER_LANES - N_EXPERTS - N_GROUPS
    wr = jnp.concatenate([w_expert, w_group, jnp.zeros((D, pad), F32)], axis=1)
    br = jnp.concatenate([b_expert, b_group, jnp.zeros((pad,), F32)]).reshape(1, ROUTER_LANES)
    return pl.pallas_call(
        _router_kernel,
        grid=(T // tm,),
        in_specs=[pl.BlockSpec((tm, D), lambda i: (i, 0)),
                  pl.BlockSpec((1, D), lambda i: (0, 0)),
                  pl.BlockSpec((D, ROUTER_LANES), lambda i: (0, 0)),
                  pl.BlockSpec((1, ROUTER_LANES), lambda i: (0, 0))],
        out_specs=[pl.BlockSpec((tm * TOKEN_ROWS, LANES), lambda i: (i, 0)),
                   pl.BlockSpec((tm, ROUTER_LANES), lambda i: (i, 0)),
                   pl.BlockSpec((1, ROUTER_LANES), lambda i: (0, 0))],
        out_shape=[jax.ShapeDtypeStruct((T * TOKEN_ROWS, LANES), jnp.uint32),
                   jax.ShapeDtypeStruct((T, ROUTER_LANES), F32),
                   jax.ShapeDtypeStruct((1, ROUTER_LANES), F32)],
        scratch_shapes=[pltpu.VMEM((1, ROUTER_LANES), F32)],
        compiler_params=_cparams("arbitrary"),
        name="moe_router",
    )(x, gain.reshape(1, D), wr, br)


MOE_TILE = 256
MOE_SLOTS = 2
DMA_UNROLL = 8


def _token_copy(src, src_row, dst, dst_row, sem):
    return pltpu.make_async_copy(src.at[pl.ds(pl.multiple_of(src_row, TOKEN_ROWS), TOKEN_ROWS)],
                                 dst.at[pl.ds(pl.multiple_of(dst_row, TOKEN_ROWS), TOKEN_ROWS)], sem)


def _dispatch_kernel(pos_ref, h_ref, zeros_hbm, xs_hbm, sem):
    del zeros_hbm
    tm = h_ref.shape[0] // TOKEN_ROWS

    def start(g, c):
        r0 = pl.multiple_of(g * DMA_UNROLL, DMA_UNROLL)
        for j in range(DMA_UNROLL):
            for k in range(MOE_SLOTS):
                _token_copy(h_ref, (r0 + j) * TOKEN_ROWS, xs_hbm, pos_ref[MOE_SLOTS * (r0 + j) + k],
                            sem).start(priority=k)
        return c

    def wait(r, c):
        for k in range(MOE_SLOTS):
            _token_copy(h_ref, 0, xs_hbm, 0, sem).wait()
        return c

    lax.fori_loop(0, tm // DMA_UNROLL, start, 0)
    lax.fori_loop(0, tm, wait, 0, unroll=DMA_UNROLL)


def _dispatch(h, pos, n_rows, tm):
    T = h.shape[0] // TOKEN_ROWS
    zeros = jnp.zeros((n_rows * TOKEN_ROWS, LANES), h.dtype)
    return pl.pallas_call(
        _dispatch_kernel,
        grid=(T // tm,),
        in_specs=[pl.BlockSpec((MOE_SLOTS * tm,), lambda i: (i,), memory_space=pltpu.SMEM),
                  pl.BlockSpec((tm * TOKEN_ROWS, LANES), lambda i: (i, 0)),
                  pl.BlockSpec(memory_space=pl.ANY)],
        out_specs=pl.BlockSpec(memory_space=pl.ANY),
        out_shape=jax.ShapeDtypeStruct(zeros.shape, h.dtype),
        scratch_shapes=[pltpu.SemaphoreType.DMA(())],
        input_output_aliases={2: 0},
        compiler_params=_cparams("arbitrary"),
        name="moe_dispatch",
    )(pos, h, zeros)


def _expert_kernel(te_ref, nv_ref, xs_ref, wg_ref, wu_ref, wd_ref, ys_ref, wgu_s, wd_s):
    i = pl.program_id(0)
    nv = nv_ref[0]
    valid = i < nv
    ic = jnp.minimum(i, nv - 1)
    changed = (i == 0) | (te_ref[ic] != te_ref[jnp.maximum(ic - 1, 0)])

    @pl.when(valid & changed)
    def _():
        wgu_s[:, :D_EXPERT] = wg_ref[0, 0].astype(BF16)
        wgu_s[:, D_EXPERT:] = wu_ref[0, 0].astype(BF16)
        wd_s[...] = wd_ref[0, 0].astype(BF16)

    @pl.when(valid)
    def _():
        x = _unpack_bf16_pairs(_load_token_major(xs_ref, MOE_TILE, TOKEN_ROWS)).astype(BF16)
        gu = jnp.dot(x, wgu_s[...], preferred_element_type=F32)
        gate, up = gu[:, :D_EXPERT], gu[:, D_EXPERT:]
        hid = (gate * jax.nn.sigmoid(gate)) * up
        y = jnp.dot(hid.astype(BF16), wd_s[...], preferred_element_type=F32)
        _store_token_major(ys_ref, _pack_bf16_pairs(y))

    @pl.when(jnp.logical_not(valid))
    def _():
        ys_ref[...] = jnp.zeros_like(ys_ref)


def _experts(xs, tile_expert, n_valid, w_gate, w_up, w_down, layer):
    D = TOKEN_ROWS * LANES * 2
    blk = MOE_TILE * TOKEN_ROWS
    n_tiles = xs.shape[0] // blk
    row = lambda i, te, nv: (jnp.minimum(i, nv[0] - 1), 0)
    wsel = lambda i, te, nv: (layer, te[jnp.minimum(i, nv[0] - 1)], 0, 0)
    return pl.pallas_call(
        _expert_kernel,
        grid_spec=pltpu.PrefetchScalarGridSpec(
            num_scalar_prefetch=2,
            grid=(n_tiles,),
            in_specs=[pl.BlockSpec((blk, LANES), row),
                      pl.BlockSpec((1, 1, D, D_EXPERT), wsel),
                      pl.BlockSpec((1, 1, D, D_EXPERT), wsel),
                      pl.BlockSpec((1, 1, D_EXPERT, D), wsel)],
            out_specs=pl.BlockSpec((blk, LANES), lambda i, te, nv: (i, 0)),
            scratch_shapes=[pltpu.VMEM((D, 2 * D_EXPERT), BF16), pltpu.VMEM((D_EXPERT, D), BF16)]),
        out_shape=jax.ShapeDtypeStruct(xs.shape, jnp.uint32),
        compiler_params=_cparams("arbitrary"),
        name="moe_experts",
    )(tile_expert, n_valid, xs, w_gate, w_up, w_down)


def _combine_kernel(pos_ref, pos_next_ref, x_ref, route_ref, ys_hbm, o_ref, buf, sems):
    i = pl.program_id(0)
    tm = x_ref.shape[0]

    def gather(p_ref, ring):
        def start(g, c):
            r0 = pl.multiple_of(g * DMA_UNROLL, DMA_UNROLL)
            for j in range(DMA_UNROLL):
                for k in range(MOE_SLOTS):
                    _token_copy(ys_hbm, p_ref[MOE_SLOTS * (r0 + j) + k], buf.at[ring, k], (r0 + j) * TOKEN_ROWS,
                                sems.at[ring]).start(priority=k)
            return c
        lax.fori_loop(0, tm // DMA_UNROLL, start, 0)

    @pl.when(i == 0)
    def _():
        gather(pos_ref, 0)

    @pl.when(i + 1 < pl.num_programs(0))
    def _():
        gather(pos_next_ref, (i + 1) % 2)

    ring = i % 2

    def wait(r, c):
        for k in range(MOE_SLOTS):
            _token_copy(ys_hbm, 0, buf.at[ring, k], 0, sems.at[ring]).wait()
        return c

    lax.fori_loop(0, tm, wait, 0, unroll=DMA_UNROLL)
    rec = route_ref[...]
    y1 = _unpack_bf16_pairs(_load_token_major(buf.at[ring, 0], tm, TOKEN_ROWS))
    y2 = _unpack_bf16_pairs(_load_token_major(buf.at[ring, 1], tm, TOKEN_ROWS))
    o_ref[...] = x_ref[...] + rec[:, RT_G1:RT_G1 + 1] * y1 + rec[:, RT_G2:RT_G2 + 1] * y2


def _combine(x, route, pos, ys, tm):
    T, D = x.shape
    n = T // tm
    return pl.pallas_call(
        _combine_kernel,
        grid=(n,),
        in_specs=[pl.BlockSpec((MOE_SLOTS * tm,), lambda i: (i,), memory_space=pltpu.SMEM),
                  pl.BlockSpec((MOE_SLOTS * tm,), lambda i: (jnp.minimum(i + 1, n - 1),), memory_space=pltpu.SMEM),
                  pl.BlockSpec((tm, D), lambda i: (i, 0)),
                  pl.BlockSpec((tm, ROUTER_LANES), lambda i: (i, 0)),
                  pl.BlockSpec(memory_space=pl.ANY)],
        out_specs=pl.BlockSpec((tm, D), lambda i: (i, 0)),
        out_shape=jax.ShapeDtypeStruct((T, D), F32),
        scratch_shapes=[pltpu.VMEM((2, MOE_SLOTS, tm * TOKEN_ROWS, LANES), ys.dtype),
                        pltpu.SemaphoreType.DMA((2,))],
        compiler_params=_cparams("arbitrary"),
        name="moe_combine",
    )(pos, pos, x, route, ys)


def _moe(x, gain, w_group, b_group, w_expert, b_expert, w_gate, w_up, w_down, layer):
    T, D = x.shape
    n_tiles = (MOE_SLOTS * T + N_EXPERTS * (MOE_TILE - 1)) // MOE_TILE + 1
    h, route, counts = _router(x, gain, w_group, b_group, w_expert, b_expert, tm=512)
    counts = counts[0, :N_EXPERTS].astype(jnp.int32)
    padded = (counts + MOE_TILE - 1) // MOE_TILE * MOE_TILE
    ends = jnp.cumsum(padded)
    starts = ends - padded
    eid = route[:, RT_E1:RT_E2 + 1].astype(jnp.int32)
    rank = route[:, RT_R1:RT_R2 + 1].astype(jnp.int32)
    onehot = eid[..., None] == jnp.arange(N_EXPERTS, dtype=jnp.int32)
    pos = (rank + jnp.sum(jnp.where(onehot, starts, 0), axis=-1)).reshape(-1)
    pos = pos * TOKEN_ROWS
    tile_start = jnp.arange(n_tiles, dtype=jnp.int32) * MOE_TILE
    tile_expert = jnp.minimum(jnp.sum(tile_start[:, None] >= ends[None, :], axis=-1), N_EXPERTS - 1).astype(jnp.int32)
    n_valid = (ends[-1:] // MOE_TILE).astype(jnp.int32)
    xs = _dispatch(h, pos, n_tiles * MOE_TILE, tm=512)
    ys = _experts(xs, tile_expert, n_valid, w_gate, w_up, w_down, layer)
    return _combine(x, route, pos, ys, tm=256)


GDN_HALO = 16


def _gdn_conv_kernel(u_ref, up_ref, un_ref, w_ref, ab_ref, aexp_ref, dtb_ref, o_ref, gb_ref, *, tiles_per_seq):
    i = pl.program_id(0)
    j = pl.program_id(1)
    tm = u_ref.shape[0]
    first = (i % tiles_per_seq) == 0
    last = (i % tiles_per_seq) == tiles_per_seq - 1
    u = u_ref[...].astype(F32)
    prev = jnp.where(first, 0.0, up_ref[...].astype(F32))
    nxt = jnp.where(last, 0.0, un_ref[...].astype(F32))
    ext = jnp.concatenate([prev, u, nxt], axis=0)
    w = w_ref[...]
    h0 = GDN_HALO - B_CONV // 2
    y = w[0:1] * ext[h0:h0 + tm]
    for t in range(1, B_CONV):
        y = y + w[t:t + 1] * ext[h0 + t:h0 + t + tm]
    y = y * jax.nn.sigmoid(y)
    qscale = jnp.where(j == 0, B_HEAD_DIM ** -0.5, 1.0)
    for h in range(B_HEADS):
        slab = y[:, h * B_HEAD_DIM:(h + 1) * B_HEAD_DIM]
        inv = lax.rsqrt(jnp.sum(slab * slab, axis=-1, keepdims=True) + RMS_EPS) * qscale
        o_ref[0, :, h * B_HEAD_DIM:(h + 1) * B_HEAD_DIM] = slab * jnp.where(j < 2, inv, 1.0)

    @pl.when(j == 0)
    def _():
        ab = ab_ref[...]
        lane = lax.broadcasted_iota(jnp.int32, ab.shape, 1)
        z = ab + dtb_ref[...]
        softplus = jnp.maximum(z, 0.0) + jnp.log1p(jnp.exp(-jnp.abs(z)))
        gb_ref[...] = jnp.where(lane < 2 * B_HEADS, -aexp_ref[...] * softplus, jax.nn.sigmoid(ab))


def _gdn_conv(qkvz, ab, conv_w, a_log, dt_bias, seq, tm):
    T = qkvz.shape[0]
    D = B_HEADS * B_HEAD_DIM
    hb = tm // GDN_HALO
    n_halo = T // GDN_HALO
    pad = ROUTER_LANES - 2 * B_HEADS
    aexp = jnp.concatenate([jnp.exp(a_log.astype(F32)).reshape(-1), jnp.zeros((pad,), F32)]).reshape(1, -1)
    dtb = jnp.concatenate([dt_bias.astype(F32).reshape(-1), jnp.zeros((pad,), F32)]).reshape(1, -1)
    return pl.pallas_call(
        functools.partial(_gdn_conv_kernel, tiles_per_seq=seq // tm),
        grid=(T // tm, 3),
        in_specs=[pl.BlockSpec((tm, D), lambda i, j: (i, j)),
                  pl.BlockSpec((GDN_HALO, D), lambda i, j: (jnp.maximum(i * hb - 1, 0), j)),
                  pl.BlockSpec((GDN_HALO, D), lambda i, j: (jnp.minimum((i + 1) * hb, n_halo - 1), j)),
                  pl.BlockSpec((B_CONV, D), lambda i, j: (0, j)),
                  pl.BlockSpec((tm, ROUTER_LANES), lambda i, j: (i, 0)),
                  pl.BlockSpec((1, ROUTER_LANES), lambda i, j: (0, 0)),
                  pl.BlockSpec((1, ROUTER_LANES), lambda i, j: (0, 0))],
        out_specs=[pl.BlockSpec((1, tm, D), lambda i, j: (j, i, 0)),
                   pl.BlockSpec((tm, ROUTER_LANES), lambda i, j: (i, 0))],
        out_shape=[jax.ShapeDtypeStruct((3, T, D), F32), jax.ShapeDtypeStruct((T, ROUTER_LANES), F32)],
        compiler_params=_cparams("parallel", "arbitrary"),
        name="gdn_conv",
    )(qkvz, qkvz, qkvz, conv_w, ab, aexp, dtb)


def _gdn_gate_terms(gb, incl):
    C = gb.shape[0]
    lane = lax.broadcasted_iota(jnp.int32, gb.shape, 1)
    g_hi, g_lo = _split_bf16(jnp.where(lane < 2 * B_HEADS, gb, 0.0))
    tri = incl.astype(BF16)
    gc = jnp.dot(tri, g_hi, preferred_element_type=F32) + jnp.dot(tri, g_lo, preferred_element_type=F32)
    return gc, gc.T


def _gdn_scan_kernel(qf_ref, kf_ref, vf_ref, gf_ref, qb_ref, kb_ref, vb_ref, gbw_ref, of_ref, ob_ref, sf_ref, sb_ref):
    @pl.when(pl.program_id(1) == 0)
    def _():
        sf_ref[...] = jnp.zeros_like(sf_ref)
        sb_ref[...] = jnp.zeros_like(sb_ref)

    C = GDN_CHUNK
    n_chunks = gf_ref.shape[0] // C
    dk = B_HEAD_DIM
    row = lax.broadcasted_iota(jnp.int32, (C, C), 0)
    colm = lax.broadcasted_iota(jnp.int32, (C, C), 1)
    eye = (row == colm).astype(F32)
    incl = (row >= colm, row <= colm)
    strict = (row > colm, row < colm)
    qkv_refs = ((qf_ref, kf_ref, vf_ref), (qb_ref, kb_ref, vb_ref))
    s_refs = (sf_ref, sb_ref)
    o_refs = (of_ref, ob_ref)

    def chunk_step(c, carry):
        rows = (pl.ds(pl.multiple_of(c * C, C), C), pl.ds(pl.multiple_of((n_chunks - 1 - c) * C, C), C))
        _gdn_chunk_pair(rows, (gf_ref, gbw_ref), qkv_refs, s_refs, o_refs, incl, strict, eye)
        return carry

    lax.fori_loop(0, n_chunks, chunk_step, 0)


def _gdn_chunk_pair(rows, g_refs, qkv_refs, s_refs, o_refs, incl, strict, eye):
    C = GDN_CHUNK
    dk = B_HEAD_DIM
    gbs = tuple(g_refs[d][rows[d], :] for d in range(2))
    gates = [_gdn_gate_terms(gbs[d], incl[d]) for d in range(2)]
    glast = [gates[0][0][C - 1:C], gates[1][0][0:1]]
    units = [(d, h) for d in range(2) for h in range(B_HEADS)]

    def lane_of(d, h):
        return d * B_HEADS + h

    def cols(h):
        return slice(h * dk, (h + 1) * dk)

    v_b, kb_l, qd_bf, kd_bf, a_l, dec_l, egc_l = [], [], [], [], [], [], []
    for d, h in units:
        r = lane_of(d, h)
        gc, gct = gates[d]
        gcol = gc[:, r:r + 1]
        beta = gbs[d][:, 2 * B_HEADS + r:2 * B_HEADS + r + 1]
        q_ref, k_ref, v_ref = qkv_refs[d]
        qh, kh, vh = q_ref[0, rows[d], cols(h)], k_ref[0, rows[d], cols(h)], v_ref[0, rows[d], cols(h)]
        egc = jnp.exp(gcol)
        kb = kh * beta
        khb = kh.astype(BF16)
        a_l.append(_bdot_nt(jnp.concatenate([kb, qh], axis=0), khb))
        dec_l.append(jnp.exp(jnp.where(incl[d], gcol - gct[r:r + 1, :], NEG_INF)))
        v_b.append(vh * beta)
        kb_l.append(kb)
        egc_l.append(egc)
        qd_bf.append((qh * egc).astype(BF16))
        kd_bf.append((kh * jnp.exp(glast[d][:, r:r + 1] - gcol)).astype(BF16))
    x_l = [-jnp.where(strict[d], a[:C] * dec, 0.0) for (d, h), a, dec in zip(units, a_l, dec_l)]
    intra_bf = [(a[C:] * dec).astype(BF16) for a, dec in zip(a_l, dec_l)]
    p_l = [eye + x for x in x_l]
    n = 1
    while 2 * n < C:
        x_l = [_bdot(x, x) for x in x_l]
        n *= 2
        p_l = [p + _bdot(p, x) for p, x in zip(p_l, x_l)]
    sol_l = [_bdot(p, jnp.concatenate([vb, kb * egc], axis=1))
             for p, vb, kb, egc in zip(p_l, v_b, kb_l, egc_l)]
    st_l = [s_refs[d][h] for d, h in units]
    wq_l = [_bdot(jnp.concatenate([sol[:, dk:].astype(BF16), qd], axis=0), st)
            for sol, qd, st in zip(sol_l, qd_bf, st_l)]
    vn_l = [sol[:, :dk] - wq[:C] for sol, wq in zip(sol_l, wq_l)]
    for (d, h), wq, intra, vn in zip(units, wq_l, intra_bf, vn_l):
        o_refs[d][rows[d], cols(h)] = wq[C:] + _bdot(intra, vn)
    for (d, h), st, kd, vn in zip(units, st_l, kd_bf, vn_l):
        r = lane_of(d, h)
        s_refs[d][h] = st * jnp.exp(glast[d][:, r:r + 1]) + _bdot_tn(kd, vn)


def _gdn_scan(qkv, gb, batch, seq, rows_per_step):
    _, T, D = qkv.shape
    chunk = rows_per_step
    nc = seq // chunk
    fwd = lambda b, c: b * nc + c
    bwd = lambda b, c: b * nc + (nc - 1 - c)
    part = lambda p, f: pl.BlockSpec((1, chunk, D), lambda b, c: (p, f(b, c), 0))
    gspec = lambda f: pl.BlockSpec((chunk, ROUTER_LANES), lambda b, c: (f(b, c), 0))
    ospec = lambda f: pl.BlockSpec((chunk, D), lambda b, c: (f(b, c), 0))
    return pl.pallas_call(
        _gdn_scan_kernel,
        grid=(batch, nc),
        in_specs=[part(0, fwd), part(1, fwd), part(2, fwd), gspec(fwd),
                  part(0, bwd), part(1, bwd), part(2, bwd), gspec(bwd)],
        out_specs=[ospec(fwd), ospec(bwd)],
        out_shape=[jax.ShapeDtypeStruct((T, D), F32), jax.ShapeDtypeStruct((T, D), F32)],
        scratch_shapes=[pltpu.VMEM((B_HEADS, B_HEAD_DIM, B_HEAD_DIM), F32),
                        pltpu.VMEM((B_HEADS, B_HEAD_DIM, B_HEAD_DIM), F32)],
        compiler_params=_cparams("parallel", "arbitrary"),
        name="gdn_scan",
    )(qkv, qkv, qkv, gb, qkv, qkv, qkv, gb)


def _gdn_out_kernel(of_ref, ob_ref, z_ref, og_ref, w_ref, x_ref, o_ref, a_ref):
    @pl.when(pl.program_id(1) == 0)
    def _():
        o = of_ref[...] + ob_ref[...]
        z = z_ref[...].astype(F32)
        for h in range(B_HEADS):
            sl = slice(h * B_HEAD_DIM, (h + 1) * B_HEAD_DIM)
            zh = z[:, sl]
            a_ref[:, sl] = (_rms(o[:, sl], og_ref[...]) * (zh * jax.nn.sigmoid(zh))).astype(BF16)

    o_ref[...] = x_ref[...] + jnp.dot(a_ref[...], w_ref[...], preferred_element_type=F32)


def _gdn_out(o_f, o_b, qkvz, o_gain, w, x, tm, tn):
    T, D = x.shape
    return pl.pallas_call(
        _gdn_out_kernel,
        grid=(T // tm, D // tn),
        in_specs=[pl.BlockSpec((tm, D), lambda i, j: (i, 0)),
                  pl.BlockSpec((tm, D), lambda i, j: (i, 0)),
                  pl.BlockSpec((tm, D), lambda i, j: (i, 3)),
                  pl.BlockSpec((1, B_HEAD_DIM), lambda i, j: (0, 0)),
                  pl.BlockSpec((D, tn), lambda i, j: (0, j)),
                  pl.BlockSpec((tm, tn), lambda i, j: (i, j))],
        out_specs=pl.BlockSpec((tm, tn), lambda i, j: (i, j)),
        out_shape=jax.ShapeDtypeStruct((T, D), F32),
        scratch_shapes=[pltpu.VMEM((tm, D), BF16)],
        compiler_params=_cparams("parallel", "arbitrary"),
        name="gdn_out",
    )(o_f, o_b, qkvz, o_gain.reshape(1, B_HEAD_DIM), w, x)


def _attention_layer(x, gain, w_in, q_gain, k_gain, sink, w_out, batch, seq):
    qkv = _norm_matmul(x, gain, w_in.astype(BF16), tm=1024, tn=768)
    a = _attention(qkv, q_gain, k_gain, sink, batch, seq)
    return _matmul_residual(a, w_out.astype(BF16), x, tm=1024, tn=1024)


def _gdn_layer(x, gain, w_in, conv_w, a_log, dt_bias, o_gain, w_out, batch, seq):
    D = x.shape[1]
    w_main = w_in[:, :4 * D].astype(BF16)
    pad = ROUTER_LANES - 4 * B_HEADS
    w_ab = jnp.concatenate([w_in[:, 4 * D:], jnp.zeros((D, pad), F32)], axis=1)
    qkvz, ab = _norm_matmul2(x, gain, w_main, w_ab, tm=1024, tn=1024)
    qkv, gb = _gdn_conv(qkvz, ab, conv_w, a_log, dt_bias, seq, tm=512)
    o_f, o_b = _gdn_scan(qkv, gb, batch, seq, 4 * GDN_CHUNK)
    return _gdn_out(o_f, o_b, qkvz, o_gain, w_out.astype(BF16), x, tm=512, tn=1024)


def kernel(x, norm_mix, norm_ffn, attn_w_in, attn_q_gain, attn_k_gain, attn_sink, attn_w_out, gdn_w_in, gdn_conv, gdn_a_log, gdn_dt_bias, gdn_o_gain, gdn_w_out, moe_w_group, moe_b_group, moe_w_expert, moe_b_expert, moe_w_gate, moe_w_up, moe_w_down):
    batch, seq, d_model = x.shape
    depth = norm_mix.shape[0]
    xt = x.reshape(batch * seq, d_model)
    for i in range(depth):
        j = i // 2
        if i % 2 == 0:
            xt = _attention_layer(xt, norm_mix[i], attn_w_in[j], attn_q_gain[j], attn_k_gain[j],
                                  attn_sink[j], attn_w_out[j], batch, seq)
        else:
            xt = _gdn_layer(xt, norm_mix[i], gdn_w_in[j], gdn_conv[j], gdn_a_log[j], gdn_dt_bias[j],
                            gdn_o_gain[j], gdn_w_out[j], batch, seq)
        xt = _moe(xt, norm_ffn[i], moe_w_group[i], moe_b_group[i], moe_w_expert[i], moe_b_expert[i],
                  moe_w_gate, moe_w_up, moe_w_down, layer=i)
    return xt.reshape(batch, seq, d_model)
```

```python
import functools
import math

import jax
import jax.numpy as jnp
import numpy as np
from jax import lax
from jax.experimental import pallas as pl
from jax.experimental.pallas import tpu as pltpu

RMS_EPS = 1e-6
NEG_INF = -1e30
F32 = jnp.float32
BF16 = jnp.bfloat16

A_HEADS = 16
A_KV_HEADS = 4
A_HEAD_DIM = 64
A_REP = A_HEADS // A_KV_HEADS
A_BLOCK = 128
B_HEADS = 8
B_HEAD_DIM = 128
B_CONV = 4
GDN_CHUNK = 64
N_GROUPS = 4
EXPERTS_PER_GROUP = 8
N_EXPERTS = 32
D_EXPERT = 256
ROUTER_LANES = 128

V7X_VMEM_LIMIT_BYTES = 56 * 1024 * 1024


def _cparams(*sem):
    return pltpu.CompilerParams(dimension_semantics=sem, vmem_limit_bytes=V7X_VMEM_LIMIT_BYTES)


def _bdot(a, b):
    return jnp.dot(a.astype(BF16), b.astype(BF16), preferred_element_type=F32)


def _bdot_nt(a, b):
    return lax.dot_general(a.astype(BF16), b.astype(BF16), (((1,), (1,)), ((), ())),
                           preferred_element_type=F32)


def _bdot_tn(a, b):
    return lax.dot_general(a.astype(BF16), b.astype(BF16), (((0,), (0,)), ((), ())),
                           preferred_element_type=F32)


def _split_bf16(a):
    hi = a.astype(BF16)
    lo = (a - hi.astype(F32)).astype(BF16)
    return hi, lo


def _rms(x, gain):
    return x * lax.rsqrt(jnp.mean(x * x, axis=-1, keepdims=True) + RMS_EPS) * gain


def _norm_matmul_kernel(x_ref, g_ref, w_ref, o_ref, xn_ref):
    @pl.when(pl.program_id(1) == 0)
    def _():
        xn_ref[...] = _rms(x_ref[...], g_ref[...]).astype(BF16)

    o_ref[...] = jnp.dot(xn_ref[...], w_ref[...], preferred_element_type=F32)


def _norm_matmul(x, gain, w, tm, tn):
    T, D = x.shape
    N = w.shape[1]
    return pl.pallas_call(
        _norm_matmul_kernel,
        grid=(T // tm, N // tn),
        in_specs=[pl.BlockSpec((tm, D), lambda i, j: (i, 0)),
                  pl.BlockSpec((1, D), lambda i, j: (0, 0)),
                  pl.BlockSpec((D, tn), lambda i, j: (0, j))],
        out_specs=pl.BlockSpec((tm, tn), lambda i, j: (i, j)),
        out_shape=jax.ShapeDtypeStruct((T, N), F32),
        scratch_shapes=[pltpu.VMEM((tm, D), BF16)],
        compiler_params=_cparams("parallel", "arbitrary"),
        name="norm_matmul",
    )(x, gain.reshape(1, D), w)


def _norm_matmul2_kernel(x_ref, g_ref, w_ref, w2_ref, o_ref, o2_ref, xn_ref):
    @pl.when(pl.program_id(1) == 0)
    def _():
        xn = _rms(x_ref[...], g_ref[...])
        xn_ref[...] = xn.astype(BF16)
        x_hi, x_lo = _split_bf16(xn)
        w_hi, w_lo = _split_bf16(w2_ref[...])
        o2_ref[...] = (jnp.dot(x_hi, w_hi, preferred_element_type=F32)
                       + jnp.dot(x_lo, w_hi, preferred_element_type=F32)
                       + jnp.dot(x_hi, w_lo, preferred_element_type=F32))

    o_ref[...] = jnp.dot(xn_ref[...], w_ref[...], preferred_element_type=F32).astype(o_ref.dtype)


def _norm_matmul2(x, gain, w, w2, tm, tn):
    T, D = x.shape
    N = w.shape[1]
    N2 = w2.shape[1]
    return pl.pallas_call(
        _norm_matmul2_kernel,
        grid=(T // tm, N // tn),
        in_specs=[pl.BlockSpec((tm, D), lambda i, j: (i, 0)),
                  pl.BlockSpec((1, D), lambda i, j: (0, 0)),
                  pl.BlockSpec((D, tn), lambda i, j: (0, j)),
                  pl.BlockSpec((D, N2), lambda i, j: (0, 0))],
        out_specs=[pl.BlockSpec((tm, tn), lambda i, j: (i, j)),
                   pl.BlockSpec((tm, N2), lambda i, j: (i, 0))],
        out_shape=[jax.ShapeDtypeStruct((T, N), BF16), jax.ShapeDtypeStruct((T, N2), F32)],
        scratch_shapes=[pltpu.VMEM((tm, D), BF16)],
        compiler_params=_cparams("parallel", "arbitrary"),
        name="norm_matmul2",
    )(x, gain.reshape(1, D), w, w2)


def _matmul_residual_kernel(a_ref, w_ref, x_ref, o_ref):
    o_ref[...] = x_ref[...] + jnp.dot(a_ref[...], w_ref[...], preferred_element_type=F32)


def _matmul_residual(a, w, x, tm, tn):
    T, K = a.shape
    N = w.shape[1]
    return pl.pallas_call(
        _matmul_residual_kernel,
        grid=(T // tm, N // tn),
        in_specs=[pl.BlockSpec((tm, K), lambda i, j: (i, 0)),
                  pl.BlockSpec((K, tn), lambda i, j: (0, j)),
                  pl.BlockSpec((tm, tn), lambda i, j: (i, j))],
        out_specs=pl.BlockSpec((tm, tn), lambda i, j: (i, j)),
        out_shape=jax.ShapeDtypeStruct((T, N), F32),
        compiler_params=_cparams("parallel", "arbitrary"),
        name="matmul_residual",
    )(a, w, x)


def _attn_kernel(main_ref, prev_ref, next_ref, qg_ref, kg_ref, sink_ref, bias_ref, o_ref):
    n = pl.program_id(1)
    nb = pl.num_programs(1)
    dh, blk = A_HEAD_DIM, A_BLOCK
    kv_cols = A_KV_HEADS * dh
    main = main_ref[...]
    kv = jnp.concatenate([prev_ref[...], main[:, A_HEADS * dh:], next_ref[...]], axis=0)
    col = lax.broadcasted_iota(jnp.int32, (1, 3 * blk), 1)
    outside = ((col < blk) & (n == 0)) | ((col >= 2 * blk) & (n == nb - 1))
    edge = jnp.where(outside, NEG_INF, 0.0)
    qg = qg_ref[...] * (dh ** -0.5)
    groups = range(A_KV_HEADS)
    kn = [_rms(kv[:, g * dh:(g + 1) * dh], kg_ref[...]).astype(BF16) for g in groups]
    q4 = [jnp.concatenate([_rms(main[:, (A_REP * g + r) * dh:(A_REP * g + r + 1) * dh], qg).astype(BF16)
                           for r in range(A_REP)], axis=0) for g in groups]
    s = [_bdot_nt(q4[g], kn[g]) + bias_ref[g] + edge for g in groups]
    m = [jnp.maximum(jnp.max(s[g], axis=-1, keepdims=True), sink_ref[g]) for g in groups]
    p = [jnp.exp(s[g] - m[g]) for g in groups]
    denom = [jnp.sum(p[g], axis=-1, keepdims=True) + jnp.exp(sink_ref[g] - m[g]) for g in groups]
    o = [jnp.dot(p[g].astype(BF16), kv[:, kv_cols + g * dh: kv_cols + (g + 1) * dh].astype(BF16),
                 preferred_element_type=F32) / denom[g] for g in groups]
    for g in groups:
        for r in range(A_REP):
            h = A_REP * g + r
            o_ref[:, h * dh:(h + 1) * dh] = o[g][r * blk:(r + 1) * blk].astype(o_ref.dtype)


def _attn_tables(sink):
    blk = A_BLOCK
    slopes = np.array([2.0 ** (-8.0 * (h + 1) / A_HEADS) for h in range(A_HEADS)], np.float32)
    qi = np.arange(blk)
    kj = np.arange(3 * blk)
    dist = np.abs(blk + qi[:, None] - kj[None, :]).astype(np.float32)
    bias = np.where(dist[None] <= blk, -slopes[:, None, None] * dist[None], np.float32(NEG_INF))
    bias = bias.astype(np.float32).reshape(A_KV_HEADS, A_REP * blk, 3 * blk)
    sink_rows = jnp.repeat(sink.astype(F32).reshape(A_KV_HEADS, A_REP), blk, axis=1)[..., None]
    return jnp.asarray(bias), sink_rows


def _attention(qkv, q_gain, k_gain, sink, batch, seq):
    T, W = qkv.shape
    blk, dh = A_BLOCK, A_HEAD_DIM
    nb = seq // blk
    kv_w = 2 * A_KV_HEADS * dh
    kv_blk = (A_HEADS * dh) // kv_w
    bias, sink_rows = _attn_tables(sink)
    return pl.pallas_call(
        _attn_kernel,
        grid=(batch, nb),
        in_specs=[
            pl.BlockSpec((blk, W), lambda b, n: (b * nb + n, 0)),
            pl.BlockSpec((blk, kv_w), lambda b, n: (b * nb + jnp.maximum(n - 1, 0), kv_blk)),
            pl.BlockSpec((blk, kv_w), lambda b, n: (b * nb + jnp.minimum(n + 1, nb - 1), kv_blk)),
            pl.BlockSpec((1, dh), lambda b, n: (0, 0)),
            pl.BlockSpec((1, dh), lambda b, n: (0, 0)),
            pl.BlockSpec((A_KV_HEADS, A_REP * blk, 1), lambda b, n: (0, 0, 0)),
            pl.BlockSpec((A_KV_HEADS, A_REP * blk, 3 * blk), lambda b, n: (0, 0, 0)),
        ],
        out_specs=pl.BlockSpec((blk, A_HEADS * dh), lambda b, n: (b * nb + n, 0)),
        out_shape=jax.ShapeDtypeStruct((T, A_HEADS * dh), BF16),
        compiler_params=_cparams("parallel", "parallel"),
        name="window_attention",
    )(qkv, qkv, qkv, q_gain.reshape(1, dh), k_gain.reshape(1, dh), sink_rows, bias)


def _route(logits):
    lane = lax.broadcasted_iota(jnp.int32, logits.shape, 1).astype(F32)
    big = jnp.float32(1e9)
    is_g = (lane >= N_EXPERTS) & (lane < N_EXPERTS + N_GROUPS)
    lg = jnp.where(is_g, logits, NEG_INF)
    gmax = jnp.max(lg, axis=-1, keepdims=True)
    gidx = jnp.min(jnp.where(is_g & (lg == gmax), lane, big), axis=-1, keepdims=True) - N_EXPERTS
    g_prob = 1.0 / jnp.sum(jnp.where(is_g, jnp.exp(lg - gmax), 0.0), axis=-1, keepdims=True)
    lo = gidx * EXPERTS_PER_GROUP
    in_grp = (lane >= lo) & (lane < lo + EXPERTS_PER_GROUP)
    le = jnp.where(in_grp, logits, NEG_INF)
    emax = jnp.max(le, axis=-1, keepdims=True)
    ex = jnp.where(in_grp, jnp.exp(le - emax), 0.0)
    prob = ex / jnp.sum(ex, axis=-1, keepdims=True)
    cand = jnp.where(in_grp, prob, -1.0)
    p1 = jnp.max(cand, axis=-1, keepdims=True)
    i1 = jnp.min(jnp.where(cand == p1, lane, big), axis=-1, keepdims=True)
    cand2 = jnp.where(lane == i1, -1.0, cand)
    p2 = jnp.max(cand2, axis=-1, keepdims=True)
    i2 = jnp.min(jnp.where(cand2 == p2, lane, big), axis=-1, keepdims=True)
    scale = g_prob / (p1 + p2)
    return i1, i2, p1 * scale, p2 * scale


RT_E1, RT_E2, RT_G1, RT_G2, RT_R1, RT_R2 = range(6)


def _pack_bf16_pairs(a):
    n = a.shape[1] // 2
    hi = lax.bitcast_convert_type(a[:, :n].astype(BF16).astype(F32), jnp.uint32)
    lo = lax.bitcast_convert_type(a[:, n:].astype(BF16).astype(F32), jnp.uint32)
    return hi | (lo >> 16)


def _unpack_bf16_pairs(p):
    hi = lax.bitcast_convert_type(p & jnp.uint32(0xFFFF0000), F32)
    lo = lax.bitcast_convert_type(p << 16, F32)
    return jnp.concatenate([hi, lo], axis=1)


LANES = 128
D_MODEL = 1024
TOKEN_ROWS = D_MODEL // 2 // LANES


def _store_token_major(ref, packed):
    m, w = packed.shape
    s_per = w // LANES
    for s in range(s_per):
        ref[pl.ds(s, m, stride=s_per), :] = packed[:, s * LANES:(s + 1) * LANES]


def _load_token_major(ref, m, s_per):
    return jnp.concatenate([ref[pl.ds(s, m, stride=s_per), :] for s in range(s_per)], axis=1)


def _router_kernel(x_ref, g_ref, wr_ref, br_ref, h_ref, route_ref, counts_ref, carry_ref):
    @pl.when(pl.program_id(0) == 0)
    def _():
        carry_ref[...] = jnp.zeros_like(carry_ref)

    hn = _rms(x_ref[...], g_ref[...])
    _store_token_major(h_ref, _pack_bf16_pairs(hn))
    h_hi, h_lo = _split_bf16(hn)
    w_hi, w_lo = _split_bf16(wr_ref[...])
    logits = (jnp.dot(h_hi, w_hi, preferred_element_type=F32)
              + jnp.dot(h_lo, w_hi, preferred_element_type=F32)
              + jnp.dot(h_hi, w_lo, preferred_element_type=F32)) + br_ref[...]
    i1, i2, g1, g2 = _route(logits)
    tm = logits.shape[0]
    lane = lax.broadcasted_iota(jnp.int32, logits.shape, 1).astype(F32)
    chosen = ((lane == i1) | (lane == i2)).astype(BF16)
    earlier = (lax.broadcasted_iota(jnp.int32, (tm, tm), 1)
               < lax.broadcasted_iota(jnp.int32, (tm, tm), 0)).astype(BF16)
    before = jnp.dot(earlier, chosen, preferred_element_type=F32) + carry_ref[...]
    r1 = jnp.sum(jnp.where(lane == i1, before, 0.0), axis=-1, keepdims=True)
    r2 = jnp.sum(jnp.where(lane == i2, before, 0.0), axis=-1, keepdims=True)
    carry_ref[...] += jnp.sum(chosen.astype(F32), axis=0, keepdims=True)
    rec = jnp.zeros_like(logits)
    for slot, val in ((RT_E1, i1), (RT_E2, i2), (RT_G1, g1), (RT_G2, g2), (RT_R1, r1), (RT_R2, r2)):
        rec = jnp.where(lane == slot, val, rec)
    route_ref[...] = rec
    counts_ref[...] = carry_ref[...]


def _router(x, gain, w_group, b_group, w_expert, b_expert, tm):
    T, D = x.shape
    pad = ROUTER_LANES - N_EXPERTS - N_GROUPS
    wr = jnp.concatenate([w_expert, w_group, jnp.zeros((D, pad), F32)], axis=1)
    br = jnp.concatenate([b_expert, b_group, jnp.zeros((pad,), F32)]).reshape(1, ROUTER_LANES)
    return pl.pallas_call(
        _router_kernel,
        grid=(T // tm,),
        in_specs=[pl.BlockSpec((tm, D), lambda i: (i, 0)),
                  pl.BlockSpec((1, D), lambda i: (0, 0)),
                  pl.BlockSpec((D, ROUTER_LANES), lambda i: (0, 0)),
                  pl.BlockSpec((1, ROUTER_LANES), lambda i: (0, 0))],
        out_specs=[pl.BlockSpec((tm * TOKEN_ROWS, LANES), lambda i: (i, 0)),
                   pl.BlockSpec((tm, ROUTER_LANES), lambda i: (i, 0)),
                   pl.BlockSpec((1, ROUTER_LANES), lambda i: (0, 0))],
        out_shape=[jax.ShapeDtypeStruct((T * TOKEN_ROWS, LANES), jnp.uint32),
                   jax.ShapeDtypeStruct((T, ROUTER_LANES), F32),
                   jax.ShapeDtypeStruct((1, ROUTER_LANES), F32)],
        scratch_shapes=[pltpu.VMEM((1, ROUTER_LANES), F32)],
        compiler_params=_cparams("arbitrary"),
        name="moe_router",
    )(x, gain.reshape(1, D), wr, br)


MOE_TILE = 512
MOE_SLOTS = 2
DMA_UNROLL = 8


def _token_copy(src, src_row, dst, dst_row, sem):
    return pltpu.make_async_copy(src.at[pl.ds(pl.multiple_of(src_row, TOKEN_ROWS), TOKEN_ROWS)],
                                 dst.at[pl.ds(pl.multiple_of(dst_row, TOKEN_ROWS), TOKEN_ROWS)], sem)


def _dispatch_kernel(pos_ref, h_ref, zeros_hbm, xs_hbm, sem):
    del zeros_hbm
    tm = h_ref.shape[0] // TOKEN_ROWS

    def start(g, c):
        r0 = pl.multiple_of(g * DMA_UNROLL, DMA_UNROLL)
        for j in range(DMA_UNROLL):
            for k in range(MOE_SLOTS):
                _token_copy(h_ref, (r0 + j) * TOKEN_ROWS, xs_hbm, pos_ref[MOE_SLOTS * (r0 + j) + k],
                            sem).start(priority=k)
        return c

    def wait(r, c):
        for k in range(MOE_SLOTS):
            _token_copy(h_ref, 0, xs_hbm, 0, sem).wait()
        return c

    lax.fori_loop(0, tm // DMA_UNROLL, start, 0)
    lax.fori_loop(0, tm, wait, 0, unroll=DMA_UNROLL)


def _dispatch(h, pos, n_rows, tm):
    T = h.shape[0] // TOKEN_ROWS
    zeros = jnp.zeros((n_rows * TOKEN_ROWS, LANES), h.dtype)
    return pl.pallas_call(
        _dispatch_kernel,
        grid=(T // tm,),
        in_specs=[pl.BlockSpec((MOE_SLOTS * tm,), lambda i: (i,), memory_space=pltpu.SMEM),
                  pl.BlockSpec((tm * TOKEN_ROWS, LANES), lambda i: (i, 0)),
                  pl.BlockSpec(memory_space=pl.ANY)],
        out_specs=pl.BlockSpec(memory_space=pl.ANY),
        out_shape=jax.ShapeDtypeStruct(zeros.shape, h.dtype),
        scratch_shapes=[pltpu.SemaphoreType.DMA(())],
        input_output_aliases={2: 0},
        compiler_params=_cparams("arbitrary"),
        name="moe_dispatch",
    )(pos, h, zeros)


def _expert_kernel(te_ref, nv_ref, xs_ref, wg_ref, wu_ref, wd_ref, ys_ref, wgu_s, wd_s):
    i = pl.program_id(0)
    nv = nv_ref[0]
    valid = i < nv
    ic = jnp.minimum(i, nv - 1)
    changed = (i == 0) | (te_ref[ic] != te_ref[jnp.maximum(ic - 1, 0)])

    @pl.when(valid & changed)
    def _():
        wgu_s[:, :D_EXPERT] = wg_ref[0, 0].astype(BF16)
        wgu_s[:, D_EXPERT:] = wu_ref[0, 0].astype(BF16)
        wd_s[...] = wd_ref[0, 0].astype(BF16)

    @pl.when(valid)
    def _():
        x = _unpack_bf16_pairs(_load_token_major(xs_ref, MOE_TILE, TOKEN_ROWS)).astype(BF16)
        gu = jnp.dot(x, wgu_s[...], preferred_element_type=F32)
        gate, up = gu[:, :D_EXPERT], gu[:, D_EXPERT:]
        hid = (gate * jax.nn.sigmoid(gate)) * up
        y = jnp.dot(hid.astype(BF16), wd_s[...], preferred_element_type=F32)
        _store_token_major(ys_ref, _pack_bf16_pairs(y))

    @pl.when(jnp.logical_not(valid))
    def _():
        ys_ref[...] = jnp.zeros_like(ys_ref)


def _experts(xs, tile_expert, n_valid, w_gate, w_up, w_down, layer):
    D = TOKEN_ROWS * LANES * 2
    blk = MOE_TILE * TOKEN_ROWS
    n_tiles = xs.shape[0] // blk
    row = lambda i, te, nv: (jnp.minimum(i, nv[0] - 1), 0)
    wsel = lambda i, te, nv: (layer, te[jnp.minimum(i, nv[0] - 1)], 0, 0)
    return pl.pallas_call(
        _expert_kernel,
        grid_spec=pltpu.PrefetchScalarGridSpec(
            num_scalar_prefetch=2,
            grid=(n_tiles,),
            in_specs=[pl.BlockSpec((blk, LANES), row),
                      pl.BlockSpec((1, 1, D, D_EXPERT), wsel),
                      pl.BlockSpec((1, 1, D, D_EXPERT), wsel),
                      pl.BlockSpec((1, 1, D_EXPERT, D), wsel)],
            out_specs=pl.BlockSpec((blk, LANES), lambda i, te, nv: (i, 0)),
            scratch_shapes=[pltpu.VMEM((D, 2 * D_EXPERT), BF16), pltpu.VMEM((D_EXPERT, D), BF16)]),
        out_shape=jax.ShapeDtypeStruct(xs.shape, jnp.uint32),
        compiler_params=_cparams("arbitrary"),
        name="moe_experts",
    )(tile_expert, n_valid, xs, w_gate, w_up, w_down)


def _combine_kernel(pos_ref, pos_next_ref, x_ref, route_ref, ys_hbm, o_ref, buf, sems):
    i = pl.program_id(0)
    tm = x_ref.shape[0]

    def gather(p_ref, ring):
        def start(g, c):
            r0 = pl.multiple_of(g * DMA_UNROLL, DMA_UNROLL)
            for j in range(DMA_UNROLL):
                for k in range(MOE_SLOTS):
                    _token_copy(ys_hbm, p_ref[MOE_SLOTS * (r0 + j) + k], buf.at[ring, k], (r0 + j) * TOKEN_ROWS,
                                sems.at[ring]).start(priority=k)
            return c
        lax.fori_loop(0, tm // DMA_UNROLL, start, 0)

    @pl.when(i == 0)
    def _():
        gather(pos_ref, 0)

    @pl.when(i + 1 < pl.num_programs(0))
    def _():
        gather(pos_next_ref, (i + 1) % 2)

    ring = i % 2

    def wait(r, c):
        for k in range(MOE_SLOTS):
            _token_copy(ys_hbm, 0, buf.at[ring, k], 0, sems.at[ring]).wait()
        return c

    lax.fori_loop(0, tm, wait, 0, unroll=DMA_UNROLL)
    rec = route_ref[...]
    y1 = _unpack_bf16_pairs(_load_token_major(buf.at[ring, 0], tm, TOKEN_ROWS))
    y2 = _unpack_bf16_pairs(_load_token_major(buf.at[ring, 1], tm, TOKEN_ROWS))
    o_ref[...] = x_ref[...] + rec[:, RT_G1:RT_G1 + 1] * y1 + rec[:, RT_G2:RT_G2 + 1] * y2


def _combine(x, route, pos, ys, tm):
    T, D = x.shape
    n = T // tm
    return pl.pallas_call(
        _combine_kernel,
        grid=(n,),
        in_specs=[pl.BlockSpec((MOE_SLOTS * tm,), lambda i: (i,), memory_space=pltpu.SMEM),
                  pl.BlockSpec((MOE_SLOTS * tm,), lambda i: (jnp.minimum(i + 1, n - 1),), memory_space=pltpu.SMEM),
                  pl.BlockSpec((tm, D), lambda i: (i, 0)),
                  pl.BlockSpec((tm, ROUTER_LANES), lambda i: (i, 0)),
                  pl.BlockSpec(memory_space=pl.ANY)],
        out_specs=pl.BlockSpec((tm, D), lambda i: (i, 0)),
        out_shape=jax.ShapeDtypeStruct((T, D), F32),
        scratch_shapes=[pltpu.VMEM((2, MOE_SLOTS, tm * TOKEN_ROWS, LANES), ys.dtype),
                        pltpu.SemaphoreType.DMA((2,))],
        compiler_params=_cparams("arbitrary"),
        name="moe_combine",
    )(pos, pos, x, route, ys)


def _moe(x, gain, w_group, b_group, w_expert, b_expert, w_gate, w_up, w_down, layer):
    T, D = x.shape
    n_tiles = (MOE_SLOTS * T + N_EXPERTS * (MOE_TILE - 1)) // MOE_TILE + 1
    h, route, counts = _router(x, gain, w_group, b_group, w_expert, b_expert, tm=512)
    counts = counts[0, :N_EXPERTS].astype(jnp.int32)
    padded = (counts + MOE_TILE - 1) // MOE_TILE * MOE_TILE
    ends = jnp.cumsum(padded)
    starts = ends - padded
    eid = route[:, RT_E1:RT_E2 + 1].astype(jnp.int32)
    rank = route[:, RT_R1:RT_R2 + 1].astype(jnp.int32)
    onehot = eid[..., None] == jnp.arange(N_EXPERTS, dtype=jnp.int32)
    pos = (rank + jnp.sum(jnp.where(onehot, starts, 0), axis=-1)).reshape(-1)
    pos = pos * TOKEN_ROWS
    tile_start = jnp.arange(n_tiles, dtype=jnp.int32) * MOE_TILE
    tile_expert = jnp.minimum(jnp.sum(tile_start[:, None] >= ends[None, :], axis=-1), N_EXPERTS - 1).astype(jnp.int32)
    n_valid = (ends[-1:] // MOE_TILE).astype(jnp.int32)
    xs = _dispatch(h, pos, n_tiles * MOE_TILE, tm=1024)
    ys = _experts(xs, tile_expert, n_valid, w_gate, w_up, w_down, layer)
    return _combine(x, route, pos, ys, tm=512)


GDN_HALO = 16


def _gdn_conv_kernel(u_ref, up_ref, un_ref, w_ref, ab_ref, aexp_ref, dtb_ref, o_ref, gb_ref, *, tiles_per_seq):
    i = pl.program_id(0)
    j = pl.program_id(1)
    tm = u_ref.shape[0]
    first = (i % tiles_per_seq) == 0
    last = (i % tiles_per_seq) == tiles_per_seq - 1
    u = u_ref[...].astype(F32)
    prev = jnp.where(first, 0.0, up_ref[...].astype(F32))
    nxt = jnp.where(last, 0.0, un_ref[...].astype(F32))
    ext = jnp.concatenate([prev, u, nxt], axis=0)
    w = w_ref[...]
    h0 = GDN_HALO - B_CONV // 2
    y = w[0:1] * ext[h0:h0 + tm]
    for t in range(1, B_CONV):
        y = y + w[t:t + 1] * ext[h0 + t:h0 + t + tm]
    y = y * jax.nn.sigmoid(y)
    qscale = jnp.where(j == 0, B_HEAD_DIM ** -0.5, 1.0)
    for h in range(B_HEADS):
        slab = y[:, h * B_HEAD_DIM:(h + 1) * B_HEAD_DIM]
        inv = lax.rsqrt(jnp.sum(slab * slab, axis=-1, keepdims=True) + RMS_EPS) * qscale
        o_ref[0, :, h * B_HEAD_DIM:(h + 1) * B_HEAD_DIM] = slab * jnp.where(j < 2, inv, 1.0)

    @pl.when(j == 0)
    def _():
        ab = ab_ref[...]
        lane = lax.broadcasted_iota(jnp.int32, ab.shape, 1)
        z = ab + dtb_ref[...]
        softplus = jnp.maximum(z, 0.0) + jnp.log1p(jnp.exp(-jnp.abs(z)))
        gb_ref[...] = jnp.where(lane < 2 * B_HEADS, -aexp_ref[...] * softplus, jax.nn.sigmoid(ab))


def _gdn_conv(qkvz, ab, conv_w, a_log, dt_bias, seq, tm):
    T = qkvz.shape[0]
    D = B_HEADS * B_HEAD_DIM
    hb = tm // GDN_HALO
    n_halo = T // GDN_HALO
    pad = ROUTER_LANES - 2 * B_HEADS
    aexp = jnp.concatenate([jnp.exp(a_log.astype(F32)).reshape(-1), jnp.zeros((pad,), F32)]).reshape(1, -1)
    dtb = jnp.concatenate([dt_bias.astype(F32).reshape(-1), jnp.zeros((pad,), F32)]).reshape(1, -1)
    return pl.pallas_call(
        functools.partial(_gdn_conv_kernel, tiles_per_seq=seq // tm),
        grid=(T // tm, 3),
        in_specs=[pl.BlockSpec((tm, D), lambda i, j: (i, j)),
                  pl.BlockSpec((GDN_HALO, D), lambda i, j: (jnp.maximum(i * hb - 1, 0), j)),
                  pl.BlockSpec((GDN_HALO, D), lambda i, j: (jnp.minimum((i + 1) * hb, n_halo - 1), j)),
                  pl.BlockSpec((B_CONV, D), lambda i, j: (0, j)),
                  pl.BlockSpec((tm, ROUTER_LANES), lambda i, j: (i, 0)),
                  pl.BlockSpec((1, ROUTER_LANES), lambda i, j: (0, 0)),
                  pl.BlockSpec((1, ROUTER_LANES), lambda i, j: (0, 0))],
        out_specs=[pl.BlockSpec((1, tm, D), lambda i, j: (j, i, 0)),
                   pl.BlockSpec((tm, ROUTER_LANES), lambda i, j: (i, 0))],
        out_shape=[jax.ShapeDtypeStruct((3, T, D), F32), jax.ShapeDtypeStruct((T, ROUTER_LANES), F32)],
        compiler_params=_cparams("parallel", "arbitrary"),
        name="gdn_conv",
    )(qkvz, qkvz, qkvz, conv_w, ab, aexp, dtb)


def _gdn_gate_terms(gb, incl):
    C = gb.shape[0]
    lane = lax.broadcasted_iota(jnp.int32, gb.shape, 1)
    g_hi, g_lo = _split_bf16(jnp.where(lane < 2 * B_HEADS, gb, 0.0))
    tri = incl.astype(BF16)
    gc = jnp.dot(tri, g_hi, preferred_element_type=F32) + jnp.dot(tri, g_lo, preferred_element_type=F32)
    return gc, gc.T


def _gdn_scan_kernel(qf_ref, kf_ref, vf_ref, gf_ref, qb_ref, kb_ref, vb_ref, gbw_ref, of_ref, ob_ref, sf_ref, sb_ref):
    @pl.when(pl.program_id(1) == 0)
    def _():
        sf_ref[...] = jnp.zeros_like(sf_ref)
        sb_ref[...] = jnp.zeros_like(sb_ref)

    C = GDN_CHUNK
    n_chunks = gf_ref.shape[0] // C
    dk = B_HEAD_DIM
    row = lax.broadcasted_iota(jnp.int32, (C, C), 0)
    colm = lax.broadcasted_iota(jnp.int32, (C, C), 1)
    eye = (row == colm).astype(F32)
    incl = (row >= colm, row <= colm)
    strict = (row > colm, row < colm)
    qkv_refs = ((qf_ref, kf_ref, vf_ref), (qb_ref, kb_ref, vb_ref))
    s_refs = (sf_ref, sb_ref)
    o_refs = (of_ref, ob_ref)

    def chunk_step(c, carry):
        rows = (pl.ds(pl.multiple_of(c * C, C), C), pl.ds(pl.multiple_of((n_chunks - 1 - c) * C, C), C))
        _gdn_chunk_pair(rows, (gf_ref, gbw_ref), qkv_refs, s_refs, o_refs, incl, strict, eye)
        return carry

    lax.fori_loop(0, n_chunks, chunk_step, 0)


def _gdn_chunk_pair(rows, g_refs, qkv_refs, s_refs, o_refs, incl, strict, eye):
    C = GDN_CHUNK
    dk = B_HEAD_DIM
    gbs = tuple(g_refs[d][rows[d], :] for d in range(2))
    gates = [_gdn_gate_terms(gbs[d], incl[d]) for d in range(2)]
    glast = [gates[0][0][C - 1:C], gates[1][0][0:1]]
    units = [(d, h) for d in range(2) for h in range(B_HEADS)]

    def lane_of(d, h):
        return d * B_HEADS + h

    def cols(h):
        return slice(h * dk, (h + 1) * dk)

    v_b, kb_l, qd_bf, kd_bf, a_l, dec_l, egc_l = [], [], [], [], [], [], []
    for d, h in units:
        r = lane_of(d, h)
        gc, gct = gates[d]
        gcol = gc[:, r:r + 1]
        beta = gbs[d][:, 2 * B_HEADS + r:2 * B_HEADS + r + 1]
        q_ref, k_ref, v_ref = qkv_refs[d]
        qh, kh, vh = q_ref[0, rows[d], cols(h)], k_ref[0, rows[d], cols(h)], v_ref[0, rows[d], cols(h)]
        egc = jnp.exp(gcol)
        kb = kh * beta
        khb = kh.astype(BF16)
        a_l.append(_bdot_nt(jnp.concatenate([kb, qh], axis=0), khb))
        dec_l.append(jnp.exp(jnp.where(incl[d], gcol - gct[r:r + 1, :], NEG_INF)))
        v_b.append(vh * beta)
        kb_l.append(kb)
        egc_l.append(egc)
        qd_bf.append((qh * egc).astype(BF16))
        kd_bf.append((kh * jnp.exp(glast[d][:, r:r + 1] - gcol)).astype(BF16))
    x_l = [-jnp.where(strict[d], a[:C] * dec, 0.0) for (d, h), a, dec in zip(units, a_l, dec_l)]
    intra_bf = [(a[C:] * dec).astype(BF16) for a, dec in zip(a_l, dec_l)]
    p_l = [eye + x for x in x_l]
    n = 1
    while 2 * n < C:
        x_l = [_bdot(x, x) for x in x_l]
        n *= 2
        p_l = [p + _bdot(p, x) for p, x in zip(p_l, x_l)]
    sol_l = [_bdot(p, jnp.concatenate([vb, kb * egc], axis=1))
             for p, vb, kb, egc in zip(p_l, v_b, kb_l, egc_l)]
    st_l = [s_refs[d][h] for d, h in units]
    wq_l = [_bdot(jnp.concatenate([sol[:, dk:].astype(BF16), qd], axis=0), st)
            for sol, qd, st in zip(sol_l, qd_bf, st_l)]
    vn_l = [sol[:, :dk] - wq[:C] for sol, wq in zip(sol_l, wq_l)]
    for (d, h), wq, intra, vn in zip(units, wq_l, intra_bf, vn_l):
        o_refs[d][rows[d], cols(h)] = wq[C:] + _bdot(intra, vn)
    for (d, h), st, kd, vn in zip(units, st_l, kd_bf, vn_l):
        r = lane_of(d, h)
        s_refs[d][h] = st * jnp.exp(glast[d][:, r:r + 1]) + _bdot_tn(kd, vn)


def _gdn_scan(qkv, gb, batch, seq, rows_per_step):
    _, T, D = qkv.shape
    chunk = rows_per_step
    nc = seq // chunk
    fwd = lambda b, c: b * nc + c
    bwd = lambda b, c: b * nc + (nc - 1 - c)
    part = lambda p, f: pl.BlockSpec((1, chunk, D), lambda b, c: (p, f(b, c), 0))
    gspec = lambda f: pl.BlockSpec((chunk, ROUTER_LANES), lambda b, c: (f(b, c), 0))
    ospec = lambda f: pl.BlockSpec((chunk, D), lambda b, c: (f(b, c), 0))
    return pl.pallas_call(
        _gdn_scan_kernel,
        grid=(batch, nc),
        in_specs=[part(0, fwd), part(1, fwd), part(2, fwd), gspec(fwd),
                  part(0, bwd), part(1, bwd), part(2, bwd), gspec(bwd)],
        out_specs=[ospec(fwd), ospec(bwd)],
        out_shape=[jax.ShapeDtypeStruct((T, D), F32), jax.ShapeDtypeStruct((T, D), F32)],
        scratch_shapes=[pltpu.VMEM((B_HEADS, B_HEAD_DIM, B_HEAD_DIM), F32),
                        pltpu.VMEM((B_HEADS, B_HEAD_DIM, B_HEAD_DIM), F32)],
        compiler_params=_cparams("parallel", "arbitrary"),
        name="gdn_scan",
    )(qkv, qkv, qkv, gb, qkv, qkv, qkv, gb)


def _gdn_out_kernel(of_ref, ob_ref, z_ref, og_ref, w_ref, x_ref, o_ref, a_ref):
    @pl.when(pl.program_id(1) == 0)
    def _():
        o = of_ref[...] + ob_ref[...]
        z = z_ref[...].astype(F32)
        for h in range(B_HEADS):
            sl = slice(h * B_HEAD_DIM, (h + 1) * B_HEAD_DIM)
            zh = z[:, sl]
            a_ref[:, sl] = (_rms(o[:, sl], og_ref[...]) * (zh * jax.nn.sigmoid(zh))).astype(BF16)

    o_ref[...] = x_ref[...] + jnp.dot(a_ref[...], w_ref[...], preferred_element_type=F32)


def _gdn_out(o_f, o_b, qkvz, o_gain, w, x, tm, tn):
    T, D = x.shape
    return pl.pallas_call(
        _gdn_out_kernel,
        grid=(T // tm, D // tn),
        in_specs=[pl.BlockSpec((tm, D), lambda i, j: (i, 0)),
                  pl.BlockSpec((tm, D), lambda i, j: (i, 0)),
                  pl.BlockSpec((tm, D), lambda i, j: (i, 3)),
                  pl.BlockSpec((1, B_HEAD_DIM), lambda i, j: (0, 0)),
                  pl.BlockSpec((D, tn), lambda i, j: (0, j)),
                  pl.BlockSpec((tm, tn), lambda i, j: (i, j))],
        out_specs=pl.BlockSpec((tm, tn), lambda i, j: (i, j)),
        out_shape=jax.ShapeDtypeStruct((T, D), F32),
        scratch_shapes=[pltpu.VMEM((tm, D), BF16)],
        compiler_params=_cparams("parallel", "arbitrary"),
        name="gdn_out",
    )(o_f, o_b, qkvz, o_gain.reshape(1, B_HEAD_DIM), w, x)


def _attention_layer(x, gain, w_in, q_gain, k_gain, sink, w_out, batch, seq):
    qkv = _norm_matmul(x, gain, w_in.astype(BF16), tm=1024, tn=768)
    a = _attention(qkv, q_gain, k_gain, sink, batch, seq)
    return _matmul_residual(a, w_out.astype(BF16), x, tm=1024, tn=1024)


def _gdn_layer(x, gain, w_in, conv_w, a_log, dt_bias, o_gain, w_out, batch, seq):
    D = x.shape[1]
    w_main = w_in[:, :4 * D].astype(BF16)
    pad = ROUTER_LANES - 4 * B_HEADS
    w_ab = jnp.concatenate([w_in[:, 4 * D:], jnp.zeros((D, pad), F32)], axis=1)
    qkvz, ab = _norm_matmul2(x, gain, w_main, w_ab, tm=1024, tn=1024)
    qkv, gb = _gdn_conv(qkvz, ab, conv_w, a_log, dt_bias, seq, tm=512)
    o_f, o_b = _gdn_scan(qkv, gb, batch, seq, 4 * GDN_CHUNK)
    return _gdn_out(o_f, o_b, qkvz, o_gain, w_out.astype(BF16), x, tm=512, tn=1024)


def kernel(x, norm_mix, norm_ffn, attn_w_in, attn_q_gain, attn_k_gain, attn_sink, attn_w_out, gdn_w_in, gdn_conv, gdn_a_log, gdn_dt_bias, gdn_o_gain, gdn_w_out, moe_w_group, moe_b_group, moe_w_expert, moe_b_expert, moe_w_gate, moe_w_up, moe_w_down):
    batch, seq, d_model = x.shape
    depth = norm_mix.shape[0]
    xt = x.reshape(batch * seq, d_model)
    for i in range(depth):
        j = i // 2
        if i % 2 == 0:
            xt = _attention_layer(xt, norm_mix[i], attn_w_in[j], attn_q_gain[j], attn_k_gain[j],
                                  attn_sink[j], attn_w_out[j], batch, seq)
        else:
            xt = _gdn_layer(xt, norm_mix[i], gdn_w_in[j], gdn_conv[j], gdn_a_log[j], gdn_dt_bias[j],
                            gdn_o_gain[j], gdn_w_out[j], batch, seq)
        xt = _moe(xt, norm_ffn[i], moe_w_group[i], moe_b_group[i], moe_w_expert[i], moe_b_expert[i],
                  moe_w_gate, moe_w_up, moe_w_down, layer=i)
    return xt.reshape(batch, seq, d_model)
```

```python
import functools
import math

import jax
import jax.numpy as jnp
import numpy as np
from jax import lax
from jax.experimental import pallas as pl
from jax.experimental.pallas import tpu as pltpu

RMS_EPS = 1e-6
NEG_INF = -1e30
F32 = jnp.float32
BF16 = jnp.bfloat16

A_HEADS = 16
A_KV_HEADS = 4
A_HEAD_DIM = 64
A_REP = A_HEADS // A_KV_HEADS
A_BLOCK = 128
B_HEADS = 8
B_HEAD_DIM = 128
B_CONV = 4
GDN_CHUNK = 64
N_GROUPS = 4
EXPERTS_PER_GROUP = 8
N_EXPERTS = 32
D_EXPERT = 256
ROUTER_LANES = 128

V7X_VMEM_LIMIT_BYTES = 56 * 1024 * 1024


def _cparams(*sem):
    return pltpu.CompilerParams(dimension_semantics=sem, vmem_limit_bytes=V7X_VMEM_LIMIT_BYTES)


def _bdot(a, b):
    return jnp.dot(a.astype(BF16), b.astype(BF16), preferred_element_type=F32)


def _bdot_nt(a, b):
    return lax.dot_general(a.astype(BF16), b.astype(BF16), (((1,), (1,)), ((), ())),
                           preferred_element_type=F32)


def _bdot_tn(a, b):
    return lax.dot_general(a.astype(BF16), b.astype(BF16), (((0,), (0,)), ((), ())),
                           preferred_element_type=F32)


def _split_bf16(a):
    hi = a.astype(BF16)
    lo = (a - hi.astype(F32)).astype(BF16)
    return hi, lo


def _rms(x, gain):
    return x * lax.rsqrt(jnp.mean(x * x, axis=-1, keepdims=True) + RMS_EPS) * gain


def _norm_matmul_kernel(x_ref, g_ref, w_ref, o_ref, xn_ref):
    @pl.when(pl.program_id(1) == 0)
    def _():
        xn_ref[...] = _rms(x_ref[...], g_ref[...]).astype(BF16)

    o_ref[...] = jnp.dot(xn_ref[...], w_ref[...], preferred_element_type=F32)


def _norm_matmul(x, gain, w, tm, tn):
    T, D = x.shape
    N = w.shape[1]
    return pl.pallas_call(
        _norm_matmul_kernel,
        grid=(T // tm, N // tn),
        in_specs=[pl.BlockSpec((tm, D), lambda i, j: (i, 0)),
                  pl.BlockSpec((1, D), lambda i, j: (0, 0)),
                  pl.BlockSpec((D, tn), lambda i, j: (0, j))],
        out_specs=pl.BlockSpec((tm, tn), lambda i, j: (i, j)),
        out_shape=jax.ShapeDtypeStruct((T, N), F32),
        scratch_shapes=[pltpu.VMEM((tm, D), BF16)],
        compiler_params=_cparams("parallel", "arbitrary"),
        name="norm_matmul",
    )(x, gain.reshape(1, D), w)


def _norm_matmul2_kernel(x_ref, g_ref, w_ref, w2_ref, o_ref, o2_ref, xn_ref):
    @pl.when(pl.program_id(1) == 0)
    def _():
        xn = _rms(x_ref[...], g_ref[...])
        xn_ref[...] = xn.astype(BF16)
        x_hi, x_lo = _split_bf16(xn)
        w_hi, w_lo = _split_bf16(w2_ref[...])
        o2_ref[...] = (jnp.dot(x_hi, w_hi, preferred_element_type=F32)
                       + jnp.dot(x_lo, w_hi, preferred_element_type=F32)
                       + jnp.dot(x_hi, w_lo, preferred_element_type=F32))

    o_ref[...] = jnp.dot(xn_ref[...], w_ref[...], preferred_element_type=F32).astype(o_ref.dtype)


def _norm_matmul2(x, gain, w, w2, tm, tn):
    T, D = x.shape
    N = w.shape[1]
    N2 = w2.shape[1]
    return pl.pallas_call(
        _norm_matmul2_kernel,
        grid=(T // tm, N // tn),
        in_specs=[pl.BlockSpec((tm, D), lambda i, j: (i, 0)),
                  pl.BlockSpec((1, D), lambda i, j: (0, 0)),
                  pl.BlockSpec((D, tn), lambda i, j: (0, j)),
                  pl.BlockSpec((D, N2), lambda i, j: (0, 0))],
        out_specs=[pl.BlockSpec((tm, tn), lambda i, j: (i, j)),
                   pl.BlockSpec((tm, N2), lambda i, j: (i, 0))],
        out_shape=[jax.ShapeDtypeStruct((T, N), BF16), jax.ShapeDtypeStruct((T, N2), F32)],
        scratch_shapes=[pltpu.VMEM((tm, D), BF16)],
        compiler_params=_cparams("parallel", "arbitrary"),
        name="norm_matmul2",
    )(x, gain.reshape(1, D), w, w2)


def _matmul_residual_kernel(a_ref, w_ref, x_ref, o_ref):
    o_ref[...] = x_ref[...] + jnp.dot(a_ref[...], w_ref[...], preferred_element_type=F32)


def _matmul_residual(a, w, x, tm, tn):
    T, K = a.shape
    N = w.shape[1]
    return pl.pallas_call(
        _matmul_residual_kernel,
        grid=(T // tm, N // tn),
        in_specs=[pl.BlockSpec((tm, K), lambda i, j: (i, 0)),
                  pl.BlockSpec((K, tn), lambda i, j: (0, j)),
                  pl.BlockSpec((tm, tn), lambda i, j: (i, j))],
        out_specs=pl.BlockSpec((tm, tn), lambda i, j: (i, j)),
        out_shape=jax.ShapeDtypeStruct((T, N), F32),
        compiler_params=_cparams("parallel", "arbitrary"),
        name="matmul_residual",
    )(a, w, x)


def _attn_kernel(main_ref, prev_ref, next_ref, qg_ref, kg_ref, sink_ref, bias_ref, o_ref):
    n = pl.program_id(1)
    nb = pl.num_programs(1)
    dh, blk = A_HEAD_DIM, A_BLOCK
    kv_cols = A_KV_HEADS * dh
    main = main_ref[...]
    kv = jnp.concatenate([prev_ref[...], main[:, A_HEADS * dh:], next_ref[...]], axis=0)
    col = lax.broadcasted_iota(jnp.int32, (1, 3 * blk), 1)
    outside = ((col < blk) & (n == 0)) | ((col >= 2 * blk) & (n == nb - 1))
    edge = jnp.where(outside, NEG_INF, 0.0)
    qg = qg_ref[...] * (dh ** -0.5)
    groups = range(A_KV_HEADS)
    kn = [_rms(kv[:, g * dh:(g + 1) * dh], kg_ref[...]).astype(BF16) for g in groups]
    q4 = [jnp.concatenate([_rms(main[:, (A_REP * g + r) * dh:(A_REP * g + r + 1) * dh], qg).astype(BF16)
                           for r in range(A_REP)], axis=0) for g in groups]
    s = [_bdot_nt(q4[g], kn[g]) + bias_ref[g] + edge for g in groups]
    m = [jnp.maximum(jnp.max(s[g], axis=-1, keepdims=True), sink_ref[g]) for g in groups]
    p = [jnp.exp(s[g] - m[g]) for g in groups]
    denom = [jnp.sum(p[g], axis=-1, keepdims=True) + jnp.exp(sink_ref[g] - m[g]) for g in groups]
    o = [jnp.dot(p[g].astype(BF16), kv[:, kv_cols + g * dh: kv_cols + (g + 1) * dh].astype(BF16),
                 preferred_element_type=F32) / denom[g] for g in groups]
    for g in groups:
        for r in range(A_REP):
            h = A_REP * g + r
            o_ref[:, h * dh:(h + 1) * dh] = o[g][r * blk:(r + 1) * blk].astype(o_ref.dtype)


def _attn_tables(sink):
    blk = A_BLOCK
    slopes = np.array([2.0 ** (-8.0 * (h + 1) / A_HEADS) for h in range(A_HEADS)], np.float32)
    qi = np.arange(blk)
    kj = np.arange(3 * blk)
    dist = np.abs(blk + qi[:, None] - kj[None, :]).astype(np.float32)
    bias = np.where(dist[None] <= blk, -slopes[:, None, None] * dist[None], np.float32(NEG_INF))
    bias = bias.astype(np.float32).reshape(A_KV_HEADS, A_REP * blk, 3 * blk)
    sink_rows = jnp.repeat(sink.astype(F32).reshape(A_KV_HEADS, A_REP), blk, axis=1)[..., None]
    return jnp.asarray(bias), sink_rows


def _attention(qkv, q_gain, k_gain, sink, batch, seq):
    T, W = qkv.shape
    blk, dh = A_BLOCK, A_HEAD_DIM
    nb = seq // blk
    kv_w = 2 * A_KV_HEADS * dh
    kv_blk = (A_HEADS * dh) // kv_w
    bias, sink_rows = _attn_tables(sink)
    return pl.pallas_call(
        _attn_kernel,
        grid=(batch, nb),
        in_specs=[
            pl.BlockSpec((blk, W), lambda b, n: (b * nb + n, 0)),
            pl.BlockSpec((blk, kv_w), lambda b, n: (b * nb + jnp.maximum(n - 1, 0), kv_blk)),
            pl.BlockSpec((blk, kv_w), lambda b, n: (b * nb + jnp.minimum(n + 1, nb - 1), kv_blk)),
            pl.BlockSpec((1, dh), lambda b, n: (0, 0)),
            pl.BlockSpec((1, dh), lambda b, n: (0, 0)),
            pl.BlockSpec((A_KV_HEADS, A_REP * blk, 1), lambda b, n: (0, 0, 0)),
            pl.BlockSpec((A_KV_HEADS, A_REP * blk, 3 * blk), lambda b, n: (0, 0, 0)),
        ],
        out_specs=pl.BlockSpec((blk, A_HEADS * dh), lambda b, n: (b * nb + n, 0)),
        out_shape=jax.ShapeDtypeStruct((T, A_HEADS * dh), BF16),
        compiler_params=_cparams("parallel", "parallel"),
        name="window_attention",
    )(qkv, qkv, qkv, q_gain.reshape(1, dh), k_gain.reshape(1, dh), sink_rows, bias)


def _route(logits):
    lane = lax.broadcasted_iota(jnp.int32, logits.shape, 1).astype(F32)
    big = jnp.float32(1e9)
    is_g = (lane >= N_EXPERTS) & (lane < N_EXPERTS + N_GROUPS)
    lg = jnp.where(is_g, logits, NEG_INF)
    gmax = jnp.max(lg, axis=-1, keepdims=True)
    gidx = jnp.min(jnp.where(is_g & (lg == gmax), lane, big), axis=-1, keepdims=True) - N_EXPERTS
    g_prob = 1.0 / jnp.sum(jnp.where(is_g, jnp.exp(lg - gmax), 0.0), axis=-1, keepdims=True)
    lo = gidx * EXPERTS_PER_GROUP
    in_grp = (lane >= lo) & (lane < lo + EXPERTS_PER_GROUP)
    le = jnp.where(in_grp, logits, NEG_INF)
    emax = jnp.max(le, axis=-1, keepdims=True)
    ex = jnp.where(in_grp, jnp.exp(le - emax), 0.0)
    prob = ex / jnp.sum(ex, axis=-1, keepdims=True)
    cand = jnp.where(in_grp, prob, -1.0)
    p1 = jnp.max(cand, axis=-1, keepdims=True)
    i1 = jnp.min(jnp.where(cand == p1, lane, big), axis=-1, keepdims=True)
    cand2 = jnp.where(lane == i1, -1.0, cand)
    p2 = jnp.max(cand2, axis=-1, keepdims=True)
    i2 = jnp.min(jnp.where(cand2 == p2, lane, big), axis=-1, keepdims=True)
    scale = g_prob / (p1 + p2)
    return i1, i2, p1 * scale, p2 * scale


RT_E1, RT_E2, RT_G1, RT_G2, RT_R1, RT_R2 = range(6)


def _pack_bf16_pairs(a):
    n = a.shape[1] // 2
    hi = lax.bitcast_convert_type(a[:, :n].astype(BF16).astype(F32), jnp.uint32)
    lo = lax.bitcast_convert_type(a[:, n:].astype(BF16).astype(F32), jnp.uint32)
    return hi | (lo >> 16)


def _unpack_bf16_pairs(p):
    hi = lax.bitcast_convert_type(p & jnp.uint32(0xFFFF0000), F32)
    lo = lax.bitcast_convert_type(p << 16, F32)
    return jnp.concatenate([hi, lo], axis=1)


LANES = 128
D_MODEL = 1024
TOKEN_ROWS = D_MODEL // 2 // LANES


def _store_token_major(ref, packed):
    m, w = packed.shape
    s_per = w // LANES
    for s in range(s_per):
        ref[pl.ds(s, m, stride=s_per), :] = packed[:, s * LANES:(s + 1) * LANES]


def _load_token_major(ref, m, s_per):
    return jnp.concatenate([ref[pl.ds(s, m, stride=s_per), :] for s in range(s_per)], axis=1)


def _router_kernel(x_ref, g_ref, wr_ref, br_ref, h_ref, route_ref, counts_ref, carry_ref):
    @pl.when(pl.program_id(0) == 0)
    def _():
        carry_ref[...] = jnp.zeros_like(carry_ref)

    hn = _rms(x_ref[...], g_ref[...])
    _store_token_major(h_ref, _pack_bf16_pairs(hn))
    h_hi, h_lo = _split_bf16(hn)
    w_hi, w_lo = _split_bf16(wr_ref[...])
    logits = (jnp.dot(h_hi, w_hi, preferred_element_type=F32)
              + jnp.dot(h_lo, w_hi, preferred_element_type=F32)
              + jnp.dot(h_hi, w_lo, preferred_element_type=F32)) + br_ref[...]
    i1, i2, g1, g2 = _route(logits)
    tm = logits.shape[0]
    lane = lax.broadcasted_iota(jnp.int32, logits.shape, 1).astype(F32)
    chosen = ((lane == i1) | (lane == i2)).astype(BF16)
    earlier = (lax.broadcasted_iota(jnp.int32, (tm, tm), 1)
               < lax.broadcasted_iota(jnp.int32, (tm, tm), 0)).astype(BF16)
    before = jnp.dot(earlier, chosen, preferred_element_type=F32) + carry_ref[...]
    r1 = jnp.sum(jnp.where(lane == i1, before, 0.0), axis=-1, keepdims=True)
    r2 = jnp.sum(jnp.where(lane == i2, before, 0.0), axis=-1, keepdims=True)
    carry_ref[...] += jnp.sum(chosen.astype(F32), axis=0, keepdims=True)
    rec = jnp.zeros_like(logits)
    for slot, val in ((RT_E1, i1), (RT_E2, i2), (RT_G1, g1), (RT_G2, g2), (RT_R1, r1), (RT_R2, r2)):
        rec = jnp.where(lane == slot, val, rec)
    route_ref[...] = rec
    counts_ref[...] = carry_ref[...]


def _router(x, gain, w_group, b_group, w_expert, b_expert, tm):
    T, D = x.shape
    pad = ROUTER_LANES - N_EXPERTS - N_GROUPS
    wr = jnp.concatenate([w_expert, w_group, jnp.zeros((D, pad), F32)], axis=1)
    br = jnp.concatenate([b_expert, b_group, jnp.zeros((pad,), F32)]).reshape(1, ROUTER_LANES)
    return pl.pallas_call(
        _router_kernel,
        grid=(T // tm,),
        in_specs=[pl.BlockSpec((tm, D), lambda i: (i, 0)),
                  pl.BlockSpec((1, D), lambda i: (0, 0)),
                  pl.BlockSpec((D, ROUTER_LANES), lambda i: (0, 0)),
                  pl.BlockSpec((1, ROUTER_LANES), lambda i: (0, 0))],
        out_specs=[pl.BlockSpec((tm * TOKEN_ROWS, LANES), lambda i: (i, 0)),
                   pl.BlockSpec((tm, ROUTER_LANES), lambda i: (i, 0)),
                   pl.BlockSpec((1, ROUTER_LANES), lambda i: (0, 0))],
        out_shape=[jax.ShapeDtypeStruct((T * TOKEN_ROWS, LANES), jnp.uint32),
                   jax.ShapeDtypeStruct((T, ROUTER_LANES), F32),
                   jax.ShapeDtypeStruct((1, ROUTER_LANES), F32)],
        scratch_shapes=[pltpu.VMEM((1, ROUTER_LANES), F32)],
        compiler_params=_cparams("arbitrary"),
        name="moe_router",
    )(x, gain.reshape(1, D), wr, br)


MOE_TILE = 512
MOE_SLOTS = 2
DMA_UNROLL = 8


def _token_copy(src, src_row, dst, dst_row, sem):
    return pltpu.make_async_copy(src.at[pl.ds(pl.multiple_of(src_row, TOKEN_ROWS), TOKEN_ROWS)],
                                 dst.at[pl.ds(pl.multiple_of(dst_row, TOKEN_ROWS), TOKEN_ROWS)], sem)


def _dispatch_kernel(pos_ref, h_ref, zeros_hbm, xs_hbm, sem):
    del zeros_hbm
    tm = h_ref.shape[0] // TOKEN_ROWS

    def start(g, c):
        r0 = pl.multiple_of(g * DMA_UNROLL, DMA_UNROLL)
        for j in range(DMA_UNROLL):
            for k in range(MOE_SLOTS):
                _token_copy(h_ref, (r0 + j) * TOKEN_ROWS, xs_hbm, pos_ref[MOE_SLOTS * (r0 + j) + k],
                            sem).start(priority=k)
        return c

    def wait(r, c):
        for k in range(MOE_SLOTS):
            _token_copy(h_ref, 0, xs_hbm, 0, sem).wait()
        return c

    lax.fori_loop(0, tm // DMA_UNROLL, start, 0)
    lax.fori_loop(0, tm, wait, 0, unroll=DMA_UNROLL)


def _dispatch(h, pos, n_rows, tm):
    T = h.shape[0] // TOKEN_ROWS
    zeros = jnp.zeros((n_rows * TOKEN_ROWS, LANES), h.dtype)
    return pl.pallas_call(
        _dispatch_kernel,
        grid=(T // tm,),
        in_specs=[pl.BlockSpec((MOE_SLOTS * tm,), lambda i: (i,), memory_space=pltpu.SMEM),
                  pl.BlockSpec((tm * TOKEN_ROWS, LANES), lambda i: (i, 0)),
                  pl.BlockSpec(memory_space=pl.ANY)],
        out_specs=pl.BlockSpec(memory_space=pl.ANY),
        out_shape=jax.ShapeDtypeStruct(zeros.shape, h.dtype),
        scratch_shapes=[pltpu.SemaphoreType.DMA(())],
        input_output_aliases={2: 0},
        compiler_params=_cparams("arbitrary"),
        name="moe_dispatch",
    )(pos, h, zeros)


def _expert_kernel(te_ref, nv_ref, xs_ref, wg_ref, wu_ref, wd_ref, ys_ref, wgu_s, wd_s):
    i = pl.program_id(0)
    nv = nv_ref[0]
    valid = i < nv
    ic = jnp.minimum(i, nv - 1)
    changed = (i == 0) | (te_ref[ic] != te_ref[jnp.maximum(ic - 1, 0)])

    @pl.when(valid & changed)
    def _():
        wgu_s[:, :D_EXPERT] = wg_ref[0, 0].astype(BF16)
        wgu_s[:, D_EXPERT:] = wu_ref[0, 0].astype(BF16)
        wd_s[...] = wd_ref[0, 0].astype(BF16)

    @pl.when(valid)
    def _():
        x = _unpack_bf16_pairs(_load_token_major(xs_ref, MOE_TILE, TOKEN_ROWS)).astype(BF16)
        gu = jnp.dot(x, wgu_s[...], preferred_element_type=F32)
        gate, up = gu[:, :D_EXPERT], gu[:, D_EXPERT:]
        hid = (gate * jax.nn.sigmoid(gate)) * up
        y = jnp.dot(hid.astype(BF16), wd_s[...], preferred_element_type=F32)
        _store_token_major(ys_ref, _pack_bf16_pairs(y))

    @pl.when(jnp.logical_not(valid))
    def _():
        ys_ref[...] = jnp.zeros_like(ys_ref)


def _experts(xs, tile_expert, n_valid, w_gate, w_up, w_down, layer):
    D = TOKEN_ROWS * LANES * 2
    blk = MOE_TILE * TOKEN_ROWS
    n_tiles = xs.shape[0] // blk
    row = lambda i, te, nv: (jnp.minimum(i, nv[0] - 1), 0)
    wsel = lambda i, te, nv: (layer, te[jnp.minimum(i, nv[0] - 1)], 0, 0)
    return pl.pallas_call(
        _expert_kernel,
        grid_spec=pltpu.PrefetchScalarGridSpec(
            num_scalar_prefetch=2,
            grid=(n_tiles,),
            in_specs=[pl.BlockSpec((blk, LANES), row),
                      pl.BlockSpec((1, 1, D, D_EXPERT), wsel),
                      pl.BlockSpec((1, 1, D, D_EXPERT), wsel),
                      pl.BlockSpec((1, 1, D_EXPERT, D), wsel)],
            out_specs=pl.BlockSpec((blk, LANES), lambda i, te, nv: (i, 0)),
            scratch_shapes=[pltpu.VMEM((D, 2 * D_EXPERT), BF16), pltpu.VMEM((D_EXPERT, D), BF16)]),
        out_shape=jax.ShapeDtypeStruct(xs.shape, jnp.uint32),
        compiler_params=_cparams("arbitrary"),
        name="moe_experts",
    )(tile_expert, n_valid, xs, w_gate, w_up, w_down)


def _combine_kernel(pos_ref, pos_next_ref, x_ref, route_ref, ys_hbm, o_ref, buf, sems):
    i = pl.program_id(0)
    tm = x_ref.shape[0]

    def gather(p_ref, ring):
        def start(g, c):
            r0 = pl.multiple_of(g * DMA_UNROLL, DMA_UNROLL)
            for j in range(DMA_UNROLL):
                for k in range(MOE_SLOTS):
                    _token_copy(ys_hbm, p_ref[MOE_SLOTS * (r0 + j) + k], buf.at[ring, k], (r0 + j) * TOKEN_ROWS,
                                sems.at[ring]).start(priority=k)
            return c
        lax.fori_loop(0, tm // DMA_UNROLL, start, 0)

    @pl.when(i == 0)
    def _():
        gather(pos_ref, 0)

    @pl.when(i + 1 < pl.num_programs(0))
    def _():
        gather(pos_next_ref, (i + 1) % 2)

    ring = i % 2

    def wait(r, c):
        for k in range(MOE_SLOTS):
            _token_copy(ys_hbm, 0, buf.at[ring, k], 0, sems.at[ring]).wait()
        return c

    lax.fori_loop(0, tm, wait, 0, unroll=DMA_UNROLL)
    rec = route_ref[...]
    y1 = _unpack_bf16_pairs(_load_token_major(buf.at[ring, 0], tm, TOKEN_ROWS))
    y2 = _unpack_bf16_pairs(_load_token_major(buf.at[ring, 1], tm, TOKEN_ROWS))
    o_ref[...] = x_ref[...] + rec[:, RT_G1:RT_G1 + 1] * y1 + rec[:, RT_G2:RT_G2 + 1] * y2


def _combine(x, route, pos, ys, tm):
    T, D = x.shape
    n = T // tm
    return pl.pallas_call(
        _combine_kernel,
        grid=(n,),
        in_specs=[pl.BlockSpec((MOE_SLOTS * tm,), lambda i: (i,), memory_space=pltpu.SMEM),
                  pl.BlockSpec((MOE_SLOTS * tm,), lambda i: (jnp.minimum(i + 1, n - 1),), memory_space=pltpu.SMEM),
                  pl.BlockSpec((tm, D), lambda i: (i, 0)),
                  pl.BlockSpec((tm, ROUTER_LANES), lambda i: (i, 0)),
                  pl.BlockSpec(memory_space=pl.ANY)],
        out_specs=pl.BlockSpec((tm, D), lambda i: (i, 0)),
        out_shape=jax.ShapeDtypeStruct((T, D), F32),
        scratch_shapes=[pltpu.VMEM((2, MOE_SLOTS, tm * TOKEN_ROWS, LANES), ys.dtype),
                        pltpu.SemaphoreType.DMA((2,))],
        compiler_params=_cparams("arbitrary"),
        name="moe_combine",
    )(pos, pos, x, route, ys)


def _moe(x, gain, w_group, b_group, w_expert, b_expert, w_gate, w_up, w_down, layer):
    T, D = x.shape
    n_tiles = (MOE_SLOTS * T + N_EXPERTS * (MOE_TILE - 1)) // MOE_TILE + 1
    h, route, counts = _router(x, gain, w_group, b_group, w_expert, b_expert, tm=512)
    counts = counts[0, :N_EXPERTS].astype(jnp.int32)
    padded = (counts + MOE_TILE - 1) // MOE_TILE * MOE_TILE
    ends = jnp.cumsum(padded)
    starts = ends - padded
    eid = route[:, RT_E1:RT_E2 + 1].astype(jnp.int32)
    rank = route[:, RT_R1:RT_R2 + 1].astype(jnp.int32)
    onehot = eid[..., None] == jnp.arange(N_EXPERTS, dtype=jnp.int32)
    pos = (rank + jnp.sum(jnp.where(onehot, starts, 0), axis=-1)).reshape(-1)
    pos = pos * TOKEN_ROWS
    tile_start = jnp.arange(n_tiles, dtype=jnp.int32) * MOE_TILE
    tile_expert = jnp.minimum(jnp.sum(tile_start[:, None] >= ends[None, :], axis=-1), N_EXPERTS - 1).astype(jnp.int32)
    n_valid = (ends[-1:] // MOE_TILE).astype(jnp.int32)
    xs = _dispatch(h, pos, n_tiles * MOE_TILE, tm=1024)
    ys = _experts(xs, tile_expert, n_valid, w_gate, w_up, w_down, layer)
    return _combine(x, route, pos, ys, tm=512)


GDN_HALO = 16


def _gdn_conv_kernel(u_ref, up_ref, un_ref, w_ref, ab_ref, aexp_ref, dtb_ref, o_ref, gb_ref, *, tiles_per_seq):
    i = pl.program_id(0)
    j = pl.program_id(1)
    tm = u_ref.shape[0]
    first = (i % tiles_per_seq) == 0
    last = (i % tiles_per_seq) == tiles_per_seq - 1
    u = u_ref[...].astype(F32)
    prev = jnp.where(first, 0.0, up_ref[...].astype(F32))
    nxt = jnp.where(last, 0.0, un_ref[...].astype(F32))
    ext = jnp.concatenate([prev, u, nxt], axis=0)
    w = w_ref[...]
    h0 = GDN_HALO - B_CONV // 2
    y = w[0:1] * ext[h0:h0 + tm]
    for t in range(1, B_CONV):
        y = y + w[t:t + 1] * ext[h0 + t:h0 + t + tm]
    y = y * jax.nn.sigmoid(y)
    qscale = jnp.where(j == 0, B_HEAD_DIM ** -0.5, 1.0)
    for h in range(B_HEADS):
        slab = y[:, h * B_HEAD_DIM:(h + 1) * B_HEAD_DIM]
        inv = lax.rsqrt(jnp.sum(slab * slab, axis=-1, keepdims=True) + RMS_EPS) * qscale
        o_ref[0, :, h * B_HEAD_DIM:(h + 1) * B_HEAD_DIM] = slab * jnp.where(j < 2, inv, 1.0)

    @pl.when(j == 0)
    def _():
        ab = ab_ref[...]
        lane = lax.broadcasted_iota(jnp.int32, ab.shape, 1)
        z = ab + dtb_ref[...]
        softplus = jnp.maximum(z, 0.0) + jnp.log1p(jnp.exp(-jnp.abs(z)))
        gb_ref[...] = jnp.where(lane < 2 * B_HEADS, -aexp_ref[...] * softplus, jax.nn.sigmoid(ab))


def _gdn_conv(qkvz, ab, conv_w, a_log, dt_bias, seq, tm):
    T = qkvz.shape[0]
    D = B_HEADS * B_HEAD_DIM
    hb = tm // GDN_HALO
    n_halo = T // GDN_HALO
    pad = ROUTER_LANES - 2 * B_HEADS
    aexp = jnp.concatenate([jnp.exp(a_log.astype(F32)).reshape(-1), jnp.zeros((pad,), F32)]).reshape(1, -1)
    dtb = jnp.concatenate([dt_bias.astype(F32).reshape(-1), jnp.zeros((pad,), F32)]).reshape(1, -1)
    return pl.pallas_call(
        functools.partial(_gdn_conv_kernel, tiles_per_seq=seq // tm),
        grid=(T // tm, 3),
        in_specs=[pl.BlockSpec((tm, D), lambda i, j: (i, j)),
                  pl.BlockSpec((GDN_HALO, D), lambda i, j: (jnp.maximum(i * hb - 1, 0), j)),
                  pl.BlockSpec((GDN_HALO, D), lambda i, j: (jnp.minimum((i + 1) * hb, n_halo - 1), j)),
                  pl.BlockSpec((B_CONV, D), lambda i, j: (0, j)),
                  pl.BlockSpec((tm, ROUTER_LANES), lambda i, j: (i, 0)),
                  pl.BlockSpec((1, ROUTER_LANES), lambda i, j: (0, 0)),
                  pl.BlockSpec((1, ROUTER_LANES), lambda i, j: (0, 0))],
        out_specs=[pl.BlockSpec((1, tm, D), lambda i, j: (j, i, 0)),
                   pl.BlockSpec((tm, ROUTER_LANES), lambda i, j: (i, 0))],
        out_shape=[jax.ShapeDtypeStruct((3, T, D), F32), jax.ShapeDtypeStruct((T, ROUTER_LANES), F32)],
        compiler_params=_cparams("parallel", "arbitrary"),
        name="gdn_conv",
    )(qkvz, qkvz, qkvz, conv_w, ab, aexp, dtb)


def _gdn_gate_terms(gb, incl):
    C = gb.shape[0]
    lane = lax.broadcasted_iota(jnp.int32, gb.shape, 1)
    g_hi, g_lo = _split_bf16(jnp.where(lane < 2 * B_HEADS, gb, 0.0))
    tri = incl.astype(BF16)
    gc = jnp.dot(tri, g_hi, preferred_element_type=F32) + jnp.dot(tri, g_lo, preferred_element_type=F32)
    return gc, jnp.concatenate([gc, jnp.zeros_like(gc)], axis=0).T


def _gdn_scan_kernel(qf_ref, kf_ref, vf_ref, gf_ref, qb_ref, kb_ref, vb_ref, gbw_ref, of_ref, ob_ref, sf_ref, sb_ref):
    @pl.when(pl.program_id(1) == 0)
    def _():
        sf_ref[...] = jnp.zeros_like(sf_ref)
        sb_ref[...] = jnp.zeros_like(sb_ref)

    C = GDN_CHUNK
    n_chunks = gf_ref.shape[0] // C
    dk = B_HEAD_DIM
    row = lax.broadcasted_iota(jnp.int32, (C, C), 0)
    colm = lax.broadcasted_iota(jnp.int32, (C, C), 1)
    tri = (row >= colm, row <= colm)
    row2 = lax.broadcasted_iota(jnp.int32, (C, 2 * C), 0)
    col2 = lax.broadcasted_iota(jnp.int32, (C, 2 * C), 1)
    left = col2 < C
    incl = (left & (row2 >= col2), left & (row2 <= col2))
    strict = (left & (row2 > col2), left & (row2 < col2))
    eye_right = (col2 == row2 + C).astype(F32)
    qkv_refs = ((qf_ref, kf_ref, vf_ref), (qb_ref, kb_ref, vb_ref))
    s_refs = (sf_ref, sb_ref)
    o_refs = (of_ref, ob_ref)

    def chunk_step(c, carry):
        rows = (pl.ds(pl.multiple_of(c * C, C), C), pl.ds(pl.multiple_of((n_chunks - 1 - c) * C, C), C))
        _gdn_chunk_pair(rows, (gf_ref, gbw_ref), qkv_refs, s_refs, o_refs, tri, incl, strict, left, eye_right)
        return carry

    lax.fori_loop(0, n_chunks, chunk_step, 0)


def _gdn_chunk_pair(rows, g_refs, qkv_refs, s_refs, o_refs, tri, incl, strict, left, eye_right):
    C = GDN_CHUNK
    dk = B_HEAD_DIM
    gbs = tuple(g_refs[d][rows[d], :] for d in range(2))
    gates = [_gdn_gate_terms(gbs[d], tri[d]) for d in range(2)]
    glast = [gates[0][0][C - 1:C], gates[1][0][0:1]]
    units = [(d, h) for d in range(2) for h in range(B_HEADS)]

    def lane_of(d, h):
        return d * B_HEADS + h

    def cols(h):
        return slice(h * dk, (h + 1) * dk)

    v_b, kb_l, qd_bf, kd_bf, a_l, dec_l, egc_l = [], [], [], [], [], [], []
    for d, h in units:
        r = lane_of(d, h)
        gc, gct = gates[d]
        gcol = gc[:, r:r + 1]
        beta = gbs[d][:, 2 * B_HEADS + r:2 * B_HEADS + r + 1]
        q_ref, k_ref, v_ref = qkv_refs[d]
        qh, kh, vh = q_ref[0, rows[d], cols(h)], k_ref[0, rows[d], cols(h)], v_ref[0, rows[d], cols(h)]
        egc = jnp.exp(gcol)
        kb = kh * beta
        khb = jnp.concatenate([kh.astype(BF16), jnp.zeros((C, dk), BF16)], axis=0)
        a_l.append(_bdot_nt(jnp.concatenate([kb, qh], axis=0), khb))
        dec_l.append(jnp.exp(jnp.where(incl[d], gcol - gct[r:r + 1, :], NEG_INF)))
        v_b.append(vh * beta)
        kb_l.append(kb)
        egc_l.append(egc)
        qd_bf.append((qh * egc).astype(BF16))
        kd_bf.append((kh * jnp.exp(glast[d][:, r:r + 1] - gcol)).astype(BF16))
    r_l = [eye_right - jnp.where(strict[d], a[:C] * dec, 0.0)
           for (d, h), a, dec in zip(units, a_l, dec_l)]
    intra_bf = [(a[C:] * dec)[:, :C].astype(BF16) for a, dec in zip(a_l, dec_l)]
    x_l = [r[:, :C] for r in r_l]
    n = 1
    while n < C:
        o_l = [_bdot(x, r) for x, r in zip(x_l, r_l)]
        r_l = [o + jnp.where(left, 0.0, r) for o, r in zip(o_l, r_l)]
        x_l = [o[:, :C] for o in o_l]
        n *= 2
    zpad = jnp.zeros((C, 2 * dk), BF16)
    sol_l = [_bdot(r, jnp.concatenate([zpad, jnp.concatenate([vb, kb * egc], axis=1).astype(BF16)], axis=0))
             for r, vb, kb, egc in zip(r_l, v_b, kb_l, egc_l)]
    st_l = [s_refs[d][h] for d, h in units]
    wq_l = [_bdot(jnp.concatenate([sol[:, dk:].astype(BF16), qd], axis=0), st)
            for sol, qd, st in zip(sol_l, qd_bf, st_l)]
    vn_l = [sol[:, :dk] - wq[:C] for sol, wq in zip(sol_l, wq_l)]
    for (d, h), wq, intra, vn in zip(units, wq_l, intra_bf, vn_l):
        o_refs[d][rows[d], cols(h)] = wq[C:] + _bdot(intra, vn)
    for (d, h), st, kd, vn in zip(units, st_l, kd_bf, vn_l):
        r = lane_of(d, h)
        s_refs[d][h] = st * jnp.exp(glast[d][:, r:r + 1]) + _bdot_tn(kd, vn)


def _gdn_scan(qkv, gb, batch, seq, rows_per_step):
    _, T, D = qkv.shape
    chunk = rows_per_step
    nc = seq // chunk
    fwd = lambda b, c: b * nc + c
    bwd = lambda b, c: b * nc + (nc - 1 - c)
    part = lambda p, f: pl.BlockSpec((1, chunk, D), lambda b, c: (p, f(b, c), 0))
    gspec = lambda f: pl.BlockSpec((chunk, ROUTER_LANES), lambda b, c: (f(b, c), 0))
    ospec = lambda f: pl.BlockSpec((chunk, D), lambda b, c: (f(b, c), 0))
    return pl.pallas_call(
        _gdn_scan_kernel,
        grid=(batch, nc),
        in_specs=[part(0, fwd), part(1, fwd), part(2, fwd), gspec(fwd),
                  part(0, bwd), part(1, bwd), part(2, bwd), gspec(bwd)],
        out_specs=[ospec(fwd), ospec(bwd)],
        out_shape=[jax.ShapeDtypeStruct((T, D), F32), jax.ShapeDtypeStruct((T, D), F32)],
        scratch_shapes=[pltpu.VMEM((B_HEADS, B_HEAD_DIM, B_HEAD_DIM), F32),
                        pltpu.VMEM((B_HEADS, B_HEAD_DIM, B_HEAD_DIM), F32)],
        compiler_params=_cparams("parallel", "arbitrary"),
        name="gdn_scan",
    )(qkv, qkv, qkv, gb, qkv, qkv, qkv, gb)


def _gdn_out_kernel(of_ref, ob_ref, z_ref, og_ref, w_ref, x_ref, o_ref, a_ref):
    @pl.when(pl.program_id(1) == 0)
    def _():
        o = of_ref[...] + ob_ref[...]
        z = z_ref[...].astype(F32)
        for h in range(B_HEADS):
            sl = slice(h * B_HEAD_DIM, (h + 1) * B_HEAD_DIM)
            zh = z[:, sl]
            a_ref[:, sl] = (_rms(o[:, sl], og_ref[...]) * (zh * jax.nn.sigmoid(zh))).astype(BF16)

    o_ref[...] = x_ref[...] + jnp.dot(a_ref[...], w_ref[...], preferred_element_type=F32)


def _gdn_out(o_f, o_b, qkvz, o_gain, w, x, tm, tn):
    T, D = x.shape
    return pl.pallas_call(
        _gdn_out_kernel,
        grid=(T // tm, D // tn),
        in_specs=[pl.BlockSpec((tm, D), lambda i, j: (i, 0)),
                  pl.BlockSpec((tm, D), lambda i, j: (i, 0)),
                  pl.BlockSpec((tm, D), lambda i, j: (i, 3)),
                  pl.BlockSpec((1, B_HEAD_DIM), lambda i, j: (0, 0)),
                  pl.BlockSpec((D, tn), lambda i, j: (0, j)),
                  pl.BlockSpec((tm, tn), lambda i, j: (i, j))],
        out_specs=pl.BlockSpec((tm, tn), lambda i, j: (i, j)),
        out_shape=jax.ShapeDtypeStruct((T, D), F32),
        scratch_shapes=[pltpu.VMEM((tm, D), BF16)],
        compiler_params=_cparams("parallel", "arbitrary"),
        name="gdn_out",
    )(o_f, o_b, qkvz, o_gain.reshape(1, B_HEAD_DIM), w, x)


def _attention_layer(x, gain, w_in, q_gain, k_gain, sink, w_out, batch, seq):
    qkv = _norm_matmul(x, gain, w_in.astype(BF16), tm=1024, tn=768)
    a = _attention(qkv, q_gain, k_gain, sink, batch, seq)
    return _matmul_residual(a, w_out.astype(BF16), x, tm=1024, tn=1024)


def _gdn_layer(x, gain, w_in, conv_w, a_log, dt_bias, o_gain, w_out, batch, seq):
    D = x.shape[1]
    w_main = w_in[:, :4 * D].astype(BF16)
    pad = ROUTER_LANES - 4 * B_HEADS
    w_ab = jnp.concatenate([w_in[:, 4 * D:], jnp.zeros((D, pad), F32)], axis=1)
    qkvz, ab = _norm_matmul2(x, gain, w_main, w_ab, tm=1024, tn=1024)
    qkv, gb = _gdn_conv(qkvz, ab, conv_w, a_log, dt_bias, seq, tm=512)
    o_f, o_b = _gdn_scan(qkv, gb, batch, seq, 4 * GDN_CHUNK)
    return _gdn_out(o_f, o_b, qkvz, o_gain, w_out.astype(BF16), x, tm=512, tn=1024)


def kernel(x, norm_mix, norm_ffn, attn_w_in, attn_q_gain, attn_k_gain, attn_sink, attn_w_out, gdn_w_in, gdn_conv, gdn_a_log, gdn_dt_bias, gdn_o_gain, gdn_w_out, moe_w_group, moe_b_group, moe_w_expert, moe_b_expert, moe_w_gate, moe_w_up, moe_w_down):
    batch, seq, d_model = x.shape
    depth = norm_mix.shape[0]
    xt = x.reshape(batch * seq, d_model)
    for i in range(depth):
        j = i // 2
        if i % 2 == 0:
            xt = _attention_layer(xt, norm_mix[i], attn_w_in[j], attn_q_gain[j], attn_k_gain[j],
                                  attn_sink[j], attn_w_out[j], batch, seq)
        else:
            xt = _gdn_layer(xt, norm_mix[i], gdn_w_in[j], gdn_conv[j], gdn_a_log[j], gdn_dt_bias[j],
                            gdn_o_gain[j], gdn_w_out[j], batch, seq)
        xt = _moe(xt, norm_ffn[i], moe_w_group[i], moe_b_group[i], moe_w_expert[i], moe_b_expert[i],
                  moe_w_gate, moe_w_up, moe_w_down, layer=i)
    return xt.reshape(batch, seq, d_model)
```

```python
import functools
import math

import jax
import jax.numpy as jnp
import numpy as np
from jax import lax
from jax.experimental import pallas as pl
from jax.experimental.pallas import tpu as pltpu

RMS_EPS = 1e-6
NEG_INF = -1e30
F32 = jnp.float32
BF16 = jnp.bfloat16

A_HEADS = 16
A_KV_HEADS = 4
A_HEAD_DIM = 64
A_REP = A_HEADS // A_KV_HEADS
A_BLOCK = 128
B_HEADS = 8
B_HEAD_DIM = 128
B_CONV = 4
GDN_CHUNK = 64
N_GROUPS = 4
EXPERTS_PER_GROUP = 8
N_EXPERTS = 32
D_EXPERT = 256
ROUTER_LANES = 128

V7X_VMEM_LIMIT_BYTES = 56 * 1024 * 1024

QKV_PROJ_TILE = (1024, 768)
ATTN_OUT_TILE = (1024, 1024)
GDN_PROJ_TILE = (1024, 1024)
GDN_CONV_ROWS = 512
GDN_SCAN_CHUNKS = 4
GDN_OUT_TILE = (512, 1024)
ROUTER_ROWS = 512
DISPATCH_ROWS = 1024
COMBINE_ROWS = 512


def _cparams(*sem):
    return pltpu.CompilerParams(dimension_semantics=sem, vmem_limit_bytes=V7X_VMEM_LIMIT_BYTES)


def _bdot(a, b):
    return jnp.dot(a.astype(BF16), b.astype(BF16), preferred_element_type=F32)


def _bdot_nt(a, b):
    return lax.dot_general(a.astype(BF16), b.astype(BF16), (((1,), (1,)), ((), ())),
                           preferred_element_type=F32)


def _bdot_tn(a, b):
    return lax.dot_general(a.astype(BF16), b.astype(BF16), (((0,), (0,)), ((), ())),
                           preferred_element_type=F32)


def _split_bf16(a):
    hi = a.astype(BF16)
    lo = (a - hi.astype(F32)).astype(BF16)
    return hi, lo


def _rms(x, gain):
    return x * lax.rsqrt(jnp.mean(x * x, axis=-1, keepdims=True) + RMS_EPS) * gain


def _norm_matmul_kernel(x_ref, g_ref, w_ref, o_ref, xn_ref):
    @pl.when(pl.program_id(1) == 0)
    def _():
        xn_ref[...] = _rms(x_ref[...], g_ref[...]).astype(BF16)

    o_ref[...] = jnp.dot(xn_ref[...], w_ref[...], preferred_element_type=F32).astype(o_ref.dtype)


def _norm_matmul(x, gain, w, tm, tn):
    T, D = x.shape
    N = w.shape[1]
    return pl.pallas_call(
        _norm_matmul_kernel,
        grid=(T // tm, N // tn),
        in_specs=[pl.BlockSpec((tm, D), lambda i, j: (i, 0)),
                  pl.BlockSpec((1, D), lambda i, j: (0, 0)),
                  pl.BlockSpec((D, tn), lambda i, j: (0, j))],
        out_specs=pl.BlockSpec((tm, tn), lambda i, j: (i, j)),
        out_shape=jax.ShapeDtypeStruct((T, N), BF16),
        scratch_shapes=[pltpu.VMEM((tm, D), BF16)],
        compiler_params=_cparams("parallel", "arbitrary"),
        name="norm_matmul",
    )(x, gain.reshape(1, D), w)


def _norm_matmul2_kernel(x_ref, g_ref, w_ref, w2_ref, o_ref, o2_ref, xn_ref):
    @pl.when(pl.program_id(1) == 0)
    def _():
        xn = _rms(x_ref[...], g_ref[...])
        xn_ref[...] = xn.astype(BF16)
        x_hi, x_lo = _split_bf16(xn)
        w_hi, w_lo = _split_bf16(w2_ref[...])
        o2_ref[...] = (jnp.dot(x_hi, w_hi, preferred_element_type=F32)
                       + jnp.dot(x_lo, w_hi, preferred_element_type=F32)
                       + jnp.dot(x_hi, w_lo, preferred_element_type=F32))

    o_ref[...] = jnp.dot(xn_ref[...], w_ref[...], preferred_element_type=F32).astype(o_ref.dtype)


def _norm_matmul2(x, gain, w, w2, tm, tn):
    T, D = x.shape
    N = w.shape[1]
    N2 = w2.shape[1]
    return pl.pallas_call(
        _norm_matmul2_kernel,
        grid=(T // tm, N // tn),
        in_specs=[pl.BlockSpec((tm, D), lambda i, j: (i, 0)),
                  pl.BlockSpec((1, D), lambda i, j: (0, 0)),
                  pl.BlockSpec((D, tn), lambda i, j: (0, j)),
                  pl.BlockSpec((D, N2), lambda i, j: (0, 0))],
        out_specs=[pl.BlockSpec((tm, tn), lambda i, j: (i, j)),
                   pl.BlockSpec((tm, N2), lambda i, j: (i, 0))],
        out_shape=[jax.ShapeDtypeStruct((T, N), BF16), jax.ShapeDtypeStruct((T, N2), F32)],
        scratch_shapes=[pltpu.VMEM((tm, D), BF16)],
        compiler_params=_cparams("parallel", "arbitrary"),
        name="norm_matmul2",
    )(x, gain.reshape(1, D), w, w2)


def _matmul_residual_kernel(a_ref, w_ref, x_ref, o_ref):
    o_ref[...] = x_ref[...] + jnp.dot(a_ref[...], w_ref[...], preferred_element_type=F32)


def _matmul_residual(a, w, x, tm, tn):
    T, K = a.shape
    N = w.shape[1]
    return pl.pallas_call(
        _matmul_residual_kernel,
        grid=(T // tm, N // tn),
        in_specs=[pl.BlockSpec((tm, K), lambda i, j: (i, 0)),
                  pl.BlockSpec((K, tn), lambda i, j: (0, j)),
                  pl.BlockSpec((tm, tn), lambda i, j: (i, j))],
        out_specs=pl.BlockSpec((tm, tn), lambda i, j: (i, j)),
        out_shape=jax.ShapeDtypeStruct((T, N), F32),
        compiler_params=_cparams("parallel", "arbitrary"),
        name="matmul_residual",
    )(a, w, x)


def _attn_kernel(main_ref, prev_ref, next_ref, qg_ref, kg_ref, sink_ref, bias_ref, o_ref):
    n = pl.program_id(1)
    nb = pl.num_programs(1)
    dh, blk = A_HEAD_DIM, A_BLOCK
    kv_cols = A_KV_HEADS * dh
    main = main_ref[...].astype(F32)
    kv = jnp.concatenate([prev_ref[...].astype(F32), main[:, A_HEADS * dh:], next_ref[...].astype(F32)],
                         axis=0)
    col = lax.broadcasted_iota(jnp.int32, (1, 3 * blk), 1)
    outside = ((col < blk) & (n == 0)) | ((col >= 2 * blk) & (n == nb - 1))
    edge = jnp.where(outside, NEG_INF, 0.0)
    qg = qg_ref[...] * (dh ** -0.5)
    groups = range(A_KV_HEADS)
    kn = [_rms(kv[:, g * dh:(g + 1) * dh], kg_ref[...]).astype(BF16) for g in groups]
    q4 = [jnp.concatenate([_rms(main[:, (A_REP * g + r) * dh:(A_REP * g + r + 1) * dh], qg).astype(BF16)
                           for r in range(A_REP)], axis=0) for g in groups]
    s = [_bdot_nt(q4[g], kn[g]) + bias_ref[g] + edge for g in groups]
    m = [jnp.maximum(jnp.max(s[g], axis=-1, keepdims=True), sink_ref[g]) for g in groups]
    p = [jnp.exp(s[g] - m[g]) for g in groups]
    denom = [jnp.sum(p[g], axis=-1, keepdims=True) + jnp.exp(sink_ref[g] - m[g]) for g in groups]
    o = [jnp.dot(p[g].astype(BF16), kv[:, kv_cols + g * dh: kv_cols + (g + 1) * dh].astype(BF16),
                 preferred_element_type=F32) / denom[g] for g in groups]
    for g in groups:
        for r in range(A_REP):
            h = A_REP * g + r
            o_ref[:, h * dh:(h + 1) * dh] = o[g][r * blk:(r + 1) * blk].astype(o_ref.dtype)


def _attn_tables(sink):
    blk = A_BLOCK
    slopes = np.array([2.0 ** (-8.0 * (h + 1) / A_HEADS) for h in range(A_HEADS)], np.float32)
    qi = np.arange(blk)
    kj = np.arange(3 * blk)
    dist = np.abs(blk + qi[:, None] - kj[None, :]).astype(np.float32)
    bias = np.where(dist[None] <= blk, -slopes[:, None, None] * dist[None], np.float32(NEG_INF))
    bias = bias.astype(np.float32).reshape(A_KV_HEADS, A_REP * blk, 3 * blk)
    sink_rows = jnp.repeat(sink.astype(F32).reshape(A_KV_HEADS, A_REP), blk, axis=1)[..., None]
    return jnp.asarray(bias), sink_rows


def _attention(qkv, q_gain, k_gain, sink, batch, seq):
    T, W = qkv.shape
    blk, dh = A_BLOCK, A_HEAD_DIM
    nb = seq // blk
    kv_w = 2 * A_KV_HEADS * dh
    kv_blk = (A_HEADS * dh) // kv_w
    bias, sink_rows = _attn_tables(sink)
    return pl.pallas_call(
        _attn_kernel,
        grid=(batch, nb),
        in_specs=[
            pl.BlockSpec((blk, W), lambda b, n: (b * nb + n, 0)),
            pl.BlockSpec((blk, kv_w), lambda b, n: (b * nb + jnp.maximum(n - 1, 0), kv_blk)),
            pl.BlockSpec((blk, kv_w), lambda b, n: (b * nb + jnp.minimum(n + 1, nb - 1), kv_blk)),
            pl.BlockSpec((1, dh), lambda b, n: (0, 0)),
            pl.BlockSpec((1, dh), lambda b, n: (0, 0)),
            pl.BlockSpec((A_KV_HEADS, A_REP * blk, 1), lambda b, n: (0, 0, 0)),
            pl.BlockSpec((A_KV_HEADS, A_REP * blk, 3 * blk), lambda b, n: (0, 0, 0)),
        ],
        out_specs=pl.BlockSpec((blk, A_HEADS * dh), lambda b, n: (b * nb + n, 0)),
        out_shape=jax.ShapeDtypeStruct((T, A_HEADS * dh), BF16),
        compiler_params=_cparams("parallel", "parallel"),
        name="window_attention",
    )(qkv, qkv, qkv, q_gain.reshape(1, dh), k_gain.reshape(1, dh), sink_rows, bias)


def _route(logits):
    lane = lax.broadcasted_iota(jnp.int32, logits.shape, 1).astype(F32)
    big = jnp.float32(1e9)
    is_g = (lane >= N_EXPERTS) & (lane < N_EXPERTS + N_GROUPS)
    lg = jnp.where(is_g, logits, NEG_INF)
    gmax = jnp.max(lg, axis=-1, keepdims=True)
    gidx = jnp.min(jnp.where(is_g & (lg == gmax), lane, big), axis=-1, keepdims=True) - N_EXPERTS
    g_prob = 1.0 / jnp.sum(jnp.where(is_g, jnp.exp(lg - gmax), 0.0), axis=-1, keepdims=True)
    lo = gidx * EXPERTS_PER_GROUP
    in_grp = (lane >= lo) & (lane < lo + EXPERTS_PER_GROUP)
    le = jnp.where(in_grp, logits, NEG_INF)
    emax = jnp.max(le, axis=-1, keepdims=True)
    ex = jnp.where(in_grp, jnp.exp(le - emax), 0.0)
    prob = ex / jnp.sum(ex, axis=-1, keepdims=True)
    cand = jnp.where(in_grp, prob, -1.0)
    p1 = jnp.max(cand, axis=-1, keepdims=True)
    i1 = jnp.min(jnp.where(cand == p1, lane, big), axis=-1, keepdims=True)
    cand2 = jnp.where(lane == i1, -1.0, cand)
    p2 = jnp.max(cand2, axis=-1, keepdims=True)
    i2 = jnp.min(jnp.where(cand2 == p2, lane, big), axis=-1, keepdims=True)
    scale = g_prob / (p1 + p2)
    return i1, i2, p1 * scale, p2 * scale


RT_E1, RT_E2, RT_G1, RT_G2, RT_R1, RT_R2 = range(6)


def _pack_bf16_pairs(a):
    n = a.shape[1] // 2
    hi = lax.bitcast_convert_type(a[:, :n].astype(BF16).astype(F32), jnp.uint32)
    lo = lax.bitcast_convert_type(a[:, n:].astype(BF16).astype(F32), jnp.uint32)
    return hi | (lo >> 16)


def _unpack_bf16_pairs(p):
    hi = lax.bitcast_convert_type(p & jnp.uint32(0xFFFF0000), F32)
    lo = lax.bitcast_convert_type(p << 16, F32)
    return jnp.concatenate([hi, lo], axis=1)


LANES = 128
D_MODEL = 1024
TOKEN_ROWS = D_MODEL // 2 // LANES


def _store_token_major(ref, packed):
    m, w = packed.shape
    s_per = w // LANES
    for s in range(s_per):
        ref[pl.ds(s, m, stride=s_per), :] = packed[:, s * LANES:(s + 1) * LANES]


def _load_token_major(ref, m, s_per):
    return jnp.concatenate([ref[pl.ds(s, m, stride=s_per), :] for s in range(s_per)], axis=1)


def _router_kernel(x_ref, g_ref, wr_ref, br_ref, h_ref, route_ref, counts_ref, carry_ref):
    @pl.when(pl.program_id(0) == 0)
    def _():
        carry_ref[...] = jnp.zeros_like(carry_ref)

    hn = _rms(x_ref[...], g_ref[...])
    _store_token_major(h_ref, _pack_bf16_pairs(hn))
    h_hi, h_lo = _split_bf16(hn)
    w_hi, w_lo = _split_bf16(wr_ref[...])
    logits = (jnp.dot(h_hi, w_hi, preferred_element_type=F32)
              + jnp.dot(h_lo, w_hi, preferred_element_type=F32)
              + jnp.dot(h_hi, w_lo, preferred_element_type=F32)) + br_ref[...]
    i1, i2, g1, g2 = _route(logits)
    tm = logits.shape[0]
    lane = lax.broadcasted_iota(jnp.int32, logits.shape, 1).astype(F32)
    chosen = ((lane == i1) | (lane == i2)).astype(BF16)
    earlier = (lax.broadcasted_iota(jnp.int32, (tm, tm), 1)
               < lax.broadcasted_iota(jnp.int32, (tm, tm), 0)).astype(BF16)
    before = jnp.dot(earlier, chosen, preferred_element_type=F32) + carry_ref[...]
    r1 = jnp.sum(jnp.where(lane == i1, before, 0.0), axis=-1, keepdims=True)
    r2 = jnp.sum(jnp.where(lane == i2, before, 0.0), axis=-1, keepdims=True)
    carry_ref[...] += jnp.sum(chosen.astype(F32), axis=0, keepdims=True)
    rec = jnp.zeros_like(logits)
    for slot, val in ((RT_E1, i1), (RT_E2, i2), (RT_G1, g1), (RT_G2, g2), (RT_R1, r1), (RT_R2, r2)):
        rec = jnp.where(lane == slot, val, rec)
    route_ref[...] = rec
    counts_ref[...] = carry_ref[...]


def _router(x, gain, w_group, b_group, w_expert, b_expert, tm):
    T, D = x.shape
    pad = ROUTER_LANES - N_EXPERTS - N_GROUPS
    wr = jnp.concatenate([w_expert, w_group, jnp.zeros((D, pad), F32)], axis=1)
    br = jnp.concatenate([b_expert, b_group, jnp.zeros((pad,), F32)]).reshape(1, ROUTER_LANES)
    return pl.pallas_call(
        _router_kernel,
        grid=(T // tm,),
        in_specs=[pl.BlockSpec((tm, D), lambda i: (i, 0)),
                  pl.BlockSpec((1, D), lambda i: (0, 0)),
                  pl.BlockSpec((D, ROUTER_LANES), lambda i: (0, 0)),
                  pl.BlockSpec((1, ROUTER_LANES), lambda i: (0, 0))],
        out_specs=[pl.BlockSpec((tm * TOKEN_ROWS, LANES), lambda i: (i, 0)),
                   pl.BlockSpec((tm, ROUTER_LANES), lambda i: (i, 0)),
                   pl.BlockSpec((1, ROUTER_LANES), lambda i: (0, 0))],
        out_shape=[jax.ShapeDtypeStruct((T * TOKEN_ROWS, LANES), jnp.uint32),
                   jax.ShapeDtypeStruct((T, ROUTER_LANES), F32),
                   jax.ShapeDtypeStruct((1, ROUTER_LANES), F32)],
        scratch_shapes=[pltpu.VMEM((1, ROUTER_LANES), F32)],
        compiler_params=_cparams("arbitrary"),
        name="moe_router",
    )(x, gain.reshape(1, D), wr, br)


MOE_TILE = 512
MOE_SLOTS = 2
DMA_UNROLL = 8


def _token_copy(src, src_row, dst, dst_row, sem):
    return pltpu.make_async_copy(src.at[pl.ds(pl.multiple_of(src_row, TOKEN_ROWS), TOKEN_ROWS)],
                                 dst.at[pl.ds(pl.multiple_of(dst_row, TOKEN_ROWS), TOKEN_ROWS)], sem)


def _dispatch_kernel(pos_ref, h_ref, zeros_hbm, xs_hbm, sem):
    del zeros_hbm
    tm = h_ref.shape[0] // TOKEN_ROWS

    def start(g, c):
        r0 = pl.multiple_of(g * DMA_UNROLL, DMA_UNROLL)
        for j in range(DMA_UNROLL):
            for k in range(MOE_SLOTS):
                _token_copy(h_ref, (r0 + j) * TOKEN_ROWS, xs_hbm, pos_ref[MOE_SLOTS * (r0 + j) + k],
                            sem).start(priority=k)
        return c

    def wait(r, c):
        for k in range(MOE_SLOTS):
            _token_copy(h_ref, 0, xs_hbm, 0, sem).wait()
        return c

    lax.fori_loop(0, tm // DMA_UNROLL, start, 0)
    lax.fori_loop(0, tm, wait, 0, unroll=DMA_UNROLL)


def _dispatch(h, pos, n_rows, tm):
    T = h.shape[0] // TOKEN_ROWS
    zeros = jnp.zeros((n_rows * TOKEN_ROWS, LANES), h.dtype)
    return pl.pallas_call(
        _dispatch_kernel,
        grid=(T // tm,),
        in_specs=[pl.BlockSpec((MOE_SLOTS * tm,), lambda i: (i,), memory_space=pltpu.SMEM),
                  pl.BlockSpec((tm * TOKEN_ROWS, LANES), lambda i: (i, 0)),
                  pl.BlockSpec(memory_space=pl.ANY)],
        out_specs=pl.BlockSpec(memory_space=pl.ANY),
        out_shape=jax.ShapeDtypeStruct(zeros.shape, h.dtype),
        scratch_shapes=[pltpu.SemaphoreType.DMA(())],
        input_output_aliases={2: 0},
        compiler_params=_cparams("arbitrary"),
        name="moe_dispatch",
    )(pos, h, zeros)


def _expert_kernel(te_ref, nv_ref, xs_ref, wg_ref, wu_ref, wd_ref, ys_ref, wgu_s, wd_s):
    i = pl.program_id(0)
    nv = nv_ref[0]
    valid = i < nv
    ic = jnp.minimum(i, nv - 1)
    changed = (i == 0) | (te_ref[ic] != te_ref[jnp.maximum(ic - 1, 0)])

    @pl.when(valid & changed)
    def _():
        wgu_s[:, :D_EXPERT] = wg_ref[0, 0].astype(BF16)
        wgu_s[:, D_EXPERT:] = wu_ref[0, 0].astype(BF16)
        wd_s[...] = wd_ref[0, 0].astype(BF16)

    @pl.when(valid)
    def _():
        x = _unpack_bf16_pairs(_load_token_major(xs_ref, MOE_TILE, TOKEN_ROWS)).astype(BF16)
        gu = jnp.dot(x, wgu_s[...], preferred_element_type=F32)
        gate, up = gu[:, :D_EXPERT], gu[:, D_EXPERT:]
        hid = (gate * jax.nn.sigmoid(gate)) * up
        y = jnp.dot(hid.astype(BF16), wd_s[...], preferred_element_type=F32)
        _store_token_major(ys_ref, _pack_bf16_pairs(y))

    @pl.when(jnp.logical_not(valid))
    def _():
        ys_ref[...] = jnp.zeros_like(ys_ref)


def _experts(xs, tile_expert, n_valid, w_gate, w_up, w_down, layer):
    D = TOKEN_ROWS * LANES * 2
    blk = MOE_TILE * TOKEN_ROWS
    n_tiles = xs.shape[0] // blk
    row = lambda i, te, nv: (jnp.minimum(i, nv[0] - 1), 0)
    wsel = lambda i, te, nv: (layer, te[jnp.minimum(i, nv[0] - 1)], 0, 0)
    return pl.pallas_call(
        _expert_kernel,
        grid_spec=pltpu.PrefetchScalarGridSpec(
            num_scalar_prefetch=2,
            grid=(n_tiles,),
            in_specs=[pl.BlockSpec((blk, LANES), row),
                      pl.BlockSpec((1, 1, D, D_EXPERT), wsel),
                      pl.BlockSpec((1, 1, D, D_EXPERT), wsel),
                      pl.BlockSpec((1, 1, D_EXPERT, D), wsel)],
            out_specs=pl.BlockSpec((blk, LANES), lambda i, te, nv: (i, 0)),
            scratch_shapes=[pltpu.VMEM((D, 2 * D_EXPERT), BF16), pltpu.VMEM((D_EXPERT, D), BF16)]),
        out_shape=jax.ShapeDtypeStruct(xs.shape, jnp.uint32),
        compiler_params=_cparams("arbitrary"),
        name="moe_experts",
    )(tile_expert, n_valid, xs, w_gate, w_up, w_down)


def _combine_kernel(pos_ref, pos_next_ref, x_ref, route_ref, ys_hbm, o_ref, buf, sems):
    i = pl.program_id(0)
    tm = x_ref.shape[0]

    def gather(p_ref, ring):
        def start(g, c):
            r0 = pl.multiple_of(g * DMA_UNROLL, DMA_UNROLL)
            for j in range(DMA_UNROLL):
                for k in range(MOE_SLOTS):
                    _token_copy(ys_hbm, p_ref[MOE_SLOTS * (r0 + j) + k], buf.at[ring, k], (r0 + j) * TOKEN_ROWS,
                                sems.at[ring]).start(priority=k)
            return c
        lax.fori_loop(0, tm // DMA_UNROLL, start, 0)

    @pl.when(i == 0)
    def _():
        gather(pos_ref, 0)

    @pl.when(i + 1 < pl.num_programs(0))
    def _():
        gather(pos_next_ref, (i + 1) % 2)

    ring = i % 2

    def wait(r, c):
        for k in range(MOE_SLOTS):
            _token_copy(ys_hbm, 0, buf.at[ring, k], 0, sems.at[ring]).wait()
        return c

    lax.fori_loop(0, tm, wait, 0, unroll=DMA_UNROLL)
    rec = route_ref[...]
    y1 = _unpack_bf16_pairs(_load_token_major(buf.at[ring, 0], tm, TOKEN_ROWS))
    y2 = _unpack_bf16_pairs(_load_token_major(buf.at[ring, 1], tm, TOKEN_ROWS))
    o_ref[...] = x_ref[...] + rec[:, RT_G1:RT_G1 + 1] * y1 + rec[:, RT_G2:RT_G2 + 1] * y2


def _combine(x, route, pos, ys, tm):
    T, D = x.shape
    n = T // tm
    return pl.pallas_call(
        _combine_kernel,
        grid=(n,),
        in_specs=[pl.BlockSpec((MOE_SLOTS * tm,), lambda i: (i,), memory_space=pltpu.SMEM),
                  pl.BlockSpec((MOE_SLOTS * tm,), lambda i: (jnp.minimum(i + 1, n - 1),), memory_space=pltpu.SMEM),
                  pl.BlockSpec((tm, D), lambda i: (i, 0)),
                  pl.BlockSpec((tm, ROUTER_LANES), lambda i: (i, 0)),
                  pl.BlockSpec(memory_space=pl.ANY)],
        out_specs=pl.BlockSpec((tm, D), lambda i: (i, 0)),
        out_shape=jax.ShapeDtypeStruct((T, D), F32),
        scratch_shapes=[pltpu.VMEM((2, MOE_SLOTS, tm * TOKEN_ROWS, LANES), ys.dtype),
                        pltpu.SemaphoreType.DMA((2,))],
        compiler_params=_cparams("arbitrary"),
        name="moe_combine",
    )(pos, pos, x, route, ys)


def _moe(x, gain, w_group, b_group, w_expert, b_expert, w_gate, w_up, w_down, layer):
    T, D = x.shape
    n_tiles = (MOE_SLOTS * T + N_EXPERTS * (MOE_TILE - 1)) // MOE_TILE + 1
    h, route, counts = _router(x, gain, w_group, b_group, w_expert, b_expert, tm=ROUTER_ROWS)
    counts = counts[0, :N_EXPERTS].astype(jnp.int32)
    padded = (counts + MOE_TILE - 1) // MOE_TILE * MOE_TILE
    ends = jnp.cumsum(padded)
    starts = ends - padded
    eid = route[:, RT_E1:RT_E2 + 1].astype(jnp.int32)
    rank = route[:, RT_R1:RT_R2 + 1].astype(jnp.int32)
    onehot = eid[..., None] == jnp.arange(N_EXPERTS, dtype=jnp.int32)
    pos = (rank + jnp.sum(jnp.where(onehot, starts, 0), axis=-1)).reshape(-1)
    pos = pos * TOKEN_ROWS
    tile_start = jnp.arange(n_tiles, dtype=jnp.int32) * MOE_TILE
    tile_expert = jnp.minimum(jnp.sum(tile_start[:, None] >= ends[None, :], axis=-1), N_EXPERTS - 1).astype(jnp.int32)
    n_valid = (ends[-1:] // MOE_TILE).astype(jnp.int32)
    xs = _dispatch(h, pos, n_tiles * MOE_TILE, tm=DISPATCH_ROWS)
    ys = _experts(xs, tile_expert, n_valid, w_gate, w_up, w_down, layer)
    return _combine(x, route, pos, ys, tm=COMBINE_ROWS)


GDN_HALO = 16


def _gdn_conv_kernel(u_ref, up_ref, un_ref, w_ref, ab_ref, aexp_ref, dtb_ref, o_ref, gb_ref, *, tiles_per_seq):
    i = pl.program_id(0)
    j = pl.program_id(1)
    tm = u_ref.shape[0]
    first = (i % tiles_per_seq) == 0
    last = (i % tiles_per_seq) == tiles_per_seq - 1
    u = u_ref[...].astype(F32)
    prev = jnp.where(first, 0.0, up_ref[...].astype(F32))
    nxt = jnp.where(last, 0.0, un_ref[...].astype(F32))
    ext = jnp.concatenate([prev, u, nxt], axis=0)
    w = w_ref[...]
    h0 = GDN_HALO - B_CONV // 2
    y = w[0:1] * ext[h0:h0 + tm]
    for t in range(1, B_CONV):
        y = y + w[t:t + 1] * ext[h0 + t:h0 + t + tm]
    y = y * jax.nn.sigmoid(y)
    qscale = jnp.where(j == 0, B_HEAD_DIM ** -0.5, 1.0)
    for h in range(B_HEADS):
        slab = y[:, h * B_HEAD_DIM:(h + 1) * B_HEAD_DIM]
        inv = lax.rsqrt(jnp.sum(slab * slab, axis=-1, keepdims=True) + RMS_EPS) * qscale
        o_ref[0, :, h * B_HEAD_DIM:(h + 1) * B_HEAD_DIM] = slab * jnp.where(j < 2, inv, 1.0)

    @pl.when(j == 0)
    def _():
        ab = ab_ref[...]
        lane = lax.broadcasted_iota(jnp.int32, ab.shape, 1)
        z = ab + dtb_ref[...]
        softplus = jnp.maximum(z, 0.0) + jnp.log1p(jnp.exp(-jnp.abs(z)))
        gb_ref[...] = jnp.where(lane < 2 * B_HEADS, -aexp_ref[...] * softplus, jax.nn.sigmoid(ab))


def _gdn_conv(qkvz, ab, conv_w, a_log, dt_bias, seq, tm):
    T = qkvz.shape[0]
    D = B_HEADS * B_HEAD_DIM
    hb = tm // GDN_HALO
    n_halo = T // GDN_HALO
    pad = ROUTER_LANES - 2 * B_HEADS
    aexp = jnp.concatenate([jnp.exp(a_log.astype(F32)).reshape(-1), jnp.zeros((pad,), F32)]).reshape(1, -1)
    dtb = jnp.concatenate([dt_bias.astype(F32).reshape(-1), jnp.zeros((pad,), F32)]).reshape(1, -1)
    return pl.pallas_call(
        functools.partial(_gdn_conv_kernel, tiles_per_seq=seq // tm),
        grid=(T // tm, 3),
        in_specs=[pl.BlockSpec((tm, D), lambda i, j: (i, j)),
                  pl.BlockSpec((GDN_HALO, D), lambda i, j: (jnp.maximum(i * hb - 1, 0), j)),
                  pl.BlockSpec((GDN_HALO, D), lambda i, j: (jnp.minimum((i + 1) * hb, n_halo - 1), j)),
                  pl.BlockSpec((B_CONV, D), lambda i, j: (0, j)),
                  pl.BlockSpec((tm, ROUTER_LANES), lambda i, j: (i, 0)),
                  pl.BlockSpec((1, ROUTER_LANES), lambda i, j: (0, 0)),
                  pl.BlockSpec((1, ROUTER_LANES), lambda i, j: (0, 0))],
        out_specs=[pl.BlockSpec((1, tm, D), lambda i, j: (j, i, 0)),
                   pl.BlockSpec((tm, ROUTER_LANES), lambda i, j: (i, 0))],
        out_shape=[jax.ShapeDtypeStruct((3, T, D), F32), jax.ShapeDtypeStruct((T, ROUTER_LANES), F32)],
        compiler_params=_cparams("parallel", "arbitrary"),
        name="gdn_conv",
    )(qkvz, qkvz, qkvz, conv_w, ab, aexp, dtb)


def _gdn_gate_terms(gb, incl):
    C = gb.shape[0]
    lane = lax.broadcasted_iota(jnp.int32, gb.shape, 1)
    g_hi, g_lo = _split_bf16(jnp.where(lane < 2 * B_HEADS, gb, 0.0))
    tri = incl.astype(BF16)
    gc = jnp.dot(tri, g_hi, preferred_element_type=F32) + jnp.dot(tri, g_lo, preferred_element_type=F32)
    return gc, jnp.concatenate([gc, jnp.zeros_like(gc)], axis=0).T


def _gdn_scan_kernel(qf_ref, kf_ref, vf_ref, gf_ref, qb_ref, kb_ref, vb_ref, gbw_ref, of_ref, ob_ref, sf_ref, sb_ref):
    @pl.when(pl.program_id(1) == 0)
    def _():
        sf_ref[...] = jnp.zeros_like(sf_ref)
        sb_ref[...] = jnp.zeros_like(sb_ref)

    C = GDN_CHUNK
    n_chunks = gf_ref.shape[0] // C
    dk = B_HEAD_DIM
    row = lax.broadcasted_iota(jnp.int32, (C, C), 0)
    colm = lax.broadcasted_iota(jnp.int32, (C, C), 1)
    tri = (row >= colm, row <= colm)
    row2 = lax.broadcasted_iota(jnp.int32, (C, 2 * C), 0)
    col2 = lax.broadcasted_iota(jnp.int32, (C, 2 * C), 1)
    left = col2 < C
    incl = (left & (row2 >= col2), left & (row2 <= col2))
    strict = (left & (row2 > col2), left & (row2 < col2))
    eye_right = (col2 == row2 + C).astype(F32)
    qkv_refs = ((qf_ref, kf_ref, vf_ref), (qb_ref, kb_ref, vb_ref))
    s_refs = (sf_ref, sb_ref)
    o_refs = (of_ref, ob_ref)

    def chunk_step(c, carry):
        rows = (pl.ds(pl.multiple_of(c * C, C), C), pl.ds(pl.multiple_of((n_chunks - 1 - c) * C, C), C))
        _gdn_chunk_pair(rows, (gf_ref, gbw_ref), qkv_refs, s_refs, o_refs, tri, incl, strict, left, eye_right)
        return carry

    lax.fori_loop(0, n_chunks, chunk_step, 0)


def _gdn_chunk_pair(rows, g_refs, qkv_refs, s_refs, o_refs, tri, incl, strict, left, eye_right):
    C = GDN_CHUNK
    dk = B_HEAD_DIM
    gbs = tuple(g_refs[d][rows[d], :] for d in range(2))
    gates = [_gdn_gate_terms(gbs[d], tri[d]) for d in range(2)]
    glast = [gates[0][0][C - 1:C], gates[1][0][0:1]]
    units = [(d, h) for d in range(2) for h in range(B_HEADS)]

    def lane_of(d, h):
        return d * B_HEADS + h

    def cols(h):
        return slice(h * dk, (h + 1) * dk)

    v_b, kb_l, qd_bf, kd_bf, a_l, dec_l, egc_l = [], [], [], [], [], [], []
    for d, h in units:
        r = lane_of(d, h)
        gc, gct = gates[d]
        gcol = gc[:, r:r + 1]
        beta = gbs[d][:, 2 * B_HEADS + r:2 * B_HEADS + r + 1]
        q_ref, k_ref, v_ref = qkv_refs[d]
        qh, kh, vh = q_ref[0, rows[d], cols(h)], k_ref[0, rows[d], cols(h)], v_ref[0, rows[d], cols(h)]
        egc = jnp.exp(gcol)
        kb = kh * beta
        khb = jnp.concatenate([kh.astype(BF16), jnp.zeros((C, dk), BF16)], axis=0)
        a_l.append(_bdot_nt(jnp.concatenate([kb, qh], axis=0), khb))
        dec_l.append(jnp.exp(jnp.where(incl[d], gcol - gct[r:r + 1, :], NEG_INF)))
        v_b.append(vh * beta)
        kb_l.append(kb)
        egc_l.append(egc)
        qd_bf.append((qh * egc).astype(BF16))
        kd_bf.append((kh * jnp.exp(glast[d][:, r:r + 1] - gcol)).astype(BF16))
    r_l = [eye_right - jnp.where(strict[d], a[:C] * dec, 0.0)
           for (d, h), a, dec in zip(units, a_l, dec_l)]
    intra_bf = [(a[C:] * dec)[:, :C].astype(BF16) for a, dec in zip(a_l, dec_l)]
    x_l = [r[:, :C] for r in r_l]
    n = 1
    while n < C:
        o_l = [_bdot(x, r) for x, r in zip(x_l, r_l)]
        r_l = [o + jnp.where(left, 0.0, r) for o, r in zip(o_l, r_l)]
        x_l = [o[:, :C] for o in o_l]
        n *= 2
    zpad = jnp.zeros((C, 2 * dk), BF16)
    sol_l = [_bdot(r, jnp.concatenate([zpad, jnp.concatenate([vb, kb * egc], axis=1).astype(BF16)], axis=0))
             for r, vb, kb, egc in zip(r_l, v_b, kb_l, egc_l)]
    st_l = [s_refs[d][h] for d, h in units]
    wq_l = [_bdot(jnp.concatenate([sol[:, dk:].astype(BF16), qd], axis=0), st)
            for sol, qd, st in zip(sol_l, qd_bf, st_l)]
    vn_l = [sol[:, :dk] - wq[:C] for sol, wq in zip(sol_l, wq_l)]
    for (d, h), wq, intra, vn in zip(units, wq_l, intra_bf, vn_l):
        o_refs[d][rows[d], cols(h)] = (wq[C:] + _bdot(intra, vn)).astype(o_refs[d].dtype)
    for (d, h), st, kd, vn in zip(units, st_l, kd_bf, vn_l):
        r = lane_of(d, h)
        s_refs[d][h] = st * jnp.exp(glast[d][:, r:r + 1]) + _bdot_tn(kd, vn)


def _gdn_scan(qkv, gb, batch, seq, rows_per_step):
    _, T, D = qkv.shape
    chunk = rows_per_step
    nc = seq // chunk
    fwd = lambda b, c: b * nc + c
    bwd = lambda b, c: b * nc + (nc - 1 - c)
    part = lambda p, f: pl.BlockSpec((1, chunk, D), lambda b, c: (p, f(b, c), 0))
    gspec = lambda f: pl.BlockSpec((chunk, ROUTER_LANES), lambda b, c: (f(b, c), 0))
    ospec = lambda f: pl.BlockSpec((chunk, D), lambda b, c: (f(b, c), 0))
    return pl.pallas_call(
        _gdn_scan_kernel,
        grid=(batch, nc),
        in_specs=[part(0, fwd), part(1, fwd), part(2, fwd), gspec(fwd),
                  part(0, bwd), part(1, bwd), part(2, bwd), gspec(bwd)],
        out_specs=[ospec(fwd), ospec(bwd)],
        out_shape=[jax.ShapeDtypeStruct((T, D), BF16), jax.ShapeDtypeStruct((T, D), BF16)],
        scratch_shapes=[pltpu.VMEM((B_HEADS, B_HEAD_DIM, B_HEAD_DIM), F32),
                        pltpu.VMEM((B_HEADS, B_HEAD_DIM, B_HEAD_DIM), F32)],
        compiler_params=_cparams("parallel", "arbitrary"),
        name="gdn_scan",
    )(qkv, qkv, qkv, gb, qkv, qkv, qkv, gb)


def _gdn_out_kernel(of_ref, ob_ref, z_ref, og_ref, w_ref, x_ref, o_ref, a_ref):
    @pl.when(pl.program_id(1) == 0)
    def _():
        o = of_ref[...].astype(F32) + ob_ref[...].astype(F32)
        z = z_ref[...].astype(F32)
        for h in range(B_HEADS):
            sl = slice(h * B_HEAD_DIM, (h + 1) * B_HEAD_DIM)
            zh = z[:, sl]
            a_ref[:, sl] = (_rms(o[:, sl], og_ref[...]) * (zh * jax.nn.sigmoid(zh))).astype(BF16)

    o_ref[...] = x_ref[...] + jnp.dot(a_ref[...], w_ref[...], preferred_element_type=F32)


def _gdn_out(o_f, o_b, qkvz, o_gain, w, x, tm, tn):
    T, D = x.shape
    return pl.pallas_call(
        _gdn_out_kernel,
        grid=(T // tm, D // tn),
        in_specs=[pl.BlockSpec((tm, D), lambda i, j: (i, 0)),
                  pl.BlockSpec((tm, D), lambda i, j: (i, 0)),
                  pl.BlockSpec((tm, D), lambda i, j: (i, 3)),
                  pl.BlockSpec((1, B_HEAD_DIM), lambda i, j: (0, 0)),
                  pl.BlockSpec((D, tn), lambda i, j: (0, j)),
                  pl.BlockSpec((tm, tn), lambda i, j: (i, j))],
        out_specs=pl.BlockSpec((tm, tn), lambda i, j: (i, j)),
        out_shape=jax.ShapeDtypeStruct((T, D), F32),
        scratch_shapes=[pltpu.VMEM((tm, D), BF16)],
        compiler_params=_cparams("parallel", "arbitrary"),
        name="gdn_out",
    )(o_f, o_b, qkvz, o_gain.reshape(1, B_HEAD_DIM), w, x)


def _attention_layer(x, gain, w_in, q_gain, k_gain, sink, w_out, batch, seq):
    qkv = _norm_matmul(x, gain, w_in.astype(BF16), *QKV_PROJ_TILE)
    a = _attention(qkv, q_gain, k_gain, sink, batch, seq)
    return _matmul_residual(a, w_out.astype(BF16), x, *ATTN_OUT_TILE)


def _gdn_layer(x, gain, w_in, conv_w, a_log, dt_bias, o_gain, w_out, batch, seq):
    D = x.shape[1]
    w_main = w_in[:, :4 * D].astype(BF16)
    pad = ROUTER_LANES - 4 * B_HEADS
    w_ab = jnp.concatenate([w_in[:, 4 * D:], jnp.zeros((D, pad), F32)], axis=1)
    qkvz, ab = _norm_matmul2(x, gain, w_main, w_ab, *GDN_PROJ_TILE)
    qkv, gb = _gdn_conv(qkvz, ab, conv_w, a_log, dt_bias, seq, GDN_CONV_ROWS)
    o_f, o_b = _gdn_scan(qkv, gb, batch, seq, GDN_SCAN_CHUNKS * GDN_CHUNK)
    return _gdn_out(o_f, o_b, qkvz, o_gain, w_out.astype(BF16), x, *GDN_OUT_TILE)


def kernel(x, norm_mix, norm_ffn, attn_w_in, attn_q_gain, attn_k_gain, attn_sink, attn_w_out, gdn_w_in, gdn_conv, gdn_a_log, gdn_dt_bias, gdn_o_gain, gdn_w_out, moe_w_group, moe_b_group, moe_w_expert, moe_b_expert, moe_w_gate, moe_w_up, moe_w_down):
    batch, seq, d_model = x.shape
    depth = norm_mix.shape[0]
    xt = x.reshape(batch * seq, d_model)
    for i in range(depth):
        j = i // 2
        if i % 2 == 0:
            xt = _attention_layer(xt, norm_mix[i], attn_w_in[j], attn_q_gain[j], attn_k_gain[j],
                                  attn_sink[j], attn_w_out[j], batch, seq)
        else:
            xt = _gdn_layer(xt, norm_mix[i], gdn_w_in[j], gdn_conv[j], gdn_a_log[j], gdn_dt_bias[j],
                            gdn_o_gain[j], gdn_w_out[j], batch, seq)
        xt = _moe(xt, norm_ffn[i], moe_w_group[i], moe_b_group[i], moe_w_expert[i], moe_b_expert[i],
                  moe_w_gate, moe_w_up, moe_w_down, layer=i)
    return xt.reshape(batch, seq, d_model)
```

```python
import functools
import math

import jax
import jax.numpy as jnp
import numpy as np
from jax import lax
from jax.experimental import pallas as pl
from jax.experimental.pallas import tpu as pltpu

RMS_EPS = 1e-6
NEG_INF = -1e30
F32 = jnp.float32
BF16 = jnp.bfloat16

A_HEADS = 16
A_KV_HEADS = 4
A_HEAD_DIM = 64
A_REP = A_HEADS // A_KV_HEADS
A_BLOCK = 128
B_HEADS = 8
B_HEAD_DIM = 128
B_CONV = 4
GDN_CHUNK = 64
N_GROUPS = 4
EXPERTS_PER_GROUP = 8
N_EXPERTS = 32
D_EXPERT = 256
ROUTER_LANES = 128

V7X_VMEM_LIMIT_BYTES = 56 * 1024 * 1024

QKV_PROJ_TILE = (1024, 768)
ATTN_QBLOCKS = 4
ATTN_OUT_TILE = (1024, 1024)
GDN_PROJ_TILE = (1024, 1024)
GDN_CONV_ROWS = 512
GDN_SCAN_CHUNKS = 4
GDN_OUT_TILE = (512, 1024)
ROUTER_ROWS = 512
DISPATCH_ROWS = 1024
COMBINE_ROWS = 512


def _cparams(*sem):
    return pltpu.CompilerParams(dimension_semantics=sem, vmem_limit_bytes=V7X_VMEM_LIMIT_BYTES)


def _bdot(a, b):
    return jnp.dot(a.astype(BF16), b.astype(BF16), preferred_element_type=F32)


def _bdot_nt(a, b):
    return lax.dot_general(a.astype(BF16), b.astype(BF16), (((1,), (1,)), ((), ())),
                           preferred_element_type=F32)


def _bdot_tn(a, b):
    return lax.dot_general(a.astype(BF16), b.astype(BF16), (((0,), (0,)), ((), ())),
                           preferred_element_type=F32)


def _split_bf16(a):
    hi = a.astype(BF16)
    lo = (a - hi.astype(F32)).astype(BF16)
    return hi, lo


def _rms(x, gain):
    return x * lax.rsqrt(jnp.mean(x * x, axis=-1, keepdims=True) + RMS_EPS) * gain


def _norm_matmul_kernel(x_ref, g_ref, w_ref, o_ref, xn_ref):
    @pl.when(pl.program_id(1) == 0)
    def _():
        xn_ref[...] = _rms(x_ref[...], g_ref[...]).astype(BF16)

    o_ref[...] = jnp.dot(xn_ref[...], w_ref[...], preferred_element_type=F32).astype(o_ref.dtype)


def _norm_matmul(x, gain, w, tm, tn):
    T, D = x.shape
    N = w.shape[1]
    return pl.pallas_call(
        _norm_matmul_kernel,
        grid=(T // tm, N // tn),
        in_specs=[pl.BlockSpec((tm, D), lambda i, j: (i, 0)),
                  pl.BlockSpec((1, D), lambda i, j: (0, 0)),
                  pl.BlockSpec((D, tn), lambda i, j: (0, j))],
        out_specs=pl.BlockSpec((tm, tn), lambda i, j: (i, j)),
        out_shape=jax.ShapeDtypeStruct((T, N), BF16),
        scratch_shapes=[pltpu.VMEM((tm, D), BF16)],
        compiler_params=_cparams("parallel", "arbitrary"),
        name="norm_matmul",
    )(x, gain.reshape(1, D), w)


def _norm_matmul2_kernel(x_ref, g_ref, w_ref, w2_ref, o_ref, o2_ref, xn_ref):
    @pl.when(pl.program_id(1) == 0)
    def _():
        xn = _rms(x_ref[...], g_ref[...])
        xn_ref[...] = xn.astype(BF16)
        x_hi, x_lo = _split_bf16(xn)
        w_hi, w_lo = _split_bf16(w2_ref[...])
        o2_ref[...] = (jnp.dot(x_hi, w_hi, preferred_element_type=F32)
                       + jnp.dot(x_lo, w_hi, preferred_element_type=F32)
                       + jnp.dot(x_hi, w_lo, preferred_element_type=F32))

    o_ref[...] = jnp.dot(xn_ref[...], w_ref[...], preferred_element_type=F32).astype(o_ref.dtype)


def _norm_matmul2(x, gain, w, w2, tm, tn):
    T, D = x.shape
    N = w.shape[1]
    N2 = w2.shape[1]
    return pl.pallas_call(
        _norm_matmul2_kernel,
        grid=(T // tm, N // tn),
        in_specs=[pl.BlockSpec((tm, D), lambda i, j: (i, 0)),
                  pl.BlockSpec((1, D), lambda i, j: (0, 0)),
                  pl.BlockSpec((D, tn), lambda i, j: (0, j)),
                  pl.BlockSpec((D, N2), lambda i, j: (0, 0))],
        out_specs=[pl.BlockSpec((tm, tn), lambda i, j: (i, j)),
                   pl.BlockSpec((tm, N2), lambda i, j: (i, 0))],
        out_shape=[jax.ShapeDtypeStruct((T, N), BF16), jax.ShapeDtypeStruct((T, N2), F32)],
        scratch_shapes=[pltpu.VMEM((tm, D), BF16)],
        compiler_params=_cparams("parallel", "arbitrary"),
        name="norm_matmul2",
    )(x, gain.reshape(1, D), w, w2)


def _matmul_residual_kernel(a_ref, w_ref, x_ref, o_ref):
    o_ref[...] = x_ref[...] + jnp.dot(a_ref[...], w_ref[...], preferred_element_type=F32)


def _matmul_residual(a, w, x, tm, tn):
    T, K = a.shape
    N = w.shape[1]
    return pl.pallas_call(
        _matmul_residual_kernel,
        grid=(T // tm, N // tn),
        in_specs=[pl.BlockSpec((tm, K), lambda i, j: (i, 0)),
                  pl.BlockSpec((K, tn), lambda i, j: (0, j)),
                  pl.BlockSpec((tm, tn), lambda i, j: (i, j))],
        out_specs=pl.BlockSpec((tm, tn), lambda i, j: (i, j)),
        out_shape=jax.ShapeDtypeStruct((T, N), F32),
        compiler_params=_cparams("parallel", "arbitrary"),
        name="matmul_residual",
    )(a, w, x)


def _attn_kernel(main_ref, prev_ref, next_ref, qg_ref, kg_ref, sink_ref, bias_ref, o_ref):
    n = pl.program_id(1)
    nb = pl.num_programs(1)
    dh, blk = A_HEAD_DIM, A_BLOCK
    kv_cols = A_KV_HEADS * dh
    qb = main_ref.shape[0] // blk
    main = main_ref[...].astype(F32)
    kv = jnp.concatenate([prev_ref[...].astype(F32), main[:, A_HEADS * dh:], next_ref[...].astype(F32)],
                         axis=0)
    col = lax.broadcasted_iota(jnp.int32, (1, 3 * blk), 1)
    no_prev = jnp.where((col < blk) & (n == 0), NEG_INF, 0.0)
    no_next = jnp.where((col >= 2 * blk) & (n == nb - 1), NEG_INF, 0.0)
    edge = [(no_prev if j == 0 else 0.0) + (no_next if j == qb - 1 else 0.0) for j in range(qb)]
    qg = qg_ref[...] * (dh ** -0.5)
    groups = range(A_KV_HEADS)
    units = [(j, g) for j in range(qb) for g in groups]
    kn = [_rms(kv[:, g * dh:(g + 1) * dh], kg_ref[...]).astype(BF16) for g in groups]
    vb = [kv[:, kv_cols + g * dh: kv_cols + (g + 1) * dh].astype(BF16) for g in groups]
    q4 = [jnp.concatenate([_rms(main[j * blk:(j + 1) * blk, (A_REP * g + r) * dh:(A_REP * g + r + 1) * dh],
                                qg).astype(BF16) for r in range(A_REP)], axis=0) for j, g in units]
    s = [_bdot_nt(q, kn[g][j * blk:(j + 3) * blk]) + bias_ref[g] + edge[j] for q, (j, g) in zip(q4, units)]
    m = [jnp.maximum(jnp.max(su, axis=-1, keepdims=True), sink_ref[g]) for su, (j, g) in zip(s, units)]
    p = [jnp.exp(su - mu) for su, mu in zip(s, m)]
    denom = [jnp.sum(pu, axis=-1, keepdims=True) + jnp.exp(sink_ref[g] - mu) for pu, mu, (j, g) in zip(p, m, units)]
    o = [jnp.dot(pu.astype(BF16), vb[g][j * blk:(j + 3) * blk], preferred_element_type=F32) / du
         for pu, du, (j, g) in zip(p, denom, units)]
    for ou, (j, g) in zip(o, units):
        for r in range(A_REP):
            h = A_REP * g + r
            o_ref[j * blk:(j + 1) * blk, h * dh:(h + 1) * dh] = ou[r * blk:(r + 1) * blk].astype(o_ref.dtype)


def _attn_tables(sink):
    blk = A_BLOCK
    slopes = np.array([2.0 ** (-8.0 * (h + 1) / A_HEADS) for h in range(A_HEADS)], np.float32)
    qi = np.arange(blk)
    kj = np.arange(3 * blk)
    dist = np.abs(blk + qi[:, None] - kj[None, :]).astype(np.float32)
    bias = np.where(dist[None] <= blk, -slopes[:, None, None] * dist[None], np.float32(NEG_INF))
    bias = bias.astype(np.float32).reshape(A_KV_HEADS, A_REP * blk, 3 * blk)
    sink_rows = jnp.repeat(sink.astype(F32).reshape(A_KV_HEADS, A_REP), blk, axis=1)[..., None]
    return jnp.asarray(bias), sink_rows


def _attention(qkv, q_gain, k_gain, sink, batch, seq):
    T, W = qkv.shape
    blk, dh = A_BLOCK, A_HEAD_DIM
    nb = seq // blk
    kv_w = 2 * A_KV_HEADS * dh
    kv_blk = (A_HEADS * dh) // kv_w
    bias, sink_rows = _attn_tables(sink)
    qb = ATTN_QBLOCKS
    steps = nb // qb
    return pl.pallas_call(
        _attn_kernel,
        grid=(batch, steps),
        in_specs=[
            pl.BlockSpec((qb * blk, W), lambda b, n: (b * steps + n, 0)),
            pl.BlockSpec((blk, kv_w), lambda b, n: (b * nb + jnp.maximum(n * qb - 1, 0), kv_blk)),
            pl.BlockSpec((blk, kv_w), lambda b, n: (b * nb + jnp.minimum(n * qb + qb, nb - 1), kv_blk)),
            pl.BlockSpec((1, dh), lambda b, n: (0, 0)),
            pl.BlockSpec((1, dh), lambda b, n: (0, 0)),
            pl.BlockSpec((A_KV_HEADS, A_REP * blk, 1), lambda b, n: (0, 0, 0)),
            pl.BlockSpec((A_KV_HEADS, A_REP * blk, 3 * blk), lambda b, n: (0, 0, 0)),
        ],
        out_specs=pl.BlockSpec((qb * blk, A_HEADS * dh), lambda b, n: (b * steps + n, 0)),
        out_shape=jax.ShapeDtypeStruct((T, A_HEADS * dh), BF16),
        compiler_params=_cparams("parallel", "parallel"),
        name="window_attention",
    )(qkv, qkv, qkv, q_gain.reshape(1, dh), k_gain.reshape(1, dh), sink_rows, bias)


def _route(logits):
    lane = lax.broadcasted_iota(jnp.int32, logits.shape, 1).astype(F32)
    big = jnp.float32(1e9)
    is_g = (lane >= N_EXPERTS) & (lane < N_EXPERTS + N_GROUPS)
    lg = jnp.where(is_g, logits, NEG_INF)
    gmax = jnp.max(lg, axis=-1, keepdims=True)
    gidx = jnp.min(jnp.where(is_g & (lg == gmax), lane, big), axis=-1, keepdims=True) - N_EXPERTS
    g_prob = 1.0 / jnp.sum(jnp.where(is_g, jnp.exp(lg - gmax), 0.0), axis=-1, keepdims=True)
    lo = gidx * EXPERTS_PER_GROUP
    in_grp = (lane >= lo) & (lane < lo + EXPERTS_PER_GROUP)
    le = jnp.where(in_grp, logits, NEG_INF)
    emax = jnp.max(le, axis=-1, keepdims=True)
    ex = jnp.where(in_grp, jnp.exp(le - emax), 0.0)
    prob = ex / jnp.sum(ex, axis=-1, keepdims=True)
    cand = jnp.where(in_grp, prob, -1.0)
    p1 = jnp.max(cand, axis=-1, keepdims=True)
    i1 = jnp.min(jnp.where(cand == p1, lane, big), axis=-1, keepdims=True)
    cand2 = jnp.where(lane == i1, -1.0, cand)
    p2 = jnp.max(cand2, axis=-1, keepdims=True)
    i2 = jnp.min(jnp.where(cand2 == p2, lane, big), axis=-1, keepdims=True)
    scale = g_prob / (p1 + p2)
    return i1, i2, p1 * scale, p2 * scale


RT_E1, RT_E2, RT_G1, RT_G2, RT_R1, RT_R2 = range(6)


def _pack_bf16_pairs(a):
    n = a.shape[1] // 2
    hi = lax.bitcast_convert_type(a[:, :n].astype(BF16).astype(F32), jnp.uint32)
    lo = lax.bitcast_convert_type(a[:, n:].astype(BF16).astype(F32), jnp.uint32)
    return hi | (lo >> 16)


def _unpack_bf16_pairs(p):
    hi = lax.bitcast_convert_type(p & jnp.uint32(0xFFFF0000), F32)
    lo = lax.bitcast_convert_type(p << 16, F32)
    return jnp.concatenate([hi, lo], axis=1)


LANES = 128
D_MODEL = 1024
TOKEN_ROWS = D_MODEL // 2 // LANES


def _store_token_major(ref, packed):
    m, w = packed.shape
    s_per = w // LANES
    for s in range(s_per):
        ref[pl.ds(s, m, stride=s_per), :] = packed[:, s * LANES:(s + 1) * LANES]


def _load_token_major(ref, m, s_per):
    return jnp.concatenate([ref[pl.ds(s, m, stride=s_per), :] for s in range(s_per)], axis=1)


def _router_kernel(x_ref, g_ref, wr_ref, br_ref, h_ref, route_ref, counts_ref, carry_ref):
    @pl.when(pl.program_id(0) == 0)
    def _():
        carry_ref[...] = jnp.zeros_like(carry_ref)

    hn = _rms(x_ref[...], g_ref[...])
    _store_token_major(h_ref, _pack_bf16_pairs(hn))
    h_hi, h_lo = _split_bf16(hn)
    w_hi, w_lo = _split_bf16(wr_ref[...])
    logits = (jnp.dot(h_hi, w_hi, preferred_element_type=F32)
              + jnp.dot(h_lo, w_hi, preferred_element_type=F32)
              + jnp.dot(h_hi, w_lo, preferred_element_type=F32)) + br_ref[...]
    i1, i2, g1, g2 = _route(logits)
    tm = logits.shape[0]
    lane = lax.broadcasted_iota(jnp.int32, logits.shape, 1).astype(F32)
    chosen = ((lane == i1) | (lane == i2)).astype(BF16)
    earlier = (lax.broadcasted_iota(jnp.int32, (tm, tm), 1)
               < lax.broadcasted_iota(jnp.int32, (tm, tm), 0)).astype(BF16)
    before = jnp.dot(earlier, chosen, preferred_element_type=F32) + carry_ref[...]
    r1 = jnp.sum(jnp.where(lane == i1, before, 0.0), axis=-1, keepdims=True)
    r2 = jnp.sum(jnp.where(lane == i2, before, 0.0), axis=-1, keepdims=True)
    carry_ref[...] += jnp.sum(chosen.astype(F32), axis=0, keepdims=True)
    rec = jnp.zeros_like(logits)
    for slot, val in ((RT_E1, i1), (RT_E2, i2), (RT_G1, g1), (RT_G2, g2), (RT_R1, r1), (RT_R2, r2)):
        rec = jnp.where(lane == slot, val, rec)
    route_ref[...] = rec
    counts_ref[...] = carry_ref[...]


def _router(x, gain, w_group, b_group, w_expert, b_expert, tm):
    T, D = x.shape
    pad = ROUTER_LANES - N_EXPERTS - N_GROUPS
    wr = jnp.concatenate([w_expert, w_group, jnp.zeros((D, pad), F32)], axis=1)
    br = jnp.concatenate([b_expert, b_group, jnp.zeros((pad,), F32)]).reshape(1, ROUTER_LANES)
    return pl.pallas_call(
        _router_kernel,
        grid=(T // tm,),
        in_specs=[pl.BlockSpec((tm, D), lambda i: (i, 0)),
                  pl.BlockSpec((1, D), lambda i: (0, 0)),
                  pl.BlockSpec((D, ROUTER_LANES), lambda i: (0, 0)),
                  pl.BlockSpec((1, ROUTER_LANES), lambda i: (0, 0))],
        out_specs=[pl.BlockSpec((tm * TOKEN_ROWS, LANES), lambda i: (i, 0)),
                   pl.BlockSpec((tm, ROUTER_LANES), lambda i: (i, 0)),
                   pl.BlockSpec((1, ROUTER_LANES), lambda i: (0, 0))],
        out_shape=[jax.ShapeDtypeStruct((T * TOKEN_ROWS, LANES), jnp.uint32),
                   jax.ShapeDtypeStruct((T, ROUTER_LANES), F32),
                   jax.ShapeDtypeStruct((1, ROUTER_LANES), F32)],
        scratch_shapes=[pltpu.VMEM((1, ROUTER_LANES), F32)],
        compiler_params=_cparams("arbitrary"),
        name="moe_router",
    )(x, gain.reshape(1, D), wr, br)


MOE_TILE = 512
MOE_SLOTS = 2
DMA_UNROLL = 8


def _token_copy(src, src_row, dst, dst_row, sem):
    return pltpu.make_async_copy(src.at[pl.ds(pl.multiple_of(src_row, TOKEN_ROWS), TOKEN_ROWS)],
                                 dst.at[pl.ds(pl.multiple_of(dst_row, TOKEN_ROWS), TOKEN_ROWS)], sem)


def _dispatch_kernel(pos_ref, h_ref, zeros_hbm, xs_hbm, sem):
    del zeros_hbm
    tm = h_ref.shape[0] // TOKEN_ROWS

    def start(g, c):
        r0 = pl.multiple_of(g * DMA_UNROLL, DMA_UNROLL)
        for j in range(DMA_UNROLL):
            for k in range(MOE_SLOTS):
                _token_copy(h_ref, (r0 + j) * TOKEN_ROWS, xs_hbm, pos_ref[MOE_SLOTS * (r0 + j) + k],
                            sem).start(priority=k)
        return c

    def wait(r, c):
        for k in range(MOE_SLOTS):
            _token_copy(h_ref, 0, xs_hbm, 0, sem).wait()
        return c

    lax.fori_loop(0, tm // DMA_UNROLL, start, 0)
    lax.fori_loop(0, tm, wait, 0, unroll=DMA_UNROLL)


def _dispatch(h, pos, n_rows, tm):
    T = h.shape[0] // TOKEN_ROWS
    zeros = jnp.zeros((n_rows * TOKEN_ROWS, LANES), h.dtype)
    return pl.pallas_call(
        _dispatch_kernel,
        grid=(T // tm,),
        in_specs=[pl.BlockSpec((MOE_SLOTS * tm,), lambda i: (i,), memory_space=pltpu.SMEM),
                  pl.BlockSpec((tm * TOKEN_ROWS, LANES), lambda i: (i, 0)),
                  pl.BlockSpec(memory_space=pl.ANY)],
        out_specs=pl.BlockSpec(memory_space=pl.ANY),
        out_shape=jax.ShapeDtypeStruct(zeros.shape, h.dtype),
        scratch_shapes=[pltpu.SemaphoreType.DMA(())],
        input_output_aliases={2: 0},
        compiler_params=_cparams("arbitrary"),
        name="moe_dispatch",
    )(pos, h, zeros)


def _expert_kernel(te_ref, nv_ref, xs_ref, wg_ref, wu_ref, wd_ref, ys_ref, wgu_s, wd_s):
    i = pl.program_id(0)
    nv = nv_ref[0]
    valid = i < nv
    ic = jnp.minimum(i, nv - 1)
    changed = (i == 0) | (te_ref[ic] != te_ref[jnp.maximum(ic - 1, 0)])

    @pl.when(valid & changed)
    def _():
        wgu_s[:, :D_EXPERT] = wg_ref[0, 0].astype(BF16)
        wgu_s[:, D_EXPERT:] = wu_ref[0, 0].astype(BF16)
        wd_s[...] = wd_ref[0, 0].astype(BF16)

    @pl.when(valid)
    def _():
        x = _unpack_bf16_pairs(_load_token_major(xs_ref, MOE_TILE, TOKEN_ROWS)).astype(BF16)
        gu = jnp.dot(x, wgu_s[...], preferred_element_type=F32)
        gate, up = gu[:, :D_EXPERT], gu[:, D_EXPERT:]
        hid = (gate * jax.nn.sigmoid(gate)) * up
        y = jnp.dot(hid.astype(BF16), wd_s[...], preferred_element_type=F32)
        _store_token_major(ys_ref, _pack_bf16_pairs(y))

    @pl.when(jnp.logical_not(valid))
    def _():
        ys_ref[...] = jnp.zeros_like(ys_ref)


def _experts(xs, tile_expert, n_valid, w_gate, w_up, w_down, layer):
    D = TOKEN_ROWS * LANES * 2
    blk = MOE_TILE * TOKEN_ROWS
    n_tiles = xs.shape[0] // blk
    row = lambda i, te, nv: (jnp.minimum(i, nv[0] - 1), 0)
    wsel = lambda i, te, nv: (layer, te[jnp.minimum(i, nv[0] - 1)], 0, 0)
    return pl.pallas_call(
        _expert_kernel,
        grid_spec=pltpu.PrefetchScalarGridSpec(
            num_scalar_prefetch=2,
            grid=(n_tiles,),
            in_specs=[pl.BlockSpec((blk, LANES), row),
                      pl.BlockSpec((1, 1, D, D_EXPERT), wsel),
                      pl.BlockSpec((1, 1, D, D_EXPERT), wsel),
                      pl.BlockSpec((1, 1, D_EXPERT, D), wsel)],
            out_specs=pl.BlockSpec((blk, LANES), lambda i, te, nv: (i, 0)),
            scratch_shapes=[pltpu.VMEM((D, 2 * D_EXPERT), BF16), pltpu.VMEM((D_EXPERT, D), BF16)]),
        out_shape=jax.ShapeDtypeStruct(xs.shape, jnp.uint32),
        compiler_params=_cparams("arbitrary"),
        name="moe_experts",
    )(tile_expert, n_valid, xs, w_gate, w_up, w_down)


def _combine_kernel(pos_ref, pos_next_ref, x_ref, route_ref, ys_hbm, o_ref, buf, sems):
    i = pl.program_id(0)
    tm = x_ref.shape[0]

    def gather(p_ref, ring):
        def start(g, c):
            r0 = pl.multiple_of(g * DMA_UNROLL, DMA_UNROLL)
            for j in range(DMA_UNROLL):
                for k in range(MOE_SLOTS):
                    _token_copy(ys_hbm, p_ref[MOE_SLOTS * (r0 + j) + k], buf.at[ring, k], (r0 + j) * TOKEN_ROWS,
                                sems.at[ring]).start(priority=k)
            return c
        lax.fori_loop(0, tm // DMA_UNROLL, start, 0)

    @pl.when(i == 0)
    def _():
        gather(pos_ref, 0)

    @pl.when(i + 1 < pl.num_programs(0))
    def _():
        gather(pos_next_ref, (i + 1) % 2)

    ring = i % 2

    def wait(r, c):
        for k in range(MOE_SLOTS):
            _token_copy(ys_hbm, 0, buf.at[ring, k], 0, sems.at[ring]).wait()
        return c

    lax.fori_loop(0, tm, wait, 0, unroll=DMA_UNROLL)
    rec = route_ref[...]
    y1 = _unpack_bf16_pairs(_load_token_major(buf.at[ring, 0], tm, TOKEN_ROWS))
    y2 = _unpack_bf16_pairs(_load_token_major(buf.at[ring, 1], tm, TOKEN_ROWS))
    o_ref[...] = x_ref[...] + rec[:, RT_G1:RT_G1 + 1] * y1 + rec[:, RT_G2:RT_G2 + 1] * y2


def _combine(x, route, pos, ys, tm):
    T, D = x.shape
    n = T // tm
    return pl.pallas_call(
        _combine_kernel,
        grid=(n,),
        in_specs=[pl.BlockSpec((MOE_SLOTS * tm,), lambda i: (i,), memory_space=pltpu.SMEM),
                  pl.BlockSpec((MOE_SLOTS * tm,), lambda i: (jnp.minimum(i + 1, n - 1),), memory_space=pltpu.SMEM),
                  pl.BlockSpec((tm, D), lambda i: (i, 0)),
                  pl.BlockSpec((tm, ROUTER_LANES), lambda i: (i, 0)),
                  pl.BlockSpec(memory_space=pl.ANY)],
        out_specs=pl.BlockSpec((tm, D), lambda i: (i, 0)),
        out_shape=jax.ShapeDtypeStruct((T, D), F32),
        scratch_shapes=[pltpu.VMEM((2, MOE_SLOTS, tm * TOKEN_ROWS, LANES), ys.dtype),
                        pltpu.SemaphoreType.DMA((2,))],
        compiler_params=_cparams("arbitrary"),
        name="moe_combine",
    )(pos, pos, x, route, ys)


def _moe(x, gain, w_group, b_group, w_expert, b_expert, w_gate, w_up, w_down, layer):
    T, D = x.shape
    n_tiles = (MOE_SLOTS * T + N_EXPERTS * (MOE_TILE - 1)) // MOE_TILE + 1
    h, route, counts = _router(x, gain, w_group, b_group, w_expert, b_expert, tm=ROUTER_ROWS)
    counts = counts[0, :N_EXPERTS].astype(jnp.int32)
    padded = (counts + MOE_TILE - 1) // MOE_TILE * MOE_TILE
    ends = jnp.cumsum(padded)
    starts = ends - padded
    eid = route[:, RT_E1:RT_E2 + 1].astype(jnp.int32)
    rank = route[:, RT_R1:RT_R2 + 1].astype(jnp.int32)
    onehot = eid[..., None] == jnp.arange(N_EXPERTS, dtype=jnp.int32)
    pos = (rank + jnp.sum(jnp.where(onehot, starts, 0), axis=-1)).reshape(-1)
    pos = pos * TOKEN_ROWS
    tile_start = jnp.arange(n_tiles, dtype=jnp.int32) * MOE_TILE
    tile_expert = jnp.minimum(jnp.sum(tile_start[:, None] >= ends[None, :], axis=-1), N_EXPERTS - 1).astype(jnp.int32)
    n_valid = (ends[-1:] // MOE_TILE).astype(jnp.int32)
    xs = _dispatch(h, pos, n_tiles * MOE_TILE, tm=DISPATCH_ROWS)
    ys = _experts(xs, tile_expert, n_valid, w_gate, w_up, w_down, layer)
    return _combine(x, route, pos, ys, tm=COMBINE_ROWS)


GDN_HALO = 16


def _gdn_conv_kernel(u_ref, up_ref, un_ref, w_ref, ab_ref, aexp_ref, dtb_ref, o_ref, gb_ref, *, tiles_per_seq):
    i = pl.program_id(0)
    j = pl.program_id(1)
    tm = u_ref.shape[0]
    first = (i % tiles_per_seq) == 0
    last = (i % tiles_per_seq) == tiles_per_seq - 1
    u = u_ref[...].astype(F32)
    prev = jnp.where(first, 0.0, up_ref[...].astype(F32))
    nxt = jnp.where(last, 0.0, un_ref[...].astype(F32))
    ext = jnp.concatenate([prev, u, nxt], axis=0)
    w = w_ref[...]
    h0 = GDN_HALO - B_CONV // 2
    y = w[0:1] * ext[h0:h0 + tm]
    for t in range(1, B_CONV):
        y = y + w[t:t + 1] * ext[h0 + t:h0 + t + tm]
    y = y * jax.nn.sigmoid(y)
    qscale = jnp.where(j == 0, B_HEAD_DIM ** -0.5, 1.0)
    for h in range(B_HEADS):
        slab = y[:, h * B_HEAD_DIM:(h + 1) * B_HEAD_DIM]
        inv = lax.rsqrt(jnp.sum(slab * slab, axis=-1, keepdims=True) + RMS_EPS) * qscale
        o_ref[0, :, h * B_HEAD_DIM:(h + 1) * B_HEAD_DIM] = slab * jnp.where(j < 2, inv, 1.0)

    @pl.when(j == 0)
    def _():
        ab = ab_ref[...]
        lane = lax.broadcasted_iota(jnp.int32, ab.shape, 1)
        z = ab + dtb_ref[...]
        softplus = jnp.maximum(z, 0.0) + jnp.log1p(jnp.exp(-jnp.abs(z)))
        gb_ref[...] = jnp.where(lane < 2 * B_HEADS, -aexp_ref[...] * softplus, jax.nn.sigmoid(ab))


def _gdn_conv(qkvz, ab, conv_w, a_log, dt_bias, seq, tm):
    T = qkvz.shape[0]
    D = B_HEADS * B_HEAD_DIM
    hb = tm // GDN_HALO
    n_halo = T // GDN_HALO
    pad = ROUTER_LANES - 2 * B_HEADS
    aexp = jnp.concatenate([jnp.exp(a_log.astype(F32)).reshape(-1), jnp.zeros((pad,), F32)]).reshape(1, -1)
    dtb = jnp.concatenate([dt_bias.astype(F32).reshape(-1), jnp.zeros((pad,), F32)]).reshape(1, -1)
    return pl.pallas_call(
        functools.partial(_gdn_conv_kernel, tiles_per_seq=seq // tm),
        grid=(T // tm, 3),
        in_specs=[pl.BlockSpec((tm, D), lambda i, j: (i, j)),
                  pl.BlockSpec((GDN_HALO, D), lambda i, j: (jnp.maximum(i * hb - 1, 0), j)),
                  pl.BlockSpec((GDN_HALO, D), lambda i, j: (jnp.minimum((i + 1) * hb, n_halo - 1), j)),
                  pl.BlockSpec((B_CONV, D), lambda i, j: (0, j)),
                  pl.BlockSpec((tm, ROUTER_LANES), lambda i, j: (i, 0)),
                  pl.BlockSpec((1, ROUTER_LANES), lambda i, j: (0, 0)),
                  pl.BlockSpec((1, ROUTER_LANES), lambda i, j: (0, 0))],
        out_specs=[pl.BlockSpec((1, tm, D), lambda i, j: (j, i, 0)),
                   pl.BlockSpec((tm, ROUTER_LANES), lambda i, j: (i, 0))],
        out_shape=[jax.ShapeDtypeStruct((3, T, D), F32), jax.ShapeDtypeStruct((T, ROUTER_LANES), F32)],
        compiler_params=_cparams("parallel", "arbitrary"),
        name="gdn_conv",
    )(qkvz, qkvz, qkvz, conv_w, ab, aexp, dtb)


def _gdn_gate_terms(gb, incl):
    C = gb.shape[0]
    lane = lax.broadcasted_iota(jnp.int32, gb.shape, 1)
    g_hi, g_lo = _split_bf16(jnp.where(lane < 2 * B_HEADS, gb, 0.0))
    tri = incl.astype(BF16)
    gc = jnp.dot(tri, g_hi, preferred_element_type=F32) + jnp.dot(tri, g_lo, preferred_element_type=F32)
    return gc, jnp.concatenate([gc, jnp.zeros_like(gc)], axis=0).T


def _gdn_scan_kernel(qf_ref, kf_ref, vf_ref, gf_ref, qb_ref, kb_ref, vb_ref, gbw_ref, of_ref, ob_ref, sf_ref, sb_ref):
    @pl.when(pl.program_id(1) == 0)
    def _():
        sf_ref[...] = jnp.zeros_like(sf_ref)
        sb_ref[...] = jnp.zeros_like(sb_ref)

    C = GDN_CHUNK
    n_chunks = gf_ref.shape[0] // C
    dk = B_HEAD_DIM
    row = lax.broadcasted_iota(jnp.int32, (C, C), 0)
    colm = lax.broadcasted_iota(jnp.int32, (C, C), 1)
    tri = (row >= colm, row <= colm)
    row2 = lax.broadcasted_iota(jnp.int32, (C, 2 * C), 0)
    col2 = lax.broadcasted_iota(jnp.int32, (C, 2 * C), 1)
    left = col2 < C
    incl = (left & (row2 >= col2), left & (row2 <= col2))
    strict = (left & (row2 > col2), left & (row2 < col2))
    eye_right = (col2 == row2 + C).astype(F32)
    qkv_refs = ((qf_ref, kf_ref, vf_ref), (qb_ref, kb_ref, vb_ref))
    s_refs = (sf_ref, sb_ref)
    o_refs = (of_ref, ob_ref)

    def chunk_step(c, carry):
        rows = (pl.ds(pl.multiple_of(c * C, C), C), pl.ds(pl.multiple_of((n_chunks - 1 - c) * C, C), C))
        _gdn_chunk_pair(rows, (gf_ref, gbw_ref), qkv_refs, s_refs, o_refs, tri, incl, strict, left, eye_right)
        return carry

    lax.fori_loop(0, n_chunks, chunk_step, 0)


def _gdn_chunk_pair(rows, g_refs, qkv_refs, s_refs, o_refs, tri, incl, strict, left, eye_right):
    C = GDN_CHUNK
    dk = B_HEAD_DIM
    gbs = tuple(g_refs[d][rows[d], :] for d in range(2))
    gates = [_gdn_gate_terms(gbs[d], tri[d]) for d in range(2)]
    glast = [gates[0][0][C - 1:C], gates[1][0][0:1]]
    units = [(d, h) for d in range(2) for h in range(B_HEADS)]

    def lane_of(d, h):
        return d * B_HEADS + h

    def cols(h):
        return slice(h * dk, (h + 1) * dk)

    v_b, kb_l, qd_bf, kd_bf, a_l, dec_l, egc_l = [], [], [], [], [], [], []
    for d, h in units:
        r = lane_of(d, h)
        gc, gct = gates[d]
        gcol = gc[:, r:r + 1]
        beta = gbs[d][:, 2 * B_HEADS + r:2 * B_HEADS + r + 1]
        q_ref, k_ref, v_ref = qkv_refs[d]
        qh, kh, vh = q_ref[0, rows[d], cols(h)], k_ref[0, rows[d], cols(h)], v_ref[0, rows[d], cols(h)]
        egc = jnp.exp(gcol)
        kb = kh * beta
        khb = jnp.concatenate([kh.astype(BF16), jnp.zeros((C, dk), BF16)], axis=0)
        a_l.append(_bdot_nt(jnp.concatenate([kb, qh], axis=0), khb))
        dec_l.append(jnp.exp(jnp.where(incl[d], gcol - gct[r:r + 1, :], NEG_INF)))
        v_b.append(vh * beta)
        kb_l.append(kb)
        egc_l.append(egc)
        qd_bf.append((qh * egc).astype(BF16))
        kd_bf.append((kh * jnp.exp(glast[d][:, r:r + 1] - gcol)).astype(BF16))
    r_l = [eye_right - jnp.where(strict[d], a[:C] * dec, 0.0)
           for (d, h), a, dec in zip(units, a_l, dec_l)]
    intra_bf = [(a[C:] * dec)[:, :C].astype(BF16) for a, dec in zip(a_l, dec_l)]
    x_l = [r[:, :C] for r in r_l]
    n = 1
    while n < C:
        o_l = [_bdot(x, r) for x, r in zip(x_l, r_l)]
        r_l = [o + jnp.where(left, 0.0, r) for o, r in zip(o_l, r_l)]
        x_l = [o[:, :C] for o in o_l]
        n *= 2
    zpad = jnp.zeros((C, 2 * dk), BF16)
    sol_l = [_bdot(r, jnp.concatenate([zpad, jnp.concatenate([vb, kb * egc], axis=1).astype(BF16)], axis=0))
             for r, vb, kb, egc in zip(r_l, v_b, kb_l, egc_l)]
    st_l = [s_refs[d][h] for d, h in units]
    wq_l = [_bdot(jnp.concatenate([sol[:, dk:].astype(BF16), qd], axis=0), st)
            for sol, qd, st in zip(sol_l, qd_bf, st_l)]
    vn_l = [sol[:, :dk] - wq[:C] for sol, wq in zip(sol_l, wq_l)]
    for (d, h), wq, intra, vn in zip(units, wq_l, intra_bf, vn_l):
        o_refs[d][rows[d], cols(h)] = (wq[C:] + _bdot(intra, vn)).astype(o_refs[d].dtype)
    for (d, h), st, kd, vn in zip(units, st_l, kd_bf, vn_l):
        r = lane_of(d, h)
        s_refs[d][h] = st * jnp.exp(glast[d][:, r:r + 1]) + _bdot_tn(kd, vn)


def _gdn_scan(qkv, gb, batch, seq, rows_per_step):
    _, T, D = qkv.shape
    chunk = rows_per_step
    nc = seq // chunk
    fwd = lambda b, c: b * nc + c
    bwd = lambda b, c: b * nc + (nc - 1 - c)
    part = lambda p, f: pl.BlockSpec((1, chunk, D), lambda b, c: (p, f(b, c), 0))
    gspec = lambda f: pl.BlockSpec((chunk, ROUTER_LANES), lambda b, c: (f(b, c), 0))
    ospec = lambda f: pl.BlockSpec((chunk, D), lambda b, c: (f(b, c), 0))
    return pl.pallas_call(
        _gdn_scan_kernel,
        grid=(batch, nc),
        in_specs=[part(0, fwd), part(1, fwd), part(2, fwd), gspec(fwd),
                  part(0, bwd), part(1, bwd), part(2, bwd), gspec(bwd)],
        out_specs=[ospec(fwd), ospec(bwd)],
        out_shape=[jax.ShapeDtypeStruct((T, D), BF16), jax.ShapeDtypeStruct((T, D), BF16)],
        scratch_shapes=[pltpu.VMEM((B_HEADS, B_HEAD_DIM, B_HEAD_DIM), F32),
                        pltpu.VMEM((B_HEADS, B_HEAD_DIM, B_HEAD_DIM), F32)],
        compiler_params=_cparams("parallel", "arbitrary"),
        name="gdn_scan",
    )(qkv, qkv, qkv, gb, qkv, qkv, qkv, gb)


def _gdn_out_kernel(of_ref, ob_ref, z_ref, og_ref, w_ref, x_ref, o_ref, a_ref):
    @pl.when(pl.program_id(1) == 0)
    def _():
        o = of_ref[...].astype(F32) + ob_ref[...].astype(F32)
        z = z_ref[...].astype(F32)
        for h in range(B_HEADS):
            sl = slice(h * B_HEAD_DIM, (h + 1) * B_HEAD_DIM)
            zh = z[:, sl]
            a_ref[:, sl] = (_rms(o[:, sl], og_ref[...]) * (zh * jax.nn.sigmoid(zh))).astype(BF16)

    o_ref[...] = x_ref[...] + jnp.dot(a_ref[...], w_ref[...], preferred_element_type=F32)


def _gdn_out(o_f, o_b, qkvz, o_gain, w, x, tm, tn):
    T, D = x.shape
    return pl.pallas_call(
        _gdn_out_kernel,
        grid=(T // tm, D // tn),
        in_specs=[pl.BlockSpec((tm, D), lambda i, j: (i, 0)),
                  pl.BlockSpec((tm, D), lambda i, j: (i, 0)),
                  pl.BlockSpec((tm, D), lambda i, j: (i, 3)),
                  pl.BlockSpec((1, B_HEAD_DIM), lambda i, j: (0, 0)),
                  pl.BlockSpec((D, tn), lambda i, j: (0, j)),
                  pl.BlockSpec((tm, tn), lambda i, j: (i, j))],
        out_specs=pl.BlockSpec((tm, tn), lambda i, j: (i, j)),
        out_shape=jax.ShapeDtypeStruct((T, D), F32),
        scratch_shapes=[pltpu.VMEM((tm, D), BF16)],
        compiler_params=_cparams("parallel", "arbitrary"),
        name="gdn_out",
    )(o_f, o_b, qkvz, o_gain.reshape(1, B_HEAD_DIM), w, x)


def _attention_layer(x, gain, w_in, q_gain, k_gain, sink, w_out, batch, seq):
    qkv = _norm_matmul(x, gain, w_in.astype(BF16), *QKV_PROJ_TILE)
    a = _attention(qkv, q_gain, k_gain, sink, batch, seq)
    return _matmul_residual(a, w_out.astype(BF16), x, *ATTN_OUT_TILE)


def _gdn_layer(x, gain, w_in, conv_w, a_log, dt_bias, o_gain, w_out, batch, seq):
    D = x.shape[1]
    w_main = w_in[:, :4 * D].astype(BF16)
    pad = ROUTER_LANES - 4 * B_HEADS
    w_ab = jnp.concatenate([w_in[:, 4 * D:], jnp.zeros((D, pad), F32)], axis=1)
    qkvz, ab = _norm_matmul2(x, gain, w_main, w_ab, *GDN_PROJ_TILE)
    qkv, gb = _gdn_conv(qkvz, ab, conv_w, a_log, dt_bias, seq, GDN_CONV_ROWS)
    o_f, o_b = _gdn_scan(qkv, gb, batch, seq, GDN_SCAN_CHUNKS * GDN_CHUNK)
    return _gdn_out(o_f, o_b, qkvz, o_gain, w_out.astype(BF16), x, *GDN_OUT_TILE)


def kernel(x, norm_mix, norm_ffn, attn_w_in, attn_q_gain, attn_k_gain, attn_sink, attn_w_out, gdn_w_in, gdn_conv, gdn_a_log, gdn_dt_bias, gdn_o_gain, gdn_w_out, moe_w_group, moe_b_group, moe_w_expert, moe_b_expert, moe_w_gate, moe_w_up, moe_w_down):
    batch, seq, d_model = x.shape
    depth = norm_mix.shape[0]
    xt = x.reshape(batch * seq, d_model)
    for i in range(depth):
        j = i // 2
        if i % 2 == 0:
            xt = _attention_layer(xt, norm_mix[i], attn_w_in[j], attn_q_gain[j], attn_k_gain[j],
                                  attn_sink[j], attn_w_out[j], batch, seq)
        else:
            xt = _gdn_layer(xt, norm_mix[i], gdn_w_in[j], gdn_conv[j], gdn_a_log[j], gdn_dt_bias[j],
                            gdn_o_gain[j], gdn_w_out[j], batch, seq)
        xt = _moe(xt, norm_ffn[i], moe_w_group[i], moe_b_group[i], moe_w_expert[i], moe_b_expert[i],
                  moe_w_gate, moe_w_up, moe_w_down, layer=i)
    return xt.reshape(batch, seq, d_model)
```

```python
import functools
import math

import jax
import jax.numpy as jnp
import numpy as np
from jax import lax
from jax.experimental import pallas as pl
from jax.experimental.pallas import tpu as pltpu

RMS_EPS = 1e-6
NEG_INF = -1e30
F32 = jnp.float32
BF16 = jnp.bfloat16

A_HEADS = 16
A_KV_HEADS = 4
A_HEAD_DIM = 64
A_REP = A_HEADS // A_KV_HEADS
A_BLOCK = 128
B_HEADS = 8
B_HEAD_DIM = 128
B_CONV = 4
GDN_CHUNK = 64
N_GROUPS = 4
EXPERTS_PER_GROUP = 8
N_EXPERTS = 32
D_EXPERT = 256
ROUTER_LANES = 128

V7X_VMEM_LIMIT_BYTES = 56 * 1024 * 1024

QKV_PROJ_TILE = (1024, 1536)
ATTN_QBLOCKS = 4
ATTN_OUT_TILE = (1024, 1024)
GDN_PROJ_TILE = (1024, 2048)
GDN_CONV_ROWS = 512
GDN_SCAN_CHUNKS = 4
GDN_OUT_TILE = (512, 1024)
ROUTER_ROWS = 512
DISPATCH_ROWS = 1024
COMBINE_ROWS = 1024


def _cparams(*sem):
    return pltpu.CompilerParams(dimension_semantics=sem, vmem_limit_bytes=V7X_VMEM_LIMIT_BYTES)


def _bdot(a, b):
    return jnp.dot(a.astype(BF16), b.astype(BF16), preferred_element_type=F32)


def _bdot_nt(a, b):
    return lax.dot_general(a.astype(BF16), b.astype(BF16), (((1,), (1,)), ((), ())),
                           preferred_element_type=F32)


def _bdot_tn(a, b):
    return lax.dot_general(a.astype(BF16), b.astype(BF16), (((0,), (0,)), ((), ())),
                           preferred_element_type=F32)


def _split_bf16(a):
    hi = a.astype(BF16)
    lo = (a - hi.astype(F32)).astype(BF16)
    return hi, lo


def _rms(x, gain):
    return x * lax.rsqrt(jnp.mean(x * x, axis=-1, keepdims=True) + RMS_EPS) * gain


def _norm_matmul_kernel(x_ref, g_ref, w_ref, o_ref, xn_ref):
    @pl.when(pl.program_id(1) == 0)
    def _():
        xn_ref[...] = _rms(x_ref[...], g_ref[...]).astype(BF16)

    o_ref[...] = jnp.dot(xn_ref[...], w_ref[...], preferred_element_type=F32).astype(o_ref.dtype)


def _norm_matmul(x, gain, w, tm, tn):
    T, D = x.shape
    N = w.shape[1]
    return pl.pallas_call(
        _norm_matmul_kernel,
        grid=(T // tm, N // tn),
        in_specs=[pl.BlockSpec((tm, D), lambda i, j: (i, 0)),
                  pl.BlockSpec((1, D), lambda i, j: (0, 0)),
                  pl.BlockSpec((D, tn), lambda i, j: (0, j))],
        out_specs=pl.BlockSpec((tm, tn), lambda i, j: (i, j)),
        out_shape=jax.ShapeDtypeStruct((T, N), BF16),
        scratch_shapes=[pltpu.VMEM((tm, D), BF16)],
        compiler_params=_cparams("parallel", "arbitrary"),
        name="norm_matmul",
    )(x, gain.reshape(1, D), w)


def _norm_matmul2_kernel(x_ref, g_ref, w_ref, w2_ref, o_ref, o2_ref, xn_ref):
    @pl.when(pl.program_id(1) == 0)
    def _():
        xn = _rms(x_ref[...], g_ref[...])
        xn_ref[...] = xn.astype(BF16)
        x_hi, x_lo = _split_bf16(xn)
        w_hi, w_lo = _split_bf16(w2_ref[...])
        o2_ref[...] = (jnp.dot(x_hi, w_hi, preferred_element_type=F32)
                       + jnp.dot(x_lo, w_hi, preferred_element_type=F32)
                       + jnp.dot(x_hi, w_lo, preferred_element_type=F32))

    o_ref[...] = jnp.dot(xn_ref[...], w_ref[...], preferred_element_type=F32).astype(o_ref.dtype)


def _norm_matmul2(x, gain, w, w2, tm, tn):
    T, D = x.shape
    N = w.shape[1]
    N2 = w2.shape[1]
    return pl.pallas_call(
        _norm_matmul2_kernel,
        grid=(T // tm, N // tn),
        in_specs=[pl.BlockSpec((tm, D), lambda i, j: (i, 0)),
                  pl.BlockSpec((1, D), lambda i, j: (0, 0)),
                  pl.BlockSpec((D, tn), lambda i, j: (0, j)),
                  pl.BlockSpec((D, N2), lambda i, j: (0, 0))],
        out_specs=[pl.BlockSpec((tm, tn), lambda i, j: (i, j)),
                   pl.BlockSpec((tm, N2), lambda i, j: (i, 0))],
        out_shape=[jax.ShapeDtypeStruct((T, N), BF16), jax.ShapeDtypeStruct((T, N2), F32)],
        scratch_shapes=[pltpu.VMEM((tm, D), BF16)],
        compiler_params=_cparams("parallel", "arbitrary"),
        name="norm_matmul2",
    )(x, gain.reshape(1, D), w, w2)


def _matmul_residual_kernel(a_ref, w_ref, x_ref, o_ref):
    o_ref[...] = x_ref[...] + jnp.dot(a_ref[...], w_ref[...], preferred_element_type=F32)


def _matmul_residual(a, w, x, tm, tn):
    T, K = a.shape
    N = w.shape[1]
    return pl.pallas_call(
        _matmul_residual_kernel,
        grid=(T // tm, N // tn),
        in_specs=[pl.BlockSpec((tm, K), lambda i, j: (i, 0)),
                  pl.BlockSpec((K, tn), lambda i, j: (0, j)),
                  pl.BlockSpec((tm, tn), lambda i, j: (i, j))],
        out_specs=pl.BlockSpec((tm, tn), lambda i, j: (i, j)),
        out_shape=jax.ShapeDtypeStruct((T, N), F32),
        compiler_params=_cparams("parallel", "arbitrary"),
        name="matmul_residual",
    )(a, w, x)


def _attn_kernel(main_ref, prev_ref, next_ref, qg_ref, kg_ref, sink_ref, bias_ref, o_ref):
    n = pl.program_id(1)
    nb = pl.num_programs(1)
    dh, blk = A_HEAD_DIM, A_BLOCK
    kv_cols = A_KV_HEADS * dh
    qb = main_ref.shape[0] // blk
    main = main_ref[...].astype(F32)
    kv = jnp.concatenate([prev_ref[...].astype(F32), main[:, A_HEADS * dh:], next_ref[...].astype(F32)],
                         axis=0)
    col = lax.broadcasted_iota(jnp.int32, (1, 3 * blk), 1)
    no_prev = jnp.where((col < blk) & (n == 0), NEG_INF, 0.0)
    no_next = jnp.where((col >= 2 * blk) & (n == nb - 1), NEG_INF, 0.0)
    edge = [(no_prev if j == 0 else 0.0) + (no_next if j == qb - 1 else 0.0) for j in range(qb)]
    qg = qg_ref[...] * (dh ** -0.5)
    groups = range(A_KV_HEADS)
    units = [(j, g) for j in range(qb) for g in groups]
    kn = [_rms(kv[:, g * dh:(g + 1) * dh], kg_ref[...]).astype(BF16) for g in groups]
    vb = [kv[:, kv_cols + g * dh: kv_cols + (g + 1) * dh].astype(BF16) for g in groups]
    q4 = [jnp.concatenate([_rms(main[j * blk:(j + 1) * blk, (A_REP * g + r) * dh:(A_REP * g + r + 1) * dh],
                                qg).astype(BF16) for r in range(A_REP)], axis=0) for j, g in units]
    s = [_bdot_nt(q, kn[g][j * blk:(j + 3) * blk]) + bias_ref[g] + edge[j] for q, (j, g) in zip(q4, units)]
    m = [jnp.maximum(jnp.max(su, axis=-1, keepdims=True), sink_ref[g]) for su, (j, g) in zip(s, units)]
    p = [jnp.exp(su - mu) for su, mu in zip(s, m)]
    denom = [jnp.sum(pu, axis=-1, keepdims=True) + jnp.exp(sink_ref[g] - mu) for pu, mu, (j, g) in zip(p, m, units)]
    o = [jnp.dot(pu.astype(BF16), vb[g][j * blk:(j + 3) * blk], preferred_element_type=F32) / du
         for pu, du, (j, g) in zip(p, denom, units)]
    for ou, (j, g) in zip(o, units):
        for r in range(A_REP):
            h = A_REP * g + r
            o_ref[j * blk:(j + 1) * blk, h * dh:(h + 1) * dh] = ou[r * blk:(r + 1) * blk].astype(o_ref.dtype)


def _attn_tables(sink):
    blk = A_BLOCK
    slopes = np.array([2.0 ** (-8.0 * (h + 1) / A_HEADS) for h in range(A_HEADS)], np.float32)
    qi = np.arange(blk)
    kj = np.arange(3 * blk)
    dist = np.abs(blk + qi[:, None] - kj[None, :]).astype(np.float32)
    bias = np.where(dist[None] <= blk, -slopes[:, None, None] * dist[None], np.float32(NEG_INF))
    bias = bias.astype(np.float32).reshape(A_KV_HEADS, A_REP * blk, 3 * blk)
    sink_rows = jnp.repeat(sink.astype(F32).reshape(A_KV_HEADS, A_REP), blk, axis=1)[..., None]
    return jnp.asarray(bias), sink_rows


def _attention(qkv, q_gain, k_gain, sink, batch, seq):
    T, W = qkv.shape
    blk, dh = A_BLOCK, A_HEAD_DIM
    nb = seq // blk
    kv_w = 2 * A_KV_HEADS * dh
    kv_blk = (A_HEADS * dh) // kv_w
    bias, sink_rows = _attn_tables(sink)
    qb = ATTN_QBLOCKS
    steps = nb // qb
    return pl.pallas_call(
        _attn_kernel,
        grid=(batch, steps),
        in_specs=[
            pl.BlockSpec((qb * blk, W), lambda b, n: (b * steps + n, 0)),
            pl.BlockSpec((blk, kv_w), lambda b, n: (b * nb + jnp.maximum(n * qb - 1, 0), kv_blk)),
            pl.BlockSpec((blk, kv_w), lambda b, n: (b * nb + jnp.minimum(n * qb + qb, nb - 1), kv_blk)),
            pl.BlockSpec((1, dh), lambda b, n: (0, 0)),
            pl.BlockSpec((1, dh), lambda b, n: (0, 0)),
            pl.BlockSpec((A_KV_HEADS, A_REP * blk, 1), lambda b, n: (0, 0, 0)),
            pl.BlockSpec((A_KV_HEADS, A_REP * blk, 3 * blk), lambda b, n: (0, 0, 0)),
        ],
        out_specs=pl.BlockSpec((qb * blk, A_HEADS * dh), lambda b, n: (b * steps + n, 0)),
        out_shape=jax.ShapeDtypeStruct((T, A_HEADS * dh), BF16),
        compiler_params=_cparams("parallel", "parallel"),
        name="window_attention",
    )(qkv, qkv, qkv, q_gain.reshape(1, dh), k_gain.reshape(1, dh), sink_rows, bias)


def _route(logits):
    lane = lax.broadcasted_iota(jnp.int32, logits.shape, 1).astype(F32)
    big = jnp.float32(1e9)
    is_g = (lane >= N_EXPERTS) & (lane < N_EXPERTS + N_GROUPS)
    lg = jnp.where(is_g, logits, NEG_INF)
    gmax = jnp.max(lg, axis=-1, keepdims=True)
    gidx = jnp.min(jnp.where(is_g & (lg == gmax), lane, big), axis=-1, keepdims=True) - N_EXPERTS
    g_prob = 1.0 / jnp.sum(jnp.where(is_g, jnp.exp(lg - gmax), 0.0), axis=-1, keepdims=True)
    lo = gidx * EXPERTS_PER_GROUP
    in_grp = (lane >= lo) & (lane < lo + EXPERTS_PER_GROUP)
    le = jnp.where(in_grp, logits, NEG_INF)
    emax = jnp.max(le, axis=-1, keepdims=True)
    ex = jnp.where(in_grp, jnp.exp(le - emax), 0.0)
    prob = ex / jnp.sum(ex, axis=-1, keepdims=True)
    cand = jnp.where(in_grp, prob, -1.0)
    p1 = jnp.max(cand, axis=-1, keepdims=True)
    i1 = jnp.min(jnp.where(cand == p1, lane, big), axis=-1, keepdims=True)
    cand2 = jnp.where(lane == i1, -1.0, cand)
    p2 = jnp.max(cand2, axis=-1, keepdims=True)
    i2 = jnp.min(jnp.where(cand2 == p2, lane, big), axis=-1, keepdims=True)
    scale = g_prob / (p1 + p2)
    return i1, i2, p1 * scale, p2 * scale


RT_E1, RT_E2, RT_G1, RT_G2, RT_R1, RT_R2 = range(6)


def _pack_bf16_pairs(a):
    n = a.shape[1] // 2
    hi = lax.bitcast_convert_type(a[:, :n].astype(BF16).astype(F32), jnp.uint32)
    lo = lax.bitcast_convert_type(a[:, n:].astype(BF16).astype(F32), jnp.uint32)
    return hi | (lo >> 16)


def _unpack_bf16_pairs(p):
    hi = lax.bitcast_convert_type(p & jnp.uint32(0xFFFF0000), F32)
    lo = lax.bitcast_convert_type(p << 16, F32)
    return jnp.concatenate([hi, lo], axis=1)


LANES = 128
D_MODEL = 1024
TOKEN_ROWS = D_MODEL // 2 // LANES


def _store_token_major(ref, packed):
    m, w = packed.shape
    s_per = w // LANES
    for s in range(s_per):
        ref[pl.ds(s, m, stride=s_per), :] = packed[:, s * LANES:(s + 1) * LANES]


def _load_token_major(ref, m, s_per):
    return jnp.concatenate([ref[pl.ds(s, m, stride=s_per), :] for s in range(s_per)], axis=1)


def _router_kernel(x_ref, g_ref, wr_ref, br_ref, h_ref, route_ref, counts_ref, carry_ref):
    @pl.when(pl.program_id(0) == 0)
    def _():
        carry_ref[...] = jnp.zeros_like(carry_ref)

    hn = _rms(x_ref[...], g_ref[...])
    _store_token_major(h_ref, _pack_bf16_pairs(hn))
    h_hi, h_lo = _split_bf16(hn)
    w_hi, w_lo = _split_bf16(wr_ref[...])
    logits = (jnp.dot(h_hi, w_hi, preferred_element_type=F32)
              + jnp.dot(h_lo, w_hi, preferred_element_type=F32)
              + jnp.dot(h_hi, w_lo, preferred_element_type=F32)) + br_ref[...]
    i1, i2, g1, g2 = _route(logits)
    tm = logits.shape[0]
    lane = lax.broadcasted_iota(jnp.int32, logits.shape, 1).astype(F32)
    chosen = ((lane == i1) | (lane == i2)).astype(BF16)
    earlier = (lax.broadcasted_iota(jnp.int32, (tm, tm), 1)
               < lax.broadcasted_iota(jnp.int32, (tm, tm), 0)).astype(BF16)
    before = jnp.dot(earlier, chosen, preferred_element_type=F32) + carry_ref[...]
    r1 = jnp.sum(jnp.where(lane == i1, before, 0.0), axis=-1, keepdims=True)
    r2 = jnp.sum(jnp.where(lane == i2, before, 0.0), axis=-1, keepdims=True)
    carry_ref[...] += jnp.sum(chosen.astype(F32), axis=0, keepdims=True)
    rec = jnp.zeros_like(logits)
    for slot, val in ((RT_E1, i1), (RT_E2, i2), (RT_G1, g1), (RT_G2, g2), (RT_R1, r1), (RT_R2, r2)):
        rec = jnp.where(lane == slot, val, rec)
    route_ref[...] = rec
    counts_ref[...] = carry_ref[...]


def _router(x, gain, w_group, b_group, w_expert, b_expert, tm):
    T, D = x.shape
    pad = ROUTER_LANES - N_EXPERTS - N_GROUPS
    wr = jnp.concatenate([w_expert, w_group, jnp.zeros((D, pad), F32)], axis=1)
    br = jnp.concatenate([b_expert, b_group, jnp.zeros((pad,), F32)]).reshape(1, ROUTER_LANES)
    return pl.pallas_call(
        _router_kernel,
        grid=(T // tm,),
        in_specs=[pl.BlockSpec((tm, D), lambda i: (i, 0)),
                  pl.BlockSpec((1, D), lambda i: (0, 0)),
                  pl.BlockSpec((D, ROUTER_LANES), lambda i: (0, 0)),
                  pl.BlockSpec((1, ROUTER_LANES), lambda i: (0, 0))],
        out_specs=[pl.BlockSpec((tm * TOKEN_ROWS, LANES), lambda i: (i, 0)),
                   pl.BlockSpec((tm, ROUTER_LANES), lambda i: (i, 0)),
                   pl.BlockSpec((1, ROUTER_LANES), lambda i: (0, 0))],
        out_shape=[jax.ShapeDtypeStruct((T * TOKEN_ROWS, LANES), jnp.uint32),
                   jax.ShapeDtypeStruct((T, ROUTER_LANES), F32),
                   jax.ShapeDtypeStruct((1, ROUTER_LANES), F32)],
        scratch_shapes=[pltpu.VMEM((1, ROUTER_LANES), F32)],
        compiler_params=_cparams("arbitrary"),
        name="moe_router",
    )(x, gain.reshape(1, D), wr, br)


MOE_TILE = 512
MOE_SLOTS = 2
DMA_UNROLL = 8


def _token_copy(src, src_row, dst, dst_row, sem):
    return pltpu.make_async_copy(src.at[pl.ds(pl.multiple_of(src_row, TOKEN_ROWS), TOKEN_ROWS)],
                                 dst.at[pl.ds(pl.multiple_of(dst_row, TOKEN_ROWS), TOKEN_ROWS)], sem)


def _dispatch_kernel(pos_ref, h_ref, zeros_hbm, xs_hbm, sem):
    del zeros_hbm
    tm = h_ref.shape[0] // TOKEN_ROWS

    def start(g, c):
        r0 = pl.multiple_of(g * DMA_UNROLL, DMA_UNROLL)
        for j in range(DMA_UNROLL):
            for k in range(MOE_SLOTS):
                _token_copy(h_ref, (r0 + j) * TOKEN_ROWS, xs_hbm, pos_ref[MOE_SLOTS * (r0 + j) + k],
                            sem).start(priority=k)
        return c

    def wait(r, c):
        for k in range(MOE_SLOTS):
            _token_copy(h_ref, 0, xs_hbm, 0, sem).wait()
        return c

    lax.fori_loop(0, tm // DMA_UNROLL, start, 0)
    lax.fori_loop(0, tm, wait, 0, unroll=DMA_UNROLL)


def _dispatch(h, pos, n_rows, tm):
    T = h.shape[0] // TOKEN_ROWS
    zeros = jnp.zeros((n_rows * TOKEN_ROWS, LANES), h.dtype)
    return pl.pallas_call(
        _dispatch_kernel,
        grid=(T // tm,),
        in_specs=[pl.BlockSpec((MOE_SLOTS * tm,), lambda i: (i,), memory_space=pltpu.SMEM),
                  pl.BlockSpec((tm * TOKEN_ROWS, LANES), lambda i: (i, 0)),
                  pl.BlockSpec(memory_space=pl.ANY)],
        out_specs=pl.BlockSpec(memory_space=pl.ANY),
        out_shape=jax.ShapeDtypeStruct(zeros.shape, h.dtype),
        scratch_shapes=[pltpu.SemaphoreType.DMA(())],
        input_output_aliases={2: 0},
        compiler_params=_cparams("arbitrary"),
        name="moe_dispatch",
    )(pos, h, zeros)


def _expert_kernel(te_ref, nv_ref, xs_ref, wg_ref, wu_ref, wd_ref, ys_ref, wgu_s, wd_s):
    i = pl.program_id(0)
    nv = nv_ref[0]
    valid = i < nv
    ic = jnp.minimum(i, nv - 1)
    changed = (i == 0) | (te_ref[ic] != te_ref[jnp.maximum(ic - 1, 0)])

    @pl.when(valid & changed)
    def _():
        wgu_s[:, :D_EXPERT] = wg_ref[0, 0].astype(BF16)
        wgu_s[:, D_EXPERT:] = wu_ref[0, 0].astype(BF16)
        wd_s[...] = wd_ref[0, 0].astype(BF16)

    @pl.when(valid)
    def _():
        x = _unpack_bf16_pairs(_load_token_major(xs_ref, MOE_TILE, TOKEN_ROWS)).astype(BF16)
        gu = jnp.dot(x, wgu_s[...], preferred_element_type=F32)
        gate, up = gu[:, :D_EXPERT], gu[:, D_EXPERT:]
        hid = (gate * jax.nn.sigmoid(gate)) * up
        y = jnp.dot(hid.astype(BF16), wd_s[...], preferred_element_type=F32)
        _store_token_major(ys_ref, _pack_bf16_pairs(y))

    @pl.when(jnp.logical_not(valid))
    def _():
        ys_ref[...] = jnp.zeros_like(ys_ref)


def _experts(xs, tile_expert, n_valid, w_gate, w_up, w_down, layer):
    D = TOKEN_ROWS * LANES * 2
    blk = MOE_TILE * TOKEN_ROWS
    n_tiles = xs.shape[0] // blk
    row = lambda i, te, nv: (jnp.minimum(i, nv[0] - 1), 0)
    wsel = lambda i, te, nv: (layer, te[jnp.minimum(i, nv[0] - 1)], 0, 0)
    return pl.pallas_call(
        _expert_kernel,
        grid_spec=pltpu.PrefetchScalarGridSpec(
            num_scalar_prefetch=2,
            grid=(n_tiles,),
            in_specs=[pl.BlockSpec((blk, LANES), row),
                      pl.BlockSpec((1, 1, D, D_EXPERT), wsel),
                      pl.BlockSpec((1, 1, D, D_EXPERT), wsel),
                      pl.BlockSpec((1, 1, D_EXPERT, D), wsel)],
            out_specs=pl.BlockSpec((blk, LANES), lambda i, te, nv: (i, 0)),
            scratch_shapes=[pltpu.VMEM((D, 2 * D_EXPERT), BF16), pltpu.VMEM((D_EXPERT, D), BF16)]),
        out_shape=jax.ShapeDtypeStruct(xs.shape, jnp.uint32),
        compiler_params=_cparams("arbitrary"),
        name="moe_experts",
    )(tile_expert, n_valid, xs, w_gate, w_up, w_down)


def _combine_kernel(pos_ref, pos_next_ref, x_ref, route_ref, ys_hbm, o_ref, buf, sems):
    i = pl.program_id(0)
    tm = x_ref.shape[0]

    def gather(p_ref, ring):
        def start(g, c):
            r0 = pl.multiple_of(g * DMA_UNROLL, DMA_UNROLL)
            for j in range(DMA_UNROLL):
                for k in range(MOE_SLOTS):
                    _token_copy(ys_hbm, p_ref[MOE_SLOTS * (r0 + j) + k], buf.at[ring, k], (r0 + j) * TOKEN_ROWS,
                                sems.at[ring]).start(priority=k)
            return c
        lax.fori_loop(0, tm // DMA_UNROLL, start, 0)

    @pl.when(i == 0)
    def _():
        gather(pos_ref, 0)

    @pl.when(i + 1 < pl.num_programs(0))
    def _():
        gather(pos_next_ref, (i + 1) % 2)

    ring = i % 2

    def wait(r, c):
        for k in range(MOE_SLOTS):
            _token_copy(ys_hbm, 0, buf.at[ring, k], 0, sems.at[ring]).wait()
        return c

    lax.fori_loop(0, tm, wait, 0, unroll=DMA_UNROLL)
    rec = route_ref[...]
    y1 = _unpack_bf16_pairs(_load_token_major(buf.at[ring, 0], tm, TOKEN_ROWS))
    y2 = _unpack_bf16_pairs(_load_token_major(buf.at[ring, 1], tm, TOKEN_ROWS))
    o_ref[...] = x_ref[...] + rec[:, RT_G1:RT_G1 + 1] * y1 + rec[:, RT_G2:RT_G2 + 1] * y2


def _combine(x, route, pos, ys, tm):
    T, D = x.shape
    n = T // tm
    return pl.pallas_call(
        _combine_kernel,
        grid=(n,),
        in_specs=[pl.BlockSpec((MOE_SLOTS * tm,), lambda i: (i,), memory_space=pltpu.SMEM),
                  pl.BlockSpec((MOE_SLOTS * tm,), lambda i: (jnp.minimum(i + 1, n - 1),), memory_space=pltpu.SMEM),
                  pl.BlockSpec((tm, D), lambda i: (i, 0)),
                  pl.BlockSpec((tm, ROUTER_LANES), lambda i: (i, 0)),
                  pl.BlockSpec(memory_space=pl.ANY)],
        out_specs=pl.BlockSpec((tm, D), lambda i: (i, 0)),
        out_shape=jax.ShapeDtypeStruct((T, D), F32),
        scratch_shapes=[pltpu.VMEM((2, MOE_SLOTS, tm * TOKEN_ROWS, LANES), ys.dtype),
                        pltpu.SemaphoreType.DMA((2,))],
        compiler_params=_cparams("arbitrary"),
        name="moe_combine",
    )(pos, pos, x, route, ys)


def _moe(x, gain, w_group, b_group, w_expert, b_expert, w_gate, w_up, w_down, layer):
    T, D = x.shape
    n_tiles = (MOE_SLOTS * T + N_EXPERTS * (MOE_TILE - 1)) // MOE_TILE + 1
    h, route, counts = _router(x, gain, w_group, b_group, w_expert, b_expert, tm=ROUTER_ROWS)
    counts = counts[0, :N_EXPERTS].astype(jnp.int32)
    padded = (counts + MOE_TILE - 1) // MOE_TILE * MOE_TILE
    ends = jnp.cumsum(padded)
    starts = ends - padded
    eid = route[:, RT_E1:RT_E2 + 1].astype(jnp.int32)
    rank = route[:, RT_R1:RT_R2 + 1].astype(jnp.int32)
    onehot = eid[..., None] == jnp.arange(N_EXPERTS, dtype=jnp.int32)
    pos = (rank + jnp.sum(jnp.where(onehot, starts, 0), axis=-1)).reshape(-1)
    pos = pos * TOKEN_ROWS
    tile_start = jnp.arange(n_tiles, dtype=jnp.int32) * MOE_TILE
    tile_expert = jnp.minimum(jnp.sum(tile_start[:, None] >= ends[None, :], axis=-1), N_EXPERTS - 1).astype(jnp.int32)
    n_valid = (ends[-1:] // MOE_TILE).astype(jnp.int32)
    xs = _dispatch(h, pos, n_tiles * MOE_TILE, tm=DISPATCH_ROWS)
    ys = _experts(xs, tile_expert, n_valid, w_gate, w_up, w_down, layer)
    return _combine(x, route, pos, ys, tm=COMBINE_ROWS)


GDN_HALO = 16


def _gdn_conv_kernel(u_ref, up_ref, un_ref, w_ref, ab_ref, aexp_ref, dtb_ref, o_ref, gb_ref, *, tiles_per_seq):
    i = pl.program_id(0)
    j = pl.program_id(1)
    tm = u_ref.shape[0]
    first = (i % tiles_per_seq) == 0
    last = (i % tiles_per_seq) == tiles_per_seq - 1
    u = u_ref[...].astype(F32)
    prev = jnp.where(first, 0.0, up_ref[...].astype(F32))
    nxt = jnp.where(last, 0.0, un_ref[...].astype(F32))
    ext = jnp.concatenate([prev, u, nxt], axis=0)
    w = w_ref[...]
    h0 = GDN_HALO - B_CONV // 2
    y = w[0:1] * ext[h0:h0 + tm]
    for t in range(1, B_CONV):
        y = y + w[t:t + 1] * ext[h0 + t:h0 + t + tm]
    y = y * jax.nn.sigmoid(y)
    qscale = jnp.where(j == 0, B_HEAD_DIM ** -0.5, 1.0)
    for h in range(B_HEADS):
        slab = y[:, h * B_HEAD_DIM:(h + 1) * B_HEAD_DIM]
        inv = lax.rsqrt(jnp.sum(slab * slab, axis=-1, keepdims=True) + RMS_EPS) * qscale
        o_ref[0, :, h * B_HEAD_DIM:(h + 1) * B_HEAD_DIM] = slab * jnp.where(j < 2, inv, 1.0)

    @pl.when(j == 0)
    def _():
        ab = ab_ref[...]
        lane = lax.broadcasted_iota(jnp.int32, ab.shape, 1)
        z = ab + dtb_ref[...]
        softplus = jnp.maximum(z, 0.0) + jnp.log1p(jnp.exp(-jnp.abs(z)))
        gb_ref[...] = jnp.where(lane < 2 * B_HEADS, -aexp_ref[...] * softplus, jax.nn.sigmoid(ab))


def _gdn_conv(qkvz, ab, conv_w, a_log, dt_bias, seq, tm):
    T = qkvz.shape[0]
    D = B_HEADS * B_HEAD_DIM
    hb = tm // GDN_HALO
    n_halo = T // GDN_HALO
    pad = ROUTER_LANES - 2 * B_HEADS
    aexp = jnp.concatenate([jnp.exp(a_log.astype(F32)).reshape(-1), jnp.zeros((pad,), F32)]).reshape(1, -1)
    dtb = jnp.concatenate([dt_bias.astype(F32).reshape(-1), jnp.zeros((pad,), F32)]).reshape(1, -1)
    return pl.pallas_call(
        functools.partial(_gdn_conv_kernel, tiles_per_seq=seq // tm),
        grid=(T // tm, 3),
        in_specs=[pl.BlockSpec((tm, D), lambda i, j: (i, j)),
                  pl.BlockSpec((GDN_HALO, D), lambda i, j: (jnp.maximum(i * hb - 1, 0), j)),
                  pl.BlockSpec((GDN_HALO, D), lambda i, j: (jnp.minimum((i + 1) * hb, n_halo - 1), j)),
                  pl.BlockSpec((B_CONV, D), lambda i, j: (0, j)),
                  pl.BlockSpec((tm, ROUTER_LANES), lambda i, j: (i, 0)),
                  pl.BlockSpec((1, ROUTER_LANES), lambda i, j: (0, 0)),
                  pl.BlockSpec((1, ROUTER_LANES), lambda i, j: (0, 0))],
        out_specs=[pl.BlockSpec((1, tm, D), lambda i, j: (j, i, 0)),
                   pl.BlockSpec((tm, ROUTER_LANES), lambda i, j: (i, 0))],
        out_shape=[jax.ShapeDtypeStruct((3, T, D), F32), jax.ShapeDtypeStruct((T, ROUTER_LANES), F32)],
        compiler_params=_cparams("parallel", "arbitrary"),
        name="gdn_conv",
    )(qkvz, qkvz, qkvz, conv_w, ab, aexp, dtb)


def _gdn_gate_terms(gb, incl):
    C = gb.shape[0]
    lane = lax.broadcasted_iota(jnp.int32, gb.shape, 1)
    g_hi, g_lo = _split_bf16(jnp.where(lane < 2 * B_HEADS, gb, 0.0))
    tri = incl.astype(BF16)
    gc = jnp.dot(tri, g_hi, preferred_element_type=F32) + jnp.dot(tri, g_lo, preferred_element_type=F32)
    return gc, jnp.concatenate([gc, jnp.zeros_like(gc)], axis=0).T


def _gdn_scan_kernel(qf_ref, kf_ref, vf_ref, gf_ref, qb_ref, kb_ref, vb_ref, gbw_ref, of_ref, ob_ref, sf_ref, sb_ref):
    @pl.when(pl.program_id(1) == 0)
    def _():
        sf_ref[...] = jnp.zeros_like(sf_ref)
        sb_ref[...] = jnp.zeros_like(sb_ref)

    C = GDN_CHUNK
    n_chunks = gf_ref.shape[0] // C
    dk = B_HEAD_DIM
    row = lax.broadcasted_iota(jnp.int32, (C, C), 0)
    colm = lax.broadcasted_iota(jnp.int32, (C, C), 1)
    tri = (row >= colm, row <= colm)
    row2 = lax.broadcasted_iota(jnp.int32, (C, 2 * C), 0)
    col2 = lax.broadcasted_iota(jnp.int32, (C, 2 * C), 1)
    left = col2 < C
    incl = (left & (row2 >= col2), left & (row2 <= col2))
    strict = (left & (row2 > col2), left & (row2 < col2))
    eye_right = (col2 == row2 + C).astype(F32)
    qkv_refs = ((qf_ref, kf_ref, vf_ref), (qb_ref, kb_ref, vb_ref))
    s_refs = (sf_ref, sb_ref)
    o_refs = (of_ref, ob_ref)

    def chunk_step(c, carry):
        rows = (pl.ds(pl.multiple_of(c * C, C), C), pl.ds(pl.multiple_of((n_chunks - 1 - c) * C, C), C))
        _gdn_chunk_pair(rows, (gf_ref, gbw_ref), qkv_refs, s_refs, o_refs, tri, incl, strict, left, eye_right)
        return carry

    lax.fori_loop(0, n_chunks, chunk_step, 0)


def _gdn_chunk_pair(rows, g_refs, qkv_refs, s_refs, o_refs, tri, incl, strict, left, eye_right):
    C = GDN_CHUNK
    dk = B_HEAD_DIM
    gbs = tuple(g_refs[d][rows[d], :] for d in range(2))
    gates = [_gdn_gate_terms(gbs[d], tri[d]) for d in range(2)]
    glast = [gates[0][0][C - 1:C], gates[1][0][0:1]]
    units = [(d, h) for d in range(2) for h in range(B_HEADS)]

    def lane_of(d, h):
        return d * B_HEADS + h

    def cols(h):
        return slice(h * dk, (h + 1) * dk)

    v_b, kb_l, qd_bf, kd_bf, a_l, dec_l, egc_l = [], [], [], [], [], [], []
    for d, h in units:
        r = lane_of(d, h)
        gc, gct = gates[d]
        gcol = gc[:, r:r + 1]
        beta = gbs[d][:, 2 * B_HEADS + r:2 * B_HEADS + r + 1]
        q_ref, k_ref, v_ref = qkv_refs[d]
        qh, kh, vh = q_ref[0, rows[d], cols(h)], k_ref[0, rows[d], cols(h)], v_ref[0, rows[d], cols(h)]
        egc = jnp.exp(gcol)
        kb = kh * beta
        khb = jnp.concatenate([kh.astype(BF16), jnp.zeros((C, dk), BF16)], axis=0)
        a_l.append(_bdot_nt(jnp.concatenate([kb, qh], axis=0), khb))
        dec_l.append(jnp.exp(jnp.where(incl[d], gcol - gct[r:r + 1, :], NEG_INF)))
        v_b.append(vh * beta)
        kb_l.append(kb)
        egc_l.append(egc)
        qd_bf.append((qh * egc).astype(BF16))
        kd_bf.append((kh * jnp.exp(glast[d][:, r:r + 1] - gcol)).astype(BF16))
    r_l = [eye_right - jnp.where(strict[d], a[:C] * dec, 0.0)
           for (d, h), a, dec in zip(units, a_l, dec_l)]
    intra_bf = [(a[C:] * dec)[:, :C].astype(BF16) for a, dec in zip(a_l, dec_l)]
    x_l = [r[:, :C] for r in r_l]
    n = 1
    while n < C:
        o_l = [_bdot(x, r) for x, r in zip(x_l, r_l)]
        r_l = [o + jnp.where(left, 0.0, r) for o, r in zip(o_l, r_l)]
        x_l = [o[:, :C] for o in o_l]
        n *= 2
    zpad = jnp.zeros((C, 2 * dk), BF16)
    sol_l = [_bdot(r, jnp.concatenate([zpad, jnp.concatenate([vb, kb * egc], axis=1).astype(BF16)], axis=0))
             for r, vb, kb, egc in zip(r_l, v_b, kb_l, egc_l)]
    st_l = [s_refs[d][h] for d, h in units]
    wq_l = [_bdot(jnp.concatenate([sol[:, dk:].astype(BF16), qd], axis=0), st)
            for sol, qd, st in zip(sol_l, qd_bf, st_l)]
    vn_l = [sol[:, :dk] - wq[:C] for sol, wq in zip(sol_l, wq_l)]
    for (d, h), wq, intra, vn in zip(units, wq_l, intra_bf, vn_l):
        o_refs[d][rows[d], cols(h)] = (wq[C:] + _bdot(intra, vn)).astype(o_refs[d].dtype)
    for (d, h), st, kd, vn in zip(units, st_l, kd_bf, vn_l):
        r = lane_of(d, h)
        s_refs[d][h] = st * jnp.exp(glast[d][:, r:r + 1]) + _bdot_tn(kd, vn)


def _gdn_scan(qkv, gb, batch, seq, rows_per_step):
    _, T, D = qkv.shape
    chunk = rows_per_step
    nc = seq // chunk
    fwd = lambda b, c: b * nc + c
    bwd = lambda b, c: b * nc + (nc - 1 - c)
    part = lambda p, f: pl.BlockSpec((1, chunk, D), lambda b, c: (p, f(b, c), 0))
    gspec = lambda f: pl.BlockSpec((chunk, ROUTER_LANES), lambda b, c: (f(b, c), 0))
    ospec = lambda f: pl.BlockSpec((chunk, D), lambda b, c: (f(b, c), 0))
    return pl.pallas_call(
        _gdn_scan_kernel,
        grid=(batch, nc),
        in_specs=[part(0, fwd), part(1, fwd), part(2, fwd), gspec(fwd),
                  part(0, bwd), part(1, bwd), part(2, bwd), gspec(bwd)],
        out_specs=[ospec(fwd), ospec(bwd)],
        out_shape=[jax.ShapeDtypeStruct((T, D), BF16), jax.ShapeDtypeStruct((T, D), BF16)],
        scratch_shapes=[pltpu.VMEM((B_HEADS, B_HEAD_DIM, B_HEAD_DIM), F32),
                        pltpu.VMEM((B_HEADS, B_HEAD_DIM, B_HEAD_DIM), F32)],
        compiler_params=_cparams("parallel", "arbitrary"),
        name="gdn_scan",
    )(qkv, qkv, qkv, gb, qkv, qkv, qkv, gb)


def _gdn_out_kernel(of_ref, ob_ref, z_ref, og_ref, w_ref, x_ref, o_ref, a_ref):
    @pl.when(pl.program_id(1) == 0)
    def _():
        o = of_ref[...].astype(F32) + ob_ref[...].astype(F32)
        z = z_ref[...].astype(F32)
        for h in range(B_HEADS):
            sl = slice(h * B_HEAD_DIM, (h + 1) * B_HEAD_DIM)
            zh = z[:, sl]
            a_ref[:, sl] = (_rms(o[:, sl], og_ref[...]) * (zh * jax.nn.sigmoid(zh))).astype(BF16)

    o_ref[...] = x_ref[...] + jnp.dot(a_ref[...], w_ref[...], preferred_element_type=F32)


def _gdn_out(o_f, o_b, qkvz, o_gain, w, x, tm, tn):
    T, D = x.shape
    return pl.pallas_call(
        _gdn_out_kernel,
        grid=(T // tm, D // tn),
        in_specs=[pl.BlockSpec((tm, D), lambda i, j: (i, 0)),
                  pl.BlockSpec((tm, D), lambda i, j: (i, 0)),
                  pl.BlockSpec((tm, D), lambda i, j: (i, 3)),
                  pl.BlockSpec((1, B_HEAD_DIM), lambda i, j: (0, 0)),
                  pl.BlockSpec((D, tn), lambda i, j: (0, j)),
                  pl.BlockSpec((tm, tn), lambda i, j: (i, j))],
        out_specs=pl.BlockSpec((tm, tn), lambda i, j: (i, j)),
        out_shape=jax.ShapeDtypeStruct((T, D), F32),
        scratch_shapes=[pltpu.VMEM((tm, D), BF16)],
        compiler_params=_cparams("parallel", "arbitrary"),
        name="gdn_out",
    )(o_f, o_b, qkvz, o_gain.reshape(1, B_HEAD_DIM), w, x)


def _attention_layer(x, gain, w_in, q_gain, k_gain, sink, w_out, batch, seq):
    qkv = _norm_matmul(x, gain, w_in.astype(BF16), *QKV_PROJ_TILE)
    a = _attention(qkv, q_gain, k_gain, sink, batch, seq)
    return _matmul_residual(a, w_out.astype(BF16), x, *ATTN_OUT_TILE)


def _gdn_layer(x, gain, w_in, conv_w, a_log, dt_bias, o_gain, w_out, batch, seq):
    D = x.shape[1]
    w_main = w_in[:, :4 * D].astype(BF16)
    pad = ROUTER_LANES - 4 * B_HEADS
    w_ab = jnp.concatenate([w_in[:, 4 * D:], jnp.zeros((D, pad), F32)], axis=1)
    qkvz, ab = _norm_matmul2(x, gain, w_main, w_ab, *GDN_PROJ_TILE)
    qkv, gb = _gdn_conv(qkvz, ab, conv_w, a_log, dt_bias, seq, GDN_CONV_ROWS)
    o_f, o_b = _gdn_scan(qkv, gb, batch, seq, GDN_SCAN_CHUNKS * GDN_CHUNK)
    return _gdn_out(o_f, o_b, qkvz, o_gain, w_out.astype(BF16), x, *GDN_OUT_TILE)


def kernel(x, norm_mix, norm_ffn, attn_w_in, attn_q_gain, attn_k_gain, attn_sink, attn_w_out, gdn_w_in, gdn_conv, gdn_a_log, gdn_dt_bias, gdn_o_gain, gdn_w_out, moe_w_group, moe_b_group, moe_w_expert, moe_b_expert, moe_w_gate, moe_w_up, moe_w_down):
    batch, seq, d_model = x.shape
    depth = norm_mix.shape[0]
    xt = x.reshape(batch * seq, d_model)
    for i in range(depth):
        j = i // 2
        if i % 2 == 0:
            xt = _attention_layer(xt, norm_mix[i], attn_w_in[j], attn_q_gain[j], attn_k_gain[j],
                                  attn_sink[j], attn_w_out[j], batch, seq)
        else:
            xt = _gdn_layer(xt, norm_mix[i], gdn_w_in[j], gdn_conv[j], gdn_a_log[j], gdn_dt_bias[j],
                            gdn_o_gain[j], gdn_w_out[j], batch, seq)
        xt = _moe(xt, norm_ffn[i], moe_w_group[i], moe_b_group[i], moe_w_expert[i], moe_b_expert[i],
                  moe_w_gate, moe_w_up, moe_w_down, layer=i)
    return xt.reshape(batch, seq, d_model)
```

```python
import functools
import math

import jax
import jax.numpy as jnp
import numpy as np
from jax import lax
from jax.experimental import pallas as pl
from jax.experimental.pallas import tpu as pltpu

RMS_EPS = 1e-6
NEG_INF = -1e30
F32 = jnp.float32
BF16 = jnp.bfloat16

A_HEADS = 16
A_KV_HEADS = 4
A_HEAD_DIM = 64
A_REP = A_HEADS // A_KV_HEADS
A_BLOCK = 128
B_HEADS = 8
B_HEAD_DIM = 128
B_CONV = 4
GDN_CHUNK = 64
N_GROUPS = 4
EXPERTS_PER_GROUP = 8
N_EXPERTS = 32
D_EXPERT = 256
ROUTER_LANES = 128

V7X_VMEM_LIMIT_BYTES = 56 * 1024 * 1024

QKV_PROJ_TILE = (1024, 1536)
ATTN_QBLOCKS = 4
ATTN_OUT_TILE = (1024, 1024)
GDN_PROJ_TILE = (1024, 2048)
GDN_CONV_ROWS = 512
GDN_SCAN_CHUNKS = 4
GDN_OUT_TILE = (512, 1024)
ROUTER_ROWS = 512
DISPATCH_ROWS = 1024
COMBINE_ROWS = 512


def _cparams(*sem):
    return pltpu.CompilerParams(dimension_semantics=sem, vmem_limit_bytes=V7X_VMEM_LIMIT_BYTES)


def _bdot(a, b):
    return jnp.dot(a.astype(BF16), b.astype(BF16), preferred_element_type=F32)


def _bdot_nt(a, b):
    return lax.dot_general(a.astype(BF16), b.astype(BF16), (((1,), (1,)), ((), ())),
                           preferred_element_type=F32)


def _bdot_tn(a, b):
    return lax.dot_general(a.astype(BF16), b.astype(BF16), (((0,), (0,)), ((), ())),
                           preferred_element_type=F32)


def _split_bf16(a):
    hi = a.astype(BF16)
    lo = (a - hi.astype(F32)).astype(BF16)
    return hi, lo


def _rms(x, gain):
    return x * lax.rsqrt(jnp.mean(x * x, axis=-1, keepdims=True) + RMS_EPS) * gain


def _norm_matmul_kernel(x_ref, g_ref, w_ref, o_ref, xn_ref):
    @pl.when(pl.program_id(1) == 0)
    def _():
        xn_ref[...] = _rms(x_ref[...], g_ref[...]).astype(BF16)

    o_ref[...] = jnp.dot(xn_ref[...], w_ref[...], preferred_element_type=F32).astype(o_ref.dtype)


def _norm_matmul(x, gain, w, tm, tn):
    T, D = x.shape
    N = w.shape[1]
    return pl.pallas_call(
        _norm_matmul_kernel,
        grid=(T // tm, N // tn),
        in_specs=[pl.BlockSpec((tm, D), lambda i, j: (i, 0)),
                  pl.BlockSpec((1, D), lambda i, j: (0, 0)),
                  pl.BlockSpec((D, tn), lambda i, j: (0, j))],
        out_specs=pl.BlockSpec((tm, tn), lambda i, j: (i, j)),
        out_shape=jax.ShapeDtypeStruct((T, N), BF16),
        scratch_shapes=[pltpu.VMEM((tm, D), BF16)],
        compiler_params=_cparams("parallel", "arbitrary"),
        name="norm_matmul",
    )(x, gain.reshape(1, D), w)


def _norm_matmul2_kernel(x_ref, g_ref, w_ref, w2_ref, o_ref, o2_ref, xn_ref):
    @pl.when(pl.program_id(1) == 0)
    def _():
        xn = _rms(x_ref[...], g_ref[...])
        xn_ref[...] = xn.astype(BF16)
        x_hi, x_lo = _split_bf16(xn)
        w_hi, w_lo = _split_bf16(w2_ref[...])
        o2_ref[...] = (jnp.dot(x_hi, w_hi, preferred_element_type=F32)
                       + jnp.dot(x_lo, w_hi, preferred_element_type=F32)
                       + jnp.dot(x_hi, w_lo, preferred_element_type=F32))

    o_ref[...] = jnp.dot(xn_ref[...], w_ref[...], preferred_element_type=F32).astype(o_ref.dtype)


def _norm_matmul2(x, gain, w, w2, tm, tn):
    T, D = x.shape
    N = w.shape[1]
    N2 = w2.shape[1]
    return pl.pallas_call(
        _norm_matmul2_kernel,
        grid=(T // tm, N // tn),
        in_specs=[pl.BlockSpec((tm, D), lambda i, j: (i, 0)),
                  pl.BlockSpec((1, D), lambda i, j: (0, 0)),
                  pl.BlockSpec((D, tn), lambda i, j: (0, j)),
                  pl.BlockSpec((D, N2), lambda i, j: (0, 0))],
        out_specs=[pl.BlockSpec((tm, tn), lambda i, j: (i, j)),
                   pl.BlockSpec((tm, N2), lambda i, j: (i, 0))],
        out_shape=[jax.ShapeDtypeStruct((T, N), BF16), jax.ShapeDtypeStruct((T, N2), F32)],
        scratch_shapes=[pltpu.VMEM((tm, D), BF16)],
        compiler_params=_cparams("parallel", "arbitrary"),
        name="norm_matmul2",
    )(x, gain.reshape(1, D), w, w2)


def _matmul_residual_kernel(a_ref, w_ref, x_ref, o_ref):
    o_ref[...] = x_ref[...] + jnp.dot(a_ref[...], w_ref[...], preferred_element_type=F32)


def _matmul_residual(a, w, x, tm, tn):
    T, K = a.shape
    N = w.shape[1]
    return pl.pallas_call(
        _matmul_residual_kernel,
        grid=(T // tm, N // tn),
        in_specs=[pl.BlockSpec((tm, K), lambda i, j: (i, 0)),
                  pl.BlockSpec((K, tn), lambda i, j: (0, j)),
                  pl.BlockSpec((tm, tn), lambda i, j: (i, j))],
        out_specs=pl.BlockSpec((tm, tn), lambda i, j: (i, j)),
        out_shape=jax.ShapeDtypeStruct((T, N), F32),
        compiler_params=_cparams("parallel", "arbitrary"),
        name="matmul_residual",
    )(a, w, x)


def _attn_kernel(main_ref, prev_ref, next_ref, qg_ref, kg_ref, sink_ref, bias_ref, o_ref):
    n = pl.program_id(1)
    nb = pl.num_programs(1)
    dh, blk = A_HEAD_DIM, A_BLOCK
    kv_cols = A_KV_HEADS * dh
    qb = main_ref.shape[0] // blk
    main = main_ref[...].astype(F32)
    kv = jnp.concatenate([prev_ref[...].astype(F32), main[:, A_HEADS * dh:], next_ref[...].astype(F32)],
                         axis=0)
    col = lax.broadcasted_iota(jnp.int32, (1, 3 * blk), 1)
    no_prev = jnp.where((col < blk) & (n == 0), NEG_INF, 0.0)
    no_next = jnp.where((col >= 2 * blk) & (n == nb - 1), NEG_INF, 0.0)
    edge = [(no_prev if j == 0 else 0.0) + (no_next if j == qb - 1 else 0.0) for j in range(qb)]
    qg = qg_ref[...] * (dh ** -0.5)
    groups = range(A_KV_HEADS)
    units = [(j, g) for j in range(qb) for g in groups]
    kn = [_rms(kv[:, g * dh:(g + 1) * dh], kg_ref[...]).astype(BF16) for g in groups]
    vb = [kv[:, kv_cols + g * dh: kv_cols + (g + 1) * dh].astype(BF16) for g in groups]
    q4 = [jnp.concatenate([_rms(main[j * blk:(j + 1) * blk, (A_REP * g + r) * dh:(A_REP * g + r + 1) * dh],
                                qg).astype(BF16) for r in range(A_REP)], axis=0) for j, g in units]
    s = [_bdot_nt(q, kn[g][j * blk:(j + 3) * blk]) + bias_ref[g] + edge[j] for q, (j, g) in zip(q4, units)]
    m = [jnp.maximum(jnp.max(su, axis=-1, keepdims=True), sink_ref[g]) for su, (j, g) in zip(s, units)]
    p = [jnp.exp(su - mu) for su, mu in zip(s, m)]
    denom = [jnp.sum(pu, axis=-1, keepdims=True) + jnp.exp(sink_ref[g] - mu) for pu, mu, (j, g) in zip(p, m, units)]
    o = [jnp.dot(pu.astype(BF16), vb[g][j * blk:(j + 3) * blk], preferred_element_type=F32) / du
         for pu, du, (j, g) in zip(p, denom, units)]
    for ou, (j, g) in zip(o, units):
        for r in range(A_REP):
            h = A_REP * g + r
            o_ref[j * blk:(j + 1) * blk, h * dh:(h + 1) * dh] = ou[r * blk:(r + 1) * blk].astype(o_ref.dtype)


def _attn_tables(sink):
    blk = A_BLOCK
    slopes = np.array([2.0 ** (-8.0 * (h + 1) / A_HEADS) for h in range(A_HEADS)], np.float32)
    qi = np.arange(blk)
    kj = np.arange(3 * blk)
    dist = np.abs(blk + qi[:, None] - kj[None, :]).astype(np.float32)
    bias = np.where(dist[None] <= blk, -slopes[:, None, None] * dist[None], np.float32(NEG_INF))
    bias = bias.astype(np.float32).reshape(A_KV_HEADS, A_REP * blk, 3 * blk)
    sink_rows = jnp.repeat(sink.astype(F32).reshape(A_KV_HEADS, A_REP), blk, axis=1)[..., None]
    return jnp.asarray(bias), sink_rows


def _attention(qkv, q_gain, k_gain, sink, batch, seq):
    T, W = qkv.shape
    blk, dh = A_BLOCK, A_HEAD_DIM
    nb = seq // blk
    kv_w = 2 * A_KV_HEADS * dh
    kv_blk = (A_HEADS * dh) // kv_w
    bias, sink_rows = _attn_tables(sink)
    qb = ATTN_QBLOCKS
    steps = nb // qb
    return pl.pallas_call(
        _attn_kernel,
        grid=(batch, steps),
        in_specs=[
            pl.BlockSpec((qb * blk, W), lambda b, n: (b * steps + n, 0)),
            pl.BlockSpec((blk, kv_w), lambda b, n: (b * nb + jnp.maximum(n * qb - 1, 0), kv_blk)),
            pl.BlockSpec((blk, kv_w), lambda b, n: (b * nb + jnp.minimum(n * qb + qb, nb - 1), kv_blk)),
            pl.BlockSpec((1, dh), lambda b, n: (0, 0)),
            pl.BlockSpec((1, dh), lambda b, n: (0, 0)),
            pl.BlockSpec((A_KV_HEADS, A_REP * blk, 1), lambda b, n: (0, 0, 0)),
            pl.BlockSpec((A_KV_HEADS, A_REP * blk, 3 * blk), lambda b, n: (0, 0, 0)),
        ],
        out_specs=pl.BlockSpec((qb * blk, A_HEADS * dh), lambda b, n: (b * steps + n, 0)),
        out_shape=jax.ShapeDtypeStruct((T, A_HEADS * dh), BF16),
        compiler_params=_cparams("parallel", "parallel"),
        name="window_attention",
    )(qkv, qkv, qkv, q_gain.reshape(1, dh), k_gain.reshape(1, dh), sink_rows, bias)


def _route(logits):
    lane = lax.broadcasted_iota(jnp.int32, logits.shape, 1).astype(F32)
    big = jnp.float32(1e9)
    is_g = (lane >= N_EXPERTS) & (lane < N_EXPERTS + N_GROUPS)
    lg = jnp.where(is_g, logits, NEG_INF)
    gmax = jnp.max(lg, axis=-1, keepdims=True)
    gidx = jnp.min(jnp.where(is_g & (lg == gmax), lane, big), axis=-1, keepdims=True) - N_EXPERTS
    g_prob = 1.0 / jnp.sum(jnp.where(is_g, jnp.exp(lg - gmax), 0.0), axis=-1, keepdims=True)
    lo = gidx * EXPERTS_PER_GROUP
    in_grp = (lane >= lo) & (lane < lo + EXPERTS_PER_GROUP)
    le = jnp.where(in_grp, logits, NEG_INF)
    emax = jnp.max(le, axis=-1, keepdims=True)
    ex = jnp.where(in_grp, jnp.exp(le - emax), 0.0)
    prob = ex / jnp.sum(ex, axis=-1, keepdims=True)
    cand = jnp.where(in_grp, prob, -1.0)
    p1 = jnp.max(cand, axis=-1, keepdims=True)
    i1 = jnp.min(jnp.where(cand == p1, lane, big), axis=-1, keepdims=True)
    cand2 = jnp.where(lane == i1, -1.0, cand)
    p2 = jnp.max(cand2, axis=-1, keepdims=True)
    i2 = jnp.min(jnp.where(cand2 == p2, lane, big), axis=-1, keepdims=True)
    scale = g_prob / (p1 + p2)
    return i1, i2, p1 * scale, p2 * scale


RT_E1, RT_E2, RT_G1, RT_G2, RT_R1, RT_R2 = range(6)


def _pack_bf16_pairs(a):
    n = a.shape[1] // 2
    hi = lax.bitcast_convert_type(a[:, :n].astype(BF16).astype(F32), jnp.uint32)
    lo = lax.bitcast_convert_type(a[:, n:].astype(BF16).astype(F32), jnp.uint32)
    return hi | (lo >> 16)


def _unpack_bf16_pairs(p):
    hi = lax.bitcast_convert_type(p & jnp.uint32(0xFFFF0000), F32)
    lo = lax.bitcast_convert_type(p << 16, F32)
    return jnp.concatenate([hi, lo], axis=1)


LANES = 128
D_MODEL = 1024
TOKEN_ROWS = D_MODEL // 2 // LANES


def _store_token_major(ref, packed):
    m, w = packed.shape
    s_per = w // LANES
    for s in range(s_per):
        ref[pl.ds(s, m, stride=s_per), :] = packed[:, s * LANES:(s + 1) * LANES]


def _load_token_major(ref, m, s_per):
    return jnp.concatenate([ref[pl.ds(s, m, stride=s_per), :] for s in range(s_per)], axis=1)


def _router_kernel(x_ref, g_ref, wr_ref, br_ref, h_ref, route_ref, counts_ref, carry_ref):
    @pl.when(pl.program_id(0) == 0)
    def _():
        carry_ref[...] = jnp.zeros_like(carry_ref)

    hn = _rms(x_ref[...], g_ref[...])
    _store_token_major(h_ref, _pack_bf16_pairs(hn))
    h_hi, h_lo = _split_bf16(hn)
    w_hi, w_lo = _split_bf16(wr_ref[...])
    logits = (jnp.dot(h_hi, w_hi, preferred_element_type=F32)
              + jnp.dot(h_lo, w_hi, preferred_element_type=F32)
              + jnp.dot(h_hi, w_lo, preferred_element_type=F32)) + br_ref[...]
    i1, i2, g1, g2 = _route(logits)
    tm = logits.shape[0]
    lane = lax.broadcasted_iota(jnp.int32, logits.shape, 1).astype(F32)
    chosen = ((lane == i1) | (lane == i2)).astype(BF16)
    earlier = (lax.broadcasted_iota(jnp.int32, (tm, tm), 1)
               < lax.broadcasted_iota(jnp.int32, (tm, tm), 0)).astype(BF16)
    before = jnp.dot(earlier, chosen, preferred_element_type=F32) + carry_ref[...]
    r1 = jnp.sum(jnp.where(lane == i1, before, 0.0), axis=-1, keepdims=True)
    r2 = jnp.sum(jnp.where(lane == i2, before, 0.0), axis=-1, keepdims=True)
    carry_ref[...] += jnp.sum(chosen.astype(F32), axis=0, keepdims=True)
    rec = jnp.zeros_like(logits)
    for slot, val in ((RT_E1, i1), (RT_E2, i2), (RT_G1, g1), (RT_G2, g2), (RT_R1, r1), (RT_R2, r2)):
        rec = jnp.where(lane == slot, val, rec)
    route_ref[...] = rec
    counts_ref[...] = carry_ref[...]


def _router(x, gain, w_group, b_group, w_expert, b_expert, tm):
    T, D = x.shape
    pad = ROUTER_LANES - N_EXPERTS - N_GROUPS
    wr = jnp.concatenate([w_expert, w_group, jnp.zeros((D, pad), F32)], axis=1)
    br = jnp.concatenate([b_expert, b_group, jnp.zeros((pad,), F32)]).reshape(1, ROUTER_LANES)
    return pl.pallas_call(
        _router_kernel,
        grid=(T // tm,),
        in_specs=[pl.BlockSpec((tm, D), lambda i: (i, 0)),
                  pl.BlockSpec((1, D), lambda i: (0, 0)),
                  pl.BlockSpec((D, ROUTER_LANES), lambda i: (0, 0)),
                  pl.BlockSpec((1, ROUTER_LANES), lambda i: (0, 0))],
        out_specs=[pl.BlockSpec((tm * TOKEN_ROWS, LANES), lambda i: (i, 0)),
                   pl.BlockSpec((tm, ROUTER_LANES), lambda i: (i, 0)),
                   pl.BlockSpec((1, ROUTER_LANES), lambda i: (0, 0))],
        out_shape=[jax.ShapeDtypeStruct((T * TOKEN_ROWS, LANES), jnp.uint32),
                   jax.ShapeDtypeStruct((T, ROUTER_LANES), F32),
                   jax.ShapeDtypeStruct((1, ROUTER_LANES), F32)],
        scratch_shapes=[pltpu.VMEM((1, ROUTER_LANES), F32)],
        compiler_params=_cparams("arbitrary"),
        name="moe_router",
    )(x, gain.reshape(1, D), wr, br)


MOE_TILE = 512
MOE_SLOTS = 2
DMA_UNROLL = 8


def _token_copy(src, src_row, dst, dst_row, sem):
    return pltpu.make_async_copy(src.at[pl.ds(pl.multiple_of(src_row, TOKEN_ROWS), TOKEN_ROWS)],
                                 dst.at[pl.ds(pl.multiple_of(dst_row, TOKEN_ROWS), TOKEN_ROWS)], sem)


def _dispatch_kernel(pos_ref, h_ref, zeros_hbm, xs_hbm, sem):
    del zeros_hbm
    tm = h_ref.shape[0] // TOKEN_ROWS

    def start(g, c):
        r0 = pl.multiple_of(g * DMA_UNROLL, DMA_UNROLL)
        for j in range(DMA_UNROLL):
            for k in range(MOE_SLOTS):
                _token_copy(h_ref, (r0 + j) * TOKEN_ROWS, xs_hbm, pos_ref[MOE_SLOTS * (r0 + j) + k],
                            sem).start(priority=k)
        return c

    def wait(r, c):
        for k in range(MOE_SLOTS):
            _token_copy(h_ref, 0, xs_hbm, 0, sem).wait()
        return c

    lax.fori_loop(0, tm // DMA_UNROLL, start, 0)
    lax.fori_loop(0, tm, wait, 0, unroll=DMA_UNROLL)


def _dispatch(h, pos, n_rows, tm):
    T = h.shape[0] // TOKEN_ROWS
    zeros = jnp.zeros((n_rows * TOKEN_ROWS, LANES), h.dtype)
    return pl.pallas_call(
        _dispatch_kernel,
        grid=(T // tm,),
        in_specs=[pl.BlockSpec((MOE_SLOTS * tm,), lambda i: (i,), memory_space=pltpu.SMEM),
                  pl.BlockSpec((tm * TOKEN_ROWS, LANES), lambda i: (i, 0)),
                  pl.BlockSpec(memory_space=pl.ANY)],
        out_specs=pl.BlockSpec(memory_space=pl.ANY),
        out_shape=jax.ShapeDtypeStruct(zeros.shape, h.dtype),
        scratch_shapes=[pltpu.SemaphoreType.DMA(())],
        input_output_aliases={2: 0},
        compiler_params=_cparams("arbitrary"),
        name="moe_dispatch",
    )(pos, h, zeros)


def _expert_kernel(te_ref, nv_ref, xs_ref, wg_ref, wu_ref, wd_ref, ys_ref, wgu_s, wd_s):
    i = pl.program_id(0)
    nv = nv_ref[0]
    valid = i < nv
    ic = jnp.minimum(i, nv - 1)
    changed = (i == 0) | (te_ref[ic] != te_ref[jnp.maximum(ic - 1, 0)])

    @pl.when(valid & changed)
    def _():
        wgu_s[:, :D_EXPERT] = wg_ref[0, 0].astype(BF16)
        wgu_s[:, D_EXPERT:] = wu_ref[0, 0].astype(BF16)
        wd_s[...] = wd_ref[0, 0].astype(BF16)

    @pl.when(valid)
    def _():
        x = _unpack_bf16_pairs(_load_token_major(xs_ref, MOE_TILE, TOKEN_ROWS)).astype(BF16)
        gu = jnp.dot(x, wgu_s[...], preferred_element_type=F32)
        gate, up = gu[:, :D_EXPERT], gu[:, D_EXPERT:]
        hid = (gate * jax.nn.sigmoid(gate)) * up
        y = jnp.dot(hid.astype(BF16), wd_s[...], preferred_element_type=F32)
        _store_token_major(ys_ref, _pack_bf16_pairs(y))

    @pl.when(jnp.logical_not(valid))
    def _():
        ys_ref[...] = jnp.zeros_like(ys_ref)


def _experts(xs, tile_expert, n_valid, w_gate, w_up, w_down, layer):
    D = TOKEN_ROWS * LANES * 2
    blk = MOE_TILE * TOKEN_ROWS
    n_tiles = xs.shape[0] // blk
    row = lambda i, te, nv: (jnp.minimum(i, nv[0] - 1), 0)
    wsel = lambda i, te, nv: (layer, te[jnp.minimum(i, nv[0] - 1)], 0, 0)
    return pl.pallas_call(
        _expert_kernel,
        grid_spec=pltpu.PrefetchScalarGridSpec(
            num_scalar_prefetch=2,
            grid=(n_tiles,),
            in_specs=[pl.BlockSpec((blk, LANES), row),
                      pl.BlockSpec((1, 1, D, D_EXPERT), wsel),
                      pl.BlockSpec((1, 1, D, D_EXPERT), wsel),
                      pl.BlockSpec((1, 1, D_EXPERT, D), wsel)],
            out_specs=pl.BlockSpec((blk, LANES), lambda i, te, nv: (i, 0)),
            scratch_shapes=[pltpu.VMEM((D, 2 * D_EXPERT), BF16), pltpu.VMEM((D_EXPERT, D), BF16)]),
        out_shape=jax.ShapeDtypeStruct(xs.shape, jnp.uint32),
        compiler_params=_cparams("arbitrary"),
        name="moe_experts",
    )(tile_expert, n_valid, xs, w_gate, w_up, w_down)


def _combine_kernel(pos_ref, pos_next_ref, x_ref, route_ref, ys_hbm, o_ref, buf, sems):
    i = pl.program_id(0)
    tm = x_ref.shape[0]

    def gather(p_ref, ring):
        def start(g, c):
            r0 = pl.multiple_of(g * DMA_UNROLL, DMA_UNROLL)
            for j in range(DMA_UNROLL):
                for k in range(MOE_SLOTS):
                    _token_copy(ys_hbm, p_ref[MOE_SLOTS * (r0 + j) + k], buf.at[ring, k], (r0 + j) * TOKEN_ROWS,
                                sems.at[ring]).start(priority=k)
            return c
        lax.fori_loop(0, tm // DMA_UNROLL, start, 0)

    @pl.when(i == 0)
    def _():
        gather(pos_ref, 0)

    @pl.when(i + 1 < pl.num_programs(0))
    def _():
        gather(pos_next_ref, (i + 1) % 2)

    ring = i % 2

    def wait(r, c):
        for k in range(MOE_SLOTS):
            _token_copy(ys_hbm, 0, buf.at[ring, k], 0, sems.at[ring]).wait()
        return c

    lax.fori_loop(0, tm, wait, 0, unroll=DMA_UNROLL)
    rec = route_ref[...]
    y1 = _unpack_bf16_pairs(_load_token_major(buf.at[ring, 0], tm, TOKEN_ROWS))
    y2 = _unpack_bf16_pairs(_load_token_major(buf.at[ring, 1], tm, TOKEN_ROWS))
    o_ref[...] = x_ref[...] + rec[:, RT_G1:RT_G1 + 1] * y1 + rec[:, RT_G2:RT_G2 + 1] * y2


def _combine(x, route, pos, ys, tm):
    T, D = x.shape
    n = T // tm
    return pl.pallas_call(
        _combine_kernel,
        grid=(n,),
        in_specs=[pl.BlockSpec((MOE_SLOTS * tm,), lambda i: (i,), memory_space=pltpu.SMEM),
                  pl.BlockSpec((MOE_SLOTS * tm,), lambda i: (jnp.minimum(i + 1, n - 1),), memory_space=pltpu.SMEM),
                  pl.BlockSpec((tm, D), lambda i: (i, 0)),
                  pl.BlockSpec((tm, ROUTER_LANES), lambda i: (i, 0)),
                  pl.BlockSpec(memory_space=pl.ANY)],
        out_specs=pl.BlockSpec((tm, D), lambda i: (i, 0)),
        out_shape=jax.ShapeDtypeStruct((T, D), F32),
        scratch_shapes=[pltpu.VMEM((2, MOE_SLOTS, tm * TOKEN_ROWS, LANES), ys.dtype),
                        pltpu.SemaphoreType.DMA((2,))],
        compiler_params=_cparams("arbitrary"),
        name="moe_combine",
    )(pos, pos, x, route, ys)


def _moe(x, gain, w_group, b_group, w_expert, b_expert, w_gate, w_up, w_down, layer):
    T, D = x.shape
    n_tiles = (MOE_SLOTS * T + N_EXPERTS * (MOE_TILE - 1)) // MOE_TILE + 1
    h, route, counts = _router(x, gain, w_group, b_group, w_expert, b_expert, tm=ROUTER_ROWS)
    counts = counts[0, :N_EXPERTS].astype(jnp.int32)
    padded = (counts + MOE_TILE - 1) // MOE_TILE * MOE_TILE
    ends = jnp.cumsum(padded)
    starts = ends - padded
    eid = route[:, RT_E1:RT_E2 + 1].astype(jnp.int32)
    rank = route[:, RT_R1:RT_R2 + 1].astype(jnp.int32)
    onehot = eid[..., None] == jnp.arange(N_EXPERTS, dtype=jnp.int32)
    pos = (rank + jnp.sum(jnp.where(onehot, starts, 0), axis=-1)).reshape(-1)
    pos = pos * TOKEN_ROWS
    tile_start = jnp.arange(n_tiles, dtype=jnp.int32) * MOE_TILE
    tile_expert = jnp.minimum(jnp.sum(tile_start[:, None] >= ends[None, :], axis=-1), N_EXPERTS - 1).astype(jnp.int32)
    n_valid = (ends[-1:] // MOE_TILE).astype(jnp.int32)
    xs = _dispatch(h, pos, n_tiles * MOE_TILE, tm=DISPATCH_ROWS)
    ys = _experts(xs, tile_expert, n_valid, w_gate, w_up, w_down, layer)
    return _combine(x, route, pos, ys, tm=COMBINE_ROWS)


GDN_HALO = 16


def _gdn_conv_kernel(u_ref, up_ref, un_ref, w_ref, ab_ref, aexp_ref, dtb_ref, o_ref, gb_ref, *, tiles_per_seq):
    i = pl.program_id(0)
    j = pl.program_id(1)
    tm = u_ref.shape[0]
    first = (i % tiles_per_seq) == 0
    last = (i % tiles_per_seq) == tiles_per_seq - 1
    u = u_ref[...].astype(F32)
    prev = jnp.where(first, 0.0, up_ref[...].astype(F32))
    nxt = jnp.where(last, 0.0, un_ref[...].astype(F32))
    ext = jnp.concatenate([prev, u, nxt], axis=0)
    w = w_ref[...]
    h0 = GDN_HALO - B_CONV // 2
    y = w[0:1] * ext[h0:h0 + tm]
    for t in range(1, B_CONV):
        y = y + w[t:t + 1] * ext[h0 + t:h0 + t + tm]
    y = y * jax.nn.sigmoid(y)
    qscale = jnp.where(j == 0, B_HEAD_DIM ** -0.5, 1.0)
    for h in range(B_HEADS):
        slab = y[:, h * B_HEAD_DIM:(h + 1) * B_HEAD_DIM]
        inv = lax.rsqrt(jnp.sum(slab * slab, axis=-1, keepdims=True) + RMS_EPS) * qscale
        o_ref[0, :, h * B_HEAD_DIM:(h + 1) * B_HEAD_DIM] = slab * jnp.where(j < 2, inv, 1.0)

    @pl.when(j == 0)
    def _():
        ab = ab_ref[...]
        lane = lax.broadcasted_iota(jnp.int32, ab.shape, 1)
        z = ab + dtb_ref[...]
        softplus = jnp.maximum(z, 0.0) + jnp.log1p(jnp.exp(-jnp.abs(z)))
        gb_ref[...] = jnp.where(lane < 2 * B_HEADS, -aexp_ref[...] * softplus, jax.nn.sigmoid(ab))


def _gdn_conv(qkvz, ab, conv_w, a_log, dt_bias, seq, tm):
    T = qkvz.shape[0]
    D = B_HEADS * B_HEAD_DIM
    hb = tm // GDN_HALO
    n_halo = T // GDN_HALO
    pad = ROUTER_LANES - 2 * B_HEADS
    aexp = jnp.concatenate([jnp.exp(a_log.astype(F32)).reshape(-1), jnp.zeros((pad,), F32)]).reshape(1, -1)
    dtb = jnp.concatenate([dt_bias.astype(F32).reshape(-1), jnp.zeros((pad,), F32)]).reshape(1, -1)
    return pl.pallas_call(
        functools.partial(_gdn_conv_kernel, tiles_per_seq=seq // tm),
        grid=(T // tm, 3),
        in_specs=[pl.BlockSpec((tm, D), lambda i, j: (i, j)),
                  pl.BlockSpec((GDN_HALO, D), lambda i, j: (jnp.maximum(i * hb - 1, 0), j)),
                  pl.BlockSpec((GDN_HALO, D), lambda i, j: (jnp.minimum((i + 1) * hb, n_halo - 1), j)),
                  pl.BlockSpec((B_CONV, D), lambda i, j: (0, j)),
                  pl.BlockSpec((tm, ROUTER_LANES), lambda i, j: (i, 0)),
                  pl.BlockSpec((1, ROUTER_LANES), lambda i, j: (0, 0)),
                  pl.BlockSpec((1, ROUTER_LANES), lambda i, j: (0, 0))],
        out_specs=[pl.BlockSpec((1, tm, D), lambda i, j: (j, i, 0)),
                   pl.BlockSpec((tm, ROUTER_LANES), lambda i, j: (i, 0))],
        out_shape=[jax.ShapeDtypeStruct((3, T, D), F32), jax.ShapeDtypeStruct((T, ROUTER_LANES), F32)],
        compiler_params=_cparams("parallel", "arbitrary"),
        name="gdn_conv",
    )(qkvz, qkvz, qkvz, conv_w, ab, aexp, dtb)


def _gdn_gate_terms(gb, incl):
    C = gb.shape[0]
    lane = lax.broadcasted_iota(jnp.int32, gb.shape, 1)
    g_hi, g_lo = _split_bf16(jnp.where(lane < 2 * B_HEADS, gb, 0.0))
    tri = incl.astype(BF16)
    gc = jnp.dot(tri, g_hi, preferred_element_type=F32) + jnp.dot(tri, g_lo, preferred_element_type=F32)
    return gc, jnp.concatenate([gc, jnp.zeros_like(gc)], axis=0).T


def _gdn_scan_kernel(qf_ref, kf_ref, vf_ref, gf_ref, qb_ref, kb_ref, vb_ref, gbw_ref, of_ref, ob_ref, sf_ref, sb_ref):
    @pl.when(pl.program_id(1) == 0)
    def _():
        sf_ref[...] = jnp.zeros_like(sf_ref)
        sb_ref[...] = jnp.zeros_like(sb_ref)

    C = GDN_CHUNK
    n_chunks = gf_ref.shape[0] // C
    dk = B_HEAD_DIM
    row = lax.broadcasted_iota(jnp.int32, (C, C), 0)
    colm = lax.broadcasted_iota(jnp.int32, (C, C), 1)
    tri = (row >= colm, row <= colm)
    row2 = lax.broadcasted_iota(jnp.int32, (C, 2 * C), 0)
    col2 = lax.broadcasted_iota(jnp.int32, (C, 2 * C), 1)
    left = col2 < C
    incl = (left & (row2 >= col2), left & (row2 <= col2))
    strict = (left & (row2 > col2), left & (row2 < col2))
    eye_right = (col2 == row2 + C).astype(F32)
    qkv_refs = ((qf_ref, kf_ref, vf_ref), (qb_ref, kb_ref, vb_ref))
    s_refs = (sf_ref, sb_ref)
    o_refs = (of_ref, ob_ref)

    def chunk_step(c, carry):
        rows = (pl.ds(pl.multiple_of(c * C, C), C), pl.ds(pl.multiple_of((n_chunks - 1 - c) * C, C), C))
        _gdn_chunk_pair(rows, (gf_ref, gbw_ref), qkv_refs, s_refs, o_refs, tri, incl, strict, left, eye_right)
        return carry

    lax.fori_loop(0, n_chunks, chunk_step, 0)


def _gdn_chunk_pair(rows, g_refs, qkv_refs, s_refs, o_refs, tri, incl, strict, left, eye_right):
    C = GDN_CHUNK
    dk = B_HEAD_DIM
    gbs = tuple(g_refs[d][rows[d], :] for d in range(2))
    gates = [_gdn_gate_terms(gbs[d], tri[d]) for d in range(2)]
    glast = [gates[0][0][C - 1:C], gates[1][0][0:1]]
    units = [(d, h) for d in range(2) for h in range(B_HEADS)]

    def lane_of(d, h):
        return d * B_HEADS + h

    def cols(h):
        return slice(h * dk, (h + 1) * dk)

    v_b, kb_l, qd_bf, kd_bf, a_l, dec_l, egc_l = [], [], [], [], [], [], []
    for d, h in units:
        r = lane_of(d, h)
        gc, gct = gates[d]
        gcol = gc[:, r:r + 1]
        beta = gbs[d][:, 2 * B_HEADS + r:2 * B_HEADS + r + 1]
        q_ref, k_ref, v_ref = qkv_refs[d]
        qh, kh, vh = q_ref[0, rows[d], cols(h)], k_ref[0, rows[d], cols(h)], v_ref[0, rows[d], cols(h)]
        egc = jnp.exp(gcol)
        kb = kh * beta
        khb = jnp.concatenate([kh.astype(BF16), jnp.zeros((C, dk), BF16)], axis=0)
        a_l.append(_bdot_nt(jnp.concatenate([kb, qh], axis=0), khb))
        dec_l.append(jnp.exp(jnp.where(incl[d], gcol - gct[r:r + 1, :], NEG_INF)))
        v_b.append(vh * beta)
        kb_l.append(kb)
        egc_l.append(egc)
        qd_bf.append((qh * egc).astype(BF16))
        kd_bf.append((kh * jnp.exp(glast[d][:, r:r + 1] - gcol)).astype(BF16))
    r_l = [eye_right - jnp.where(strict[d], a[:C] * dec, 0.0)
           for (d, h), a, dec in zip(units, a_l, dec_l)]
    intra_bf = [(a[C:] * dec)[:, :C].astype(BF16) for a, dec in zip(a_l, dec_l)]
    x_l = [r[:, :C] for r in r_l]
    n = 1
    while n < C:
        o_l = [_bdot(x, r) for x, r in zip(x_l, r_l)]
        r_l = [o + jnp.where(left, 0.0, r) for o, r in zip(o_l, r_l)]
        x_l = [o[:, :C] for o in o_l]
        n *= 2
    zpad = jnp.zeros((C, 2 * dk), BF16)
    sol_l = [_bdot(r, jnp.concatenate([zpad, jnp.concatenate([vb, kb * egc], axis=1).astype(BF16)], axis=0))
             for r, vb, kb, egc in zip(r_l, v_b, kb_l, egc_l)]
    st_l = [s_refs[d][h] for d, h in units]
    wq_l = [_bdot(jnp.concatenate([sol[:, dk:].astype(BF16), qd], axis=0), st)
            for sol, qd, st in zip(sol_l, qd_bf, st_l)]
    vn_l = [sol[:, :dk] - wq[:C] for sol, wq in zip(sol_l, wq_l)]
    for (d, h), wq, intra, vn in zip(units, wq_l, intra_bf, vn_l):
        o_refs[d][rows[d], cols(h)] = (wq[C:] + _bdot(intra, vn)).astype(o_refs[d].dtype)
    for (d, h), st, kd, vn in zip(units, st_l, kd_bf, vn_l):
        r = lane_of(d, h)
        s_refs[d][h] = st * jnp.exp(glast[d][:, r:r + 1]) + _bdot_tn(kd, vn)


def _gdn_scan(qkv, gb, batch, seq, rows_per_step):
    _, T, D = qkv.shape
    chunk = rows_per_step
    nc = seq // chunk
    fwd = lambda b, c: b * nc + c
    bwd = lambda b, c: b * nc + (nc - 1 - c)
    part = lambda p, f: pl.BlockSpec((1, chunk, D), lambda b, c: (p, f(b, c), 0))
    gspec = lambda f: pl.BlockSpec((chunk, ROUTER_LANES), lambda b, c: (f(b, c), 0))
    ospec = lambda f: pl.BlockSpec((chunk, D), lambda b, c: (f(b, c), 0))
    return pl.pallas_call(
        _gdn_scan_kernel,
        grid=(batch, nc),
        in_specs=[part(0, fwd), part(1, fwd), part(2, fwd), gspec(fwd),
                  part(0, bwd), part(1, bwd), part(2, bwd), gspec(bwd)],
        out_specs=[ospec(fwd), ospec(bwd)],
        out_shape=[jax.ShapeDtypeStruct((T, D), BF16), jax.ShapeDtypeStruct((T, D), BF16)],
        scratch_shapes=[pltpu.VMEM((B_HEADS, B_HEAD_DIM, B_HEAD_DIM), F32),
                        pltpu.VMEM((B_HEADS, B_HEAD_DIM, B_HEAD_DIM), F32)],
        compiler_params=_cparams("parallel", "arbitrary"),
        name="gdn_scan",
    )(qkv, qkv, qkv, gb, qkv, qkv, qkv, gb)


def _gdn_out_kernel(of_ref, ob_ref, z_ref, og_ref, w_ref, x_ref, o_ref, a_ref):
    @pl.when(pl.program_id(1) == 0)
    def _():
        o = of_ref[...].astype(F32) + ob_ref[...].astype(F32)
        z = z_ref[...].astype(F32)
        for h in range(B_HEADS):
            sl = slice(h * B_HEAD_DIM, (h + 1) * B_HEAD_DIM)
            zh = z[:, sl]
            a_ref[:, sl] = (_rms(o[:, sl], og_ref[...]) * (zh * jax.nn.sigmoid(zh))).astype(BF16)

    o_ref[...] = x_ref[...] + jnp.dot(a_ref[...], w_ref[...], preferred_element_type=F32)


def _gdn_out(o_f, o_b, qkvz, o_gain, w, x, tm, tn):
    T, D = x.shape
    return pl.pallas_call(
        _gdn_out_kernel,
        grid=(T // tm, D // tn),
        in_specs=[pl.BlockSpec((tm, D), lambda i, j: (i, 0)),
                  pl.BlockSpec((tm, D), lambda i, j: (i, 0)),
                  pl.BlockSpec((tm, D), lambda i, j: (i, 3)),
                  pl.BlockSpec((1, B_HEAD_DIM), lambda i, j: (0, 0)),
                  pl.BlockSpec((D, tn), lambda i, j: (0, j)),
                  pl.BlockSpec((tm, tn), lambda i, j: (i, j))],
        out_specs=pl.BlockSpec((tm, tn), lambda i, j: (i, j)),
        out_shape=jax.ShapeDtypeStruct((T, D), F32),
        scratch_shapes=[pltpu.VMEM((tm, D), BF16)],
        compiler_params=_cparams("parallel", "arbitrary"),
        name="gdn_out",
    )(o_f, o_b, qkvz, o_gain.reshape(1, B_HEAD_DIM), w, x)


def _attention_layer(x, gain, w_in, q_gain, k_gain, sink, w_out, batch, seq):
    qkv = _norm_matmul(x, gain, w_in.astype(BF16), *QKV_PROJ_TILE)
    a = _attention(qkv, q_gain, k_gain, sink, batch, seq)
    return _matmul_residual(a, w_out.astype(BF16), x, *ATTN_OUT_TILE)


def _gdn_layer(x, gain, w_in, conv_w, a_log, dt_bias, o_gain, w_out, batch, seq):
    D = x.shape[1]
    w_main = w_in[:, :4 * D].astype(BF16)
    pad = ROUTER_LANES - 4 * B_HEADS
    w_ab = jnp.concatenate([w_in[:, 4 * D:], jnp.zeros((D, pad), F32)], axis=1)
    qkvz, ab = _norm_matmul2(x, gain, w_main, w_ab, *GDN_PROJ_TILE)
    qkv, gb = _gdn_conv(qkvz, ab, conv_w, a_log, dt_bias, seq, GDN_CONV_ROWS)
    o_f, o_b = _gdn_scan(qkv, gb, batch, seq, GDN_SCAN_CHUNKS * GDN_CHUNK)
    return _gdn_out(o_f, o_b, qkvz, o_gain, w_out.astype(BF16), x, *GDN_OUT_TILE)


def kernel(x, norm_mix, norm_ffn, attn_w_in, attn_q_gain, attn_k_gain, attn_sink, attn_w_out, gdn_w_in, gdn_conv, gdn_a_log, gdn_dt_bias, gdn_o_gain, gdn_w_out, moe_w_group, moe_b_group, moe_w_expert, moe_b_expert, moe_w_gate, moe_w_up, moe_w_down):
    batch, seq, d_model = x.shape
    depth = norm_mix.shape[0]
    xt = x.reshape(batch * seq, d_model)
    for i in range(depth):
        j = i // 2
        if i % 2 == 0:
            xt = _attention_layer(xt, norm_mix[i], attn_w_in[j], attn_q_gain[j], attn_k_gain[j],
                                  attn_sink[j], attn_w_out[j], batch, seq)
        else:
            xt = _gdn_layer(xt, norm_mix[i], gdn_w_in[j], gdn_conv[j], gdn_a_log[j], gdn_dt_bias[j],
                            gdn_o_gain[j], gdn_w_out[j], batch, seq)
        xt = _moe(xt, norm_ffn[i], moe_w_group[i], moe_b_group[i], moe_w_expert[i], moe_b_expert[i],
                  moe_w_gate, moe_w_up, moe_w_down, layer=i)
    return xt.reshape(batch, seq, d_model)
```
